```python
import math
import jax, jax.numpy as jnp
from jax import lax
import numpy as np

D_MODEL = 1024
BATCH = 8
SEQ = 4096
DEPTH = 1

HEAD_DIM = 64
N_HEADS_MOBA = 8
N_HEADS_DSA = 8
ROT_DIM = HEAD_DIM // 4
ROPE_THETA = 500000.0
MOBA_BLOCK = 256
MOBA_TOPK = 3
MOBA_Q_CHUNK = 16
IDX_HEADS = 8
IDX_DIM = 64
DSA_TOPK_MAX = 256
DSA_Q_CHUNK = 64
N_GROUPS = 4
EXPERTS_PER_GROUP = 8
N_EXPERTS = N_GROUPS * EXPERTS_PER_GROUP
EXPERT_TOPK = 2
D_EXPERT = 512
MOE_ROWS = 256
PLE_DIM = 256
RMS_EPS = 1e-6

A_W = N_HEADS_MOBA * HEAD_DIM
B_W = N_HEADS_DSA * HEAD_DIM
IQ_W = IDX_HEADS * IDX_DIM
IN_SPLITS = (A_W, A_W, A_W, B_W, B_W, B_W, IQ_W, IDX_DIM, IDX_HEADS, D_MODEL, D_MODEL)
IN_COLS = sum(IN_SPLITS)

kernel_name = "hybrid_moba_dsa_hiermoe_ple"


def rms_norm(x, g):
    xf = x.astype(jnp.float32)
    y = xf * lax.rsqrt(jnp.mean(xf * xf, axis=-1, keepdims=True) + RMS_EPS)
    return (y * g.astype(jnp.float32)).astype(x.dtype)


def rope_tables(seq):
    inv = 1.0 / (ROPE_THETA ** (jnp.arange(0, ROT_DIM, 2, dtype=jnp.float32) / ROT_DIM))
    ang = jnp.arange(seq, dtype=jnp.float32)[:, None] * inv[None, :]
    return jnp.cos(ang), jnp.sin(ang)


def apply_rope(x, cos, sin):
    half = ROT_DIM // 2
    xf = x.astype(jnp.float32)
    x1 = xf[..., :half]
    x2 = xf[..., half:ROT_DIM]
    c = cos[None, :, None, :]
    s = sin[None, :, None, :]
    out = jnp.concatenate([x1 * c - x2 * s, x2 * c + x1 * s, xf[..., ROT_DIM:]], axis=-1)
    return out.astype(x.dtype)


def moba_attention(q, k, v):
    b, s, h, dh = q.shape
    n_blk = -(-s // MOBA_BLOCK)
    pad = n_blk * MOBA_BLOCK - s
    kp = jnp.pad(k, ((0, 0), (0, pad), (0, 0), (0, 0)))
    vp = jnp.pad(v, ((0, 0), (0, pad), (0, 0), (0, 0)))
    k_bh = kp.reshape(b, n_blk, MOBA_BLOCK, h, dh).transpose(0, 3, 1, 2, 4)
    v_bh = vp.reshape(b, n_blk, MOBA_BLOCK, h, dh).transpose(0, 3, 1, 2, 4)
    k_mean = jnp.mean(k_bh.astype(jnp.float32), axis=3)
    n_sel = min(MOBA_TOPK, n_blk - 1)
    scale = HEAD_DIM ** -0.5
    n_chunks = s // MOBA_Q_CHUNK
    qc = q.reshape(b, n_chunks, MOBA_Q_CHUNK, h, dh).transpose(1, 0, 2, 3, 4)
    bi = jnp.arange(b)[:, None, None, None]
    hi = jnp.arange(h)[None, None, :, None]
    kpos_in_blk = jnp.arange(MOBA_BLOCK)

    def chunk_fn(args):
        q_c, c = args
        t0 = c * MOBA_Q_CHUNK
        pos = t0 + jnp.arange(MOBA_Q_CHUNK)
        cur = t0 // MOBA_BLOCK
        k_own = lax.dynamic_index_in_dim(k_bh, cur, axis=2, keepdims=False)
        v_own = lax.dynamic_index_in_dim(v_bh, cur, axis=2, keepdims=False)
        s_own = jnp.einsum('bqhd,bhkd->bqhk', q_c, k_own,
                           preferred_element_type=jnp.float32) * scale
        causal = (cur * MOBA_BLOCK + kpos_in_blk)[None, :] <= pos[:, None]
        s_own = jnp.where(causal[None, :, None, :], s_own, -jnp.inf)
        if n_sel > 0:
            gate = jnp.einsum('bqhd,bhnd->bqhn', q_c.astype(jnp.float32), k_mean)
            past = jnp.arange(n_blk) < cur
            gate = jnp.where(past[None, None, None, :], gate, -jnp.inf)
            _, sel = lax.top_k(gate, n_sel)
            sel_ok = sel < cur
            k_sel = k_bh[bi, hi, sel]
            v_sel = v_bh[bi, hi, sel]
            s_past = jnp.einsum('bqhd,bqhnkd->bqhnk', q_c, k_sel,
                                preferred_element_type=jnp.float32) * scale
            s_past = jnp.where(sel_ok[..., None], s_past, -jnp.inf)
            s_past = s_past.reshape(b, MOBA_Q_CHUNK, h, n_sel * MOBA_BLOCK)
            probs = jax.nn.softmax(jnp.concatenate([s_past, s_own], axis=-1), axis=-1)
            p_past = probs[..., :n_sel * MOBA_BLOCK].reshape(b, MOBA_Q_CHUNK, h, n_sel, MOBA_BLOCK)
            p_own = probs[..., n_sel * MOBA_BLOCK:]
            o = (jnp.einsum('bqhnk,bqhnkd->bqhd', p_past, v_sel.astype(jnp.float32))
                 + jnp.einsum('bqhk,bhkd->bqhd', p_own, v_own.astype(jnp.float32)))
        else:
            p_own = jax.nn.softmax(s_own, axis=-1)
            o = jnp.einsum('bqhk,bhkd->bqhd', p_own, v_own.astype(jnp.float32))
        return o.astype(q.dtype)

    out = lax.map(chunk_fn, (qc, jnp.arange(n_chunks)))
    return out.transpose(1, 0, 2, 3, 4).reshape(b, s, h, dh)


def dsa_attention(q, k, v, iq, ik, iw):
    b, s, h, dh = q.shape
    topk = min(DSA_TOPK_MAX, s // 4)
    scale = HEAD_DIM ** -0.5
    idx_scale = IDX_DIM ** -0.5
    w_scale = IDX_HEADS ** -0.5
    n_chunks = s // DSA_Q_CHUNK

    def to_chunks(a):
        return a.reshape((b, n_chunks, DSA_Q_CHUNK) + a.shape[2:]).swapaxes(0, 1)

    key_pos = jnp.arange(s)
    bi = jnp.arange(b)[:, None, None]

    def chunk_fn(args):
        q_c, iq_c, iw_c, c = args
        pos = c * DSA_Q_CHUNK + jnp.arange(DSA_Q_CHUNK)
        rel = jnp.einsum('bqhd,bsd->bhqs', iq_c, ik,
                         preferred_element_type=jnp.float32) * idx_scale
        score = jnp.einsum('bhqs,bqh->bqs', jax.nn.relu(rel),
                           iw_c.astype(jnp.float32) * w_scale)
        admissible = key_pos[None, :] <= pos[:, None]
        score = jnp.where(admissible[None], score, -jnp.inf)
        _, idx = lax.top_k(score, topk)
        ok = idx <= pos[None, :, None]
        k_sel = k[bi, idx]
        v_sel = v[bi, idx]
        logits = jnp.einsum('bqhd,bqkhd->bqhk', q_c, k_sel,
                            preferred_element_type=jnp.float32) * scale
        logits = jnp.where(ok[:, :, None, :], logits, -jnp.inf)
        probs = jax.nn.softmax(logits, axis=-1)
        o = jnp.einsum('bqhk,bqkhd->bqhd', probs, v_sel.astype(jnp.float32))
        return o.astype(q.dtype)

    out = lax.map(chunk_fn, (to_chunks(q), to_chunks(iq), to_chunks(iw), jnp.arange(n_chunks)))
    return out.swapaxes(0, 1).reshape(b, s, h, dh)


def hier_moe(h, w_grp, b_grp, w_rt, b_rt, w_gate, w_up, w_down):
    b, s, d = h.shape
    t = b * s
    hf = h.reshape(t, d)
    grp_logits = jnp.matmul(hf, w_grp).astype(jnp.float32) + b_grp.astype(jnp.float32)
    grp_prob = jax.nn.softmax(grp_logits, axis=-1)
    g = jnp.argmax(grp_logits, axis=-1)
    p_g = jnp.take_along_axis(grp_prob, g[:, None], axis=-1)[:, 0]
    exp_logits = (jnp.matmul(hf, w_rt).astype(jnp.float32) + b_rt.astype(jnp.float32))
    exp_logits = exp_logits.reshape(t, N_GROUPS, EXPERTS_PER_GROUP)
    in_grp = jnp.take_along_axis(exp_logits, g[:, None, None], axis=1)[:, 0]
    top_val, top_idx = lax.top_k(in_grp, EXPERT_TOPK)
    combine = p_g[:, None] * jax.nn.softmax(top_val, axis=-1)
    expert_id = g[:, None] * EXPERTS_PER_GROUP + top_idx

    a = t * EXPERT_TOPK
    e_flat = expert_id.reshape(a)
    tok_flat = jnp.repeat(jnp.arange(t), EXPERT_TOPK)
    w_flat = combine.reshape(a)
    order = jnp.argsort(e_flat)
    e_s, tok_s, w_s = e_flat[order], tok_flat[order], w_flat[order]
    counts = jnp.bincount(e_flat, length=N_EXPERTS)
    padded = ((counts + MOE_ROWS - 1) // MOE_ROWS) * MOE_ROWS
    start = jnp.cumsum(counts) - counts
    pend = jnp.cumsum(padded)
    pstart = pend - padded
    dest = pstart[e_s] + (jnp.arange(a) - start[e_s])
    n_blocks = -(-a // MOE_ROWS) + N_EXPERTS
    rows = n_blocks * MOE_ROWS
    buf_tok = jnp.full((rows,), t, dtype=jnp.int32).at[dest].set(tok_s.astype(jnp.int32))
    buf_w = jnp.zeros((rows,), jnp.float32).at[dest].set(w_s)
    blk_exp = jnp.minimum(jnp.searchsorted(pend, jnp.arange(n_blocks) * MOE_ROWS, side='right'),
                          N_EXPERTS - 1)
    h_pad = jnp.concatenate([hf, jnp.zeros((1, d), hf.dtype)], axis=0)

    def block_fn(args):
        tok_b, e_b = args
        xb = h_pad[tok_b]
        hid = jax.nn.silu(jnp.matmul(xb, w_gate[e_b])) * jnp.matmul(xb, w_up[e_b])
        return jnp.matmul(hid, w_down[e_b])

    y = lax.map(block_fn, (buf_tok.reshape(n_blocks, MOE_ROWS), blk_exp))
    y = y.reshape(rows, d).astype(jnp.float32) * buf_w[:, None]
    out = jnp.zeros((t + 1, d), jnp.float32).at[buf_tok].add(y)[:t]
    return out.reshape(b, s, d).astype(h.dtype)


def hybrid_layer(x, p_i, cos, sin, g_attn, w_in, w_br_a, w_br_b, w_out, g_ffn,
                 w_grp, b_grp, w_rt, b_rt, w_gate, w_up, w_down, g_ple, w_ple_gate, w_ple_proj):
    b, s, d = x.shape
    h = rms_norm(x, g_attn)
    z = jnp.matmul(h, w_in)
    offs = np.cumsum(IN_SPLITS)[:-1].tolist()
    qa, ka, va, qb, kb, vb, iq, ik, iw, ga, gb = jnp.split(z, offs, axis=-1)
    heads = lambda a, n: a.reshape(b, s, n, HEAD_DIM)
    qa = apply_rope(heads(qa, N_HEADS_MOBA), cos, sin)
    ka = apply_rope(heads(ka, N_HEADS_MOBA), cos, sin)
    va = heads(va, N_HEADS_MOBA)
    qb = apply_rope(heads(qb, N_HEADS_DSA), cos, sin)
    kb = apply_rope(heads(kb, N_HEADS_DSA), cos, sin)
    vb = heads(vb, N_HEADS_DSA)
    iq = apply_rope(iq.reshape(b, s, IDX_HEADS, IDX_DIM), cos, sin)
    ik = apply_rope(ik[:, :, None, :], cos, sin)[:, :, 0, :]
    o_a = moba_attention(qa, ka, va).reshape(b, s, A_W)
    o_b = dsa_attention(qb, kb, vb, iq, ik, iw).reshape(b, s, B_W)
    merged = (jax.nn.sigmoid(ga) * jnp.matmul(o_a, w_br_a)
              + jax.nn.sigmoid(gb) * jnp.matmul(o_b, w_br_b))
    x = x + jnp.matmul(merged, w_out)
    x = x + hier_moe(rms_norm(x, g_ffn), w_grp, b_grp, w_rt, b_rt, w_gate, w_up, w_down)
    gate = jax.nn.sigmoid(jnp.matmul(rms_norm(x, g_ple), w_ple_gate))
    x = x + gate * jnp.matmul(p_i, w_ple_proj)
    return x


def setup_inputs(seed: int = 0) -> dict:
    key = jax.random.key(seed)
    ks = jax.random.split(key, 20)
    f32 = jnp.float32
    nrm = lambda k, shape, fan_in: jax.random.normal(k, shape, f32) * (fan_in ** -0.5)
    gain = lambda k: 1.0 + 0.05 * jax.random.normal(k, (DEPTH, D_MODEL), f32)
    return {
        "x": jax.random.normal(ks[0], (BATCH, SEQ, D_MODEL), f32),
        "p": jax.random.normal(ks[1], (DEPTH, BATCH, SEQ, PLE_DIM), f32),
        "g_attn": gain(ks[2]),
        "w_in": nrm(ks[3], (DEPTH, D_MODEL, IN_COLS), D_MODEL),
        "w_br_a": nrm(ks[4], (DEPTH, A_W, D_MODEL), A_W),
        "w_br_b": nrm(ks[5], (DEPTH, B_W, D_MODEL), B_W),
        "w_out": nrm(ks[6], (DEPTH, D_MODEL, D_MODEL), D_MODEL),
        "g_ffn": gain(ks[7]),
        "w_grp": nrm(ks[8], (DEPTH, D_MODEL, N_GROUPS), D_MODEL),
        "b_grp": 0.01 * jax.random.normal(ks[9], (DEPTH, N_GROUPS), f32),
        "w_rt": nrm(ks[10], (DEPTH, D_MODEL, N_EXPERTS), D_MODEL),
        "b_rt": 0.01 * jax.random.normal(ks[11], (DEPTH, N_EXPERTS), f32),
        "w_gate": nrm(ks[12], (DEPTH, N_EXPERTS, D_MODEL, D_EXPERT), D_MODEL),
        "w_up": nrm(ks[13], (DEPTH, N_EXPERTS, D_MODEL, D_EXPERT), D_MODEL),
        "w_down": nrm(ks[14], (DEPTH, N_EXPERTS, D_EXPERT, D_MODEL), D_EXPERT),
        "g_ple": gain(ks[15]),
        "w_ple_gate": nrm(ks[16], (DEPTH, D_MODEL, D_MODEL), D_MODEL),
        "w_ple_proj": nrm(ks[17], (DEPTH, PLE_DIM, D_MODEL), PLE_DIM),
        "g_final": 1.0 + 0.05 * jax.random.normal(ks[18], (D_MODEL,), f32),
    }


def reference(x, p, g_attn, w_in, w_br_a, w_br_b, w_out, g_ffn, w_grp, b_grp, w_rt, b_rt,
              w_gate, w_up, w_down, g_ple, w_ple_gate, w_ple_proj, g_final):
    cos, sin = rope_tables(x.shape[1])
    for i in range(DEPTH):
        x = hybrid_layer(x, p[i], cos, sin, g_attn[i], w_in[i], w_br_a[i], w_br_b[i], w_out[i],
                         g_ffn[i], w_grp[i], b_grp[i], w_rt[i], b_rt[i], w_gate[i], w_up[i],
                         w_down[i], g_ple[i], w_ple_gate[i], w_ple_proj[i])
    return rms_norm(x, g_final)
```

```python
import functools

import jax
import jax.numpy as jnp
from jax import lax
from jax.experimental import pallas as pl
from jax.experimental.pallas import tpu as pltpu

F32 = jnp.float32
BF16 = jnp.bfloat16
I32 = jnp.int32

HEAD_DIM = 64
N_HEADS = 8
ROT_DIM = HEAD_DIM // 4
ROT_HALF = ROT_DIM // 2
ROPE_THETA = 500000.0
MOBA_BLOCK = 256
MOBA_TOPK = 3
IDX_HEADS = 8
IDX_DIM = 64
DSA_TOPK_MAX = 256
N_GROUPS = 4
EXPERTS_PER_GROUP = 8
N_EXPERTS = N_GROUPS * EXPERTS_PER_GROUP
D_EXPERT = 512
PLE_DIM = 256
RMS_EPS = 1e-6
MIXER_W = N_HEADS * HEAD_DIM

Q_TILE = 256
LANES = 128
SUBLANES = 8
VMEM_LIMIT = 56 * 1024 * 1024

NT_DIMS = (((1,), (1,)), ((), ()))


def _cparams(sem):
    return pltpu.CompilerParams(dimension_semantics=sem, vmem_limit_bytes=VMEM_LIMIT)


def _const_spec(shape):
    nd = len(shape)
    return pl.BlockSpec(shape, lambda *_: (0,) * nd, pipeline_mode=pl.Buffered(1))


def _rope_feature_major(z, cos_t, sin_t):
    tm = z.shape[1]
    z3 = z.reshape(N_HEADS, HEAD_DIM, tm)
    x1 = z3[:, 0:ROT_HALF, :]
    x2 = z3[:, ROT_HALF:ROT_DIM, :]
    o1 = x1 * cos_t - x2 * sin_t
    o2 = x2 * cos_t + x1 * sin_t
    return jnp.concatenate([o1, o2, z3[:, ROT_DIM:, :]], axis=1).reshape(N_HEADS * HEAD_DIM, tm)


def _rope_token_major(z, c_tab, s_lo, s_hi):
    up = pltpu.roll(z, LANES - ROT_HALF, 1)
    dn = pltpu.roll(z, ROT_HALF, 1)
    return z * c_tab + up * s_lo + dn * s_hi


def _store_blocked(ref, z):
    for c in range(z.shape[1] // Q_TILE):
        ref[0, c] = z[:, c * Q_TILE:(c + 1) * Q_TILE]


def _inproj_kernel(x_ref, g_ref, wf_ref, wiw_ref, wt_ref, wg_ref, cos_t_ref, sin_t_ref,
                   ctab_ref, slo_ref, shi_ref,
                   qa_t_ref, va_t_ref, qb_t_ref, vb_t_ref, iq_t_ref, iw_t_ref,
                   ka_ref, kb_ref, ik_ref, ga_ref, gb_ref, km_ref, *, w_scale):
    x = x_ref[0]
    ms = jnp.mean(x * x, axis=-1, keepdims=True)
    h = ((x * lax.rsqrt(ms + RMS_EPS)) * g_ref[...]).astype(BF16)
    cos_t = cos_t_ref[...]
    sin_t = sin_t_ref[...]
    q_scale = HEAD_DIM ** -0.5

    fm_outs = ((qa_t_ref, True, q_scale), (va_t_ref, False, 1.0), (qb_t_ref, True, q_scale),
               (vb_t_ref, False, 1.0), (iq_t_ref, True, IDX_DIM ** -0.5))
    for i, (ref, rope, scale) in enumerate(fm_outs):
        z = lax.dot_general(wf_ref[i * MIXER_W:(i + 1) * MIXER_W, :], h, NT_DIMS,
                            preferred_element_type=F32)
        if rope:
            z = _rope_feature_major(z, cos_t, sin_t)
        if scale != 1.0:
            z = z * scale
        _store_blocked(ref, z.astype(BF16))

    iw = lax.dot_general(wiw_ref[...], h, NT_DIMS, preferred_element_type=F32)
    _store_blocked(iw_t_ref, iw * w_scale)

    zt = jnp.dot(h, wt_ref[...], preferred_element_type=F32)
    ctab, slo, shi = ctab_ref[...], slo_ref[...], shi_ref[...]
    n_grp = zt.shape[1] // LANES
    roped = [_rope_token_major(zt[:, j * LANES:(j + 1) * LANES], ctab, slo, shi) for j in range(n_grp)]
    per_mixer = MIXER_W // LANES
    ka = jnp.concatenate(roped[:per_mixer], axis=1)
    kb = jnp.concatenate(roped[per_mixer:2 * per_mixer], axis=1)
    ka_ref[0] = ka.astype(BF16)
    kb_ref[0] = kb.astype(BF16)
    ik_ref[0] = roped[2 * per_mixer][:, :IDX_DIM].astype(BF16)
    tm = ka.shape[0]
    km_ref[0] = jnp.mean(ka.reshape(tm // MOBA_BLOCK, MOBA_BLOCK, MIXER_W), axis=1, keepdims=True)

    zg = jnp.dot(h, wg_ref[...], preferred_element_type=F32)
    d_model = zg.shape[1] // 2
    ga_ref[0] = zg[:, :d_model]
    gb_ref[0] = zg[:, d_model:]


def _rope_tables(seq):
    inv = 1.0 / (ROPE_THETA ** (jnp.arange(0, ROT_DIM, 2, dtype=F32) / ROT_DIM))
    ang = jnp.arange(seq, dtype=F32)[:, None] * inv[None, :]
    cos, sin = jnp.cos(ang), jnp.sin(ang)
    d = jnp.arange(LANES) % HEAD_DIM
    lo = d < ROT_HALF
    hi = (d >= ROT_HALF) & (d < ROT_DIM)
    f = d % ROT_HALF
    cos_l, sin_l = cos[:, f], sin[:, f]
    ctab = jnp.where(lo | hi, cos_l, 1.0)
    slo = jnp.where(lo, -sin_l, 0.0)
    shi = jnp.where(hi, sin_l, 0.0)
    return cos.T, sin.T, ctab, slo, shi


def _in_projection(x, g_attn, w_in, tm=512):
    b, s, d = x.shape
    splits = (MIXER_W,) * 6 + (IDX_HEADS * IDX_DIM, IDX_DIM, IDX_HEADS, d, d)
    offs = [0]
    for w in splits:
        offs.append(offs[-1] + w)
    wqa, wka, wva, wqb, wkb, wvb, wiq, wik, wiw, wga, wgb = (
        w_in[:, offs[i]:offs[i + 1]] for i in range(len(splits)))
    wf = jnp.concatenate([wqa, wva, wqb, wvb, wiq], axis=1).T.astype(BF16)
    wiw_t = wiw.T.astype(BF16)
    wt = jnp.concatenate([wka, wkb, wik, jnp.zeros((d, LANES - IDX_DIM), w_in.dtype)], axis=1).astype(BF16)
    wg = jnp.concatenate([wga, wgb], axis=1).astype(BF16)
    cos_t, sin_t, ctab, slo, shi = _rope_tables(s)
    n_blk = s // MOBA_BLOCK
    fm_shape = jax.ShapeDtypeStruct((b, s // Q_TILE, MIXER_W, Q_TILE), BF16)
    tok_shape = jax.ShapeDtypeStruct((b, s, MIXER_W), BF16)
    fm_spec = pl.BlockSpec((1, tm // Q_TILE, MIXER_W, Q_TILE), lambda bi, ti: (bi, ti, 0, 0))
    tok_spec = pl.BlockSpec((1, tm, MIXER_W), lambda bi, ti: (bi, ti, 0))
    gate_spec = pl.BlockSpec((1, tm, d), lambda bi, ti: (bi, ti, 0))
    outs = pl.pallas_call(
        functools.partial(_inproj_kernel, w_scale=IDX_HEADS ** -0.5),
        grid=(b, s // tm),
        in_specs=[
            pl.BlockSpec((1, tm, d), lambda bi, ti: (bi, ti, 0)),
            _const_spec((1, d)),
            _const_spec(wf.shape), _const_spec(wiw_t.shape), _const_spec(wt.shape), _const_spec(wg.shape),
            pl.BlockSpec((ROT_HALF, tm), lambda bi, ti: (0, ti)),
            pl.BlockSpec((ROT_HALF, tm), lambda bi, ti: (0, ti)),
            pl.BlockSpec((tm, LANES), lambda bi, ti: (ti, 0)),
            pl.BlockSpec((tm, LANES), lambda bi, ti: (ti, 0)),
            pl.BlockSpec((tm, LANES), lambda bi, ti: (ti, 0)),
        ],
        out_specs=[
            fm_spec, fm_spec, fm_spec, fm_spec, fm_spec,
            pl.BlockSpec((1, tm // Q_TILE, IDX_HEADS, Q_TILE), lambda bi, ti: (bi, ti, 0, 0)),
            tok_spec, tok_spec,
            pl.BlockSpec((1, tm, IDX_DIM), lambda bi, ti: (bi, ti, 0)),
            gate_spec, gate_spec,
            pl.BlockSpec((1, tm // MOBA_BLOCK, 1, MIXER_W), lambda bi, ti: (bi, ti, 0, 0)),
        ],
        out_shape=[
            fm_shape, fm_shape, fm_shape, fm_shape, fm_shape,
            jax.ShapeDtypeStruct((b, s // Q_TILE, IDX_HEADS, Q_TILE), F32),
            tok_shape, tok_shape,
            jax.ShapeDtypeStruct((b, s, IDX_DIM), BF16),
            jax.ShapeDtypeStruct((b, s, d), F32), jax.ShapeDtypeStruct((b, s, d), F32),
            jax.ShapeDtypeStruct((b, n_blk, 1, MIXER_W), F32),
        ],
        compiler_params=_cparams(("parallel", "parallel")),
        name="in_projection",
    )(x, g_attn.reshape(1, d), wf, wiw_t, wt, wg, cos_t, sin_t, ctab, slo, shi)
    return outs


MASKED = -1e30


def _one_head_of_pair(q_pair, hh):
    row = lax.broadcasted_iota(I32, q_pair.shape, 0)
    return jnp.where((row // HEAD_DIM) == hh, q_pair, jnp.zeros_like(q_pair))


def _softmax_step(s, v_t, m, l, acc):
    m_new = jnp.maximum(m, jnp.max(s, axis=0, keepdims=True))
    alpha = jnp.exp(m - m_new)
    p = jnp.exp(s - m_new)
    l_new = alpha * l + jnp.sum(p, axis=0, keepdims=True)
    acc_new = alpha * acc + jnp.dot(v_t, p.astype(BF16), preferred_element_type=F32)
    return m_new, l_new, acc_new


def _moba_kernel(q_t_ref, k_ref, v_t_ref, km_ref, o_ref, bias_ref):
    i = pl.program_id(2)
    tq = q_t_ref.shape[3]
    n_blk = km_ref.shape[1]
    q_pair = q_t_ref[0, 0]
    km = km_ref[0]
    blk = lax.broadcasted_iota(I32, (n_blk, tq), 0)
    kpos = lax.broadcasted_iota(I32, (tq, tq), 0)
    qpos = lax.broadcasted_iota(I32, (tq, tq), 1)
    outs = []
    for hh in range(LANES // HEAD_DIM):
        q_h = _one_head_of_pair(q_pair, hh)
        gate = jnp.dot(km, q_h.astype(F32), preferred_element_type=F32)
        beaten_by = jnp.zeros((n_blk, tq), I32)
        for m_blk in range(n_blk):
            gm = gate[m_blk:m_blk + 1, :]
            beats = ((gm > gate) | ((gm == gate) & (m_blk < blk))) & (m_blk < i)
            beaten_by = beaten_by + beats.astype(I32)
        keep = (blk < i) & (beaten_by < MOBA_TOPK)
        bias_ref[...] = jnp.where(keep, 0.0, MASKED)

        v_rows = slice(hh * HEAD_DIM, (hh + 1) * HEAD_DIM)
        k_own = k_ref[0, pl.ds(pl.multiple_of(i * tq, tq), tq), :]
        s_own = jnp.dot(k_own, q_h, preferred_element_type=F32)
        s_own = jnp.where(kpos <= qpos, s_own, MASKED)
        init = (jnp.full((1, tq), MASKED, F32), jnp.zeros((1, tq), F32), jnp.zeros((HEAD_DIM, tq), F32))
        carry = _softmax_step(s_own, v_t_ref[0, i, v_rows, :], *init)

        def past_block(j, carry):
            k_j = k_ref[0, pl.ds(pl.multiple_of(j * tq, tq), tq), :]
            s = jnp.dot(k_j, q_h, preferred_element_type=F32) + bias_ref[pl.ds(j, 1), :]
            return _softmax_step(s, v_t_ref[0, j, v_rows, :], *carry)

        m, l, acc = lax.fori_loop(0, i, past_block, carry)
        outs.append(acc / l)
    o_ref[0] = jnp.concatenate(outs, axis=0).T.astype(BF16)


def _moba_attention(qa_t, ka, va_t, km):
    b, n_blk, w, tq = qa_t.shape
    s = n_blk * tq
    n_pair = w // LANES
    return pl.pallas_call(
        _moba_kernel,
        grid=(b, n_pair, n_blk),
        in_specs=[
            pl.BlockSpec((1, 1, LANES, tq), lambda bi, pi, i: (bi, i, pi, 0)),
            pl.BlockSpec((1, s, LANES), lambda bi, pi, i: (bi, 0, pi)),
            pl.BlockSpec((1, n_blk, LANES, tq), lambda bi, pi, i: (bi, 0, pi, 0)),
            pl.BlockSpec((1, n_blk, LANES), lambda bi, pi, i: (bi, 0, pi)),
        ],
        out_specs=pl.BlockSpec((1, tq, LANES), lambda bi, pi, i: (bi, i, pi)),
        out_shape=jax.ShapeDtypeStruct((b, s, w), BF16),
        scratch_shapes=[pltpu.VMEM((n_blk, tq), F32)],
        compiler_params=_cparams(("parallel", "parallel", "arbitrary")),
        name="moba_attention",
    )(qa_t, ka, va_t, km)


INT_MIN = -2 ** 31


def _sortable_key(s):
    bits = lax.bitcast_convert_type(s, I32)
    return bits ^ ((bits >> 31) & 0x7FFFFFFF)


def _column_count(mask):
    tk, tq = mask.shape
    return jnp.sum(mask.astype(I32).reshape(tk // SUBLANES, SUBLANES, tq), axis=0)


def _dsa_kernel(iq_t_ref, ik_ref, iw_t_ref, q_t_ref, k_ref, v_t_ref, o_ref,
                key_ref, bias_ref, o_t_ref, *, topk, index_bits):
    t = pl.program_id(1)
    tq = q_t_ref.shape[3]
    tk = tq
    n_chunks = t + 1
    qpos = t * tq + lax.broadcasted_iota(I32, (1, tq), 1)
    krow = lax.broadcasted_iota(I32, (tk, tq), 0)

    def chunk_rows(c):
        return pl.ds(pl.multiple_of(c * tk, tk), tk)

    def total(count8):
        return jnp.sum(count8, axis=0, keepdims=True)

    def count_over_chunks(pred):
        def body(c, cnt):
            return cnt + _column_count(pred(key_ref[chunk_rows(c), :], c * tk + krow))
        return total(lax.fori_loop(0, n_chunks, body, jnp.zeros((SUBLANES, tq), I32)))

    iq = iq_t_ref[0, 0]
    iw = iw_t_ref[0, 0]

    def score_chunk(c, carry):
        ik_c = ik_ref[0, chunk_rows(c), :]
        score = jnp.zeros((tk, tq), F32)
        for h in range(IDX_HEADS):
            rel = jnp.dot(ik_c, iq[h * IDX_DIM:(h + 1) * IDX_DIM, :], preferred_element_type=F32)
            score = score + jnp.maximum(rel, 0.0) * iw[h:h + 1, :]
        key_ref[chunk_rows(c), :] = jnp.where(c * tk + krow <= qpos, _sortable_key(score), INT_MIN)
        return carry

    lax.fori_loop(0, n_chunks, score_chunk, 0)

    def value_bit(b, prefix):
        cand = prefix | jnp.left_shift(jnp.int32(1), 31 - b)
        cand_key = cand ^ INT_MIN
        n_ge = count_over_chunks(lambda keys, kpos: keys >= cand_key)
        return jnp.where(n_ge >= topk, cand, prefix)

    thr = lax.fori_loop(0, 32, value_bit, jnp.zeros((1, tq), I32)) ^ INT_MIN

    n_gt = count_over_chunks(lambda keys, kpos: keys > thr)
    n_eq = count_over_chunks(lambda keys, kpos: keys == thr)
    need = topk - n_gt
    has_surplus = jnp.any((n_eq > need) & (thr != INT_MIN))

    def last_tie_index():
        def index_bit(b, prefix):
            cand = prefix | jnp.left_shift(jnp.int32(1), index_bits - 1 - b)
            n_before = count_over_chunks(lambda keys, kpos: (keys == thr) & (kpos < cand))
            return jnp.where(n_before < need, cand, prefix)
        return lax.fori_loop(0, index_bits, index_bit, jnp.zeros((1, tq), I32))

    tie_end = lax.cond(has_surplus, last_tie_index, lambda: jnp.full((1, tq), 2 ** index_bits, I32))

    def bias_chunk(c, carry):
        keys = key_ref[chunk_rows(c), :]
        kpos = c * tk + krow
        chosen = ((keys > thr) | ((keys == thr) & (kpos <= tie_end))) & (kpos <= qpos)
        bias_ref[chunk_rows(c), :] = jnp.where(chosen, 0.0, MASKED)
        return carry

    lax.fori_loop(0, n_chunks, bias_chunk, 0)

    heads_per_group = LANES // HEAD_DIM
    for h in range(N_HEADS):
        pr, hh = divmod(h, heads_per_group)
        lanes = slice(pr * LANES, (pr + 1) * LANES)
        q_h = _one_head_of_pair(q_t_ref[0, 0, lanes, :], hh)
        v_rows = slice(h * HEAD_DIM, (h + 1) * HEAD_DIM)

        def attend_chunk(c, carry, lanes=lanes, q_h=q_h, v_rows=v_rows):
            s = jnp.dot(k_ref[0, chunk_rows(c), lanes], q_h, preferred_element_type=F32)
            return _softmax_step(s + bias_ref[chunk_rows(c), :], v_t_ref[0, c, v_rows, :], *carry)

        init = (jnp.full((1, tq), MASKED, F32), jnp.zeros((1, tq), F32), jnp.zeros((HEAD_DIM, tq), F32))
        m, l, acc = lax.fori_loop(0, n_chunks, attend_chunk, init)
        o_t_ref[v_rows, :] = acc / l
    o_ref[0] = o_t_ref[...].T.astype(BF16)


def _dsa_attention(iq_t, ik, iw_t, qb_t, kb, vb_t):
    b, n_blk, w, tq = qb_t.shape
    s = n_blk * tq
    topk = min(DSA_TOPK_MAX, s // 4)
    tile_spec = lambda rows: pl.BlockSpec((1, 1, rows, tq), lambda bi, ti: (bi, ti, 0, 0))
    return pl.pallas_call(
        functools.partial(_dsa_kernel, topk=topk, index_bits=max(1, (s - 1).bit_length())),
        grid=(b, n_blk),
        in_specs=[
            tile_spec(IDX_HEADS * IDX_DIM),
            pl.BlockSpec((1, s, IDX_DIM), lambda bi, ti: (bi, 0, 0)),
            tile_spec(IDX_HEADS),
            tile_spec(w),
            pl.BlockSpec((1, s, w), lambda bi, ti: (bi, 0, 0)),
            pl.BlockSpec((1, n_blk, w, tq), lambda bi, ti: (bi, 0, 0, 0)),
        ],
        out_specs=pl.BlockSpec((1, tq, w), lambda bi, ti: (bi, ti, 0)),
        out_shape=jax.ShapeDtypeStruct((b, s, w), BF16),
        scratch_shapes=[pltpu.VMEM((s, tq), I32), pltpu.VMEM((s, tq), F32), pltpu.VMEM((w, tq), F32)],
        compiler_params=_cparams(("parallel", "arbitrary")),
        name="dsa_attention",
    )(iq_t, ik, iw_t, qb_t, kb, vb_t)


ROUTER_ROWS = 8 + N_EXPERTS
EXPERT_TOPK = 2


def _rms_norm(x, g):
    ms = jnp.mean(x * x, axis=-1, keepdims=True)
    return (x * lax.rsqrt(ms + RMS_EPS)) * g


def _first_index_of_max(vals, idx):
    top = jnp.max(vals, axis=0, keepdims=True)
    first = jnp.min(jnp.where(vals == top, idx, vals.shape[0]), axis=0, keepdims=True)
    return top, first


def _merge_router_kernel(oa_ref, ob_ref, ga_ref, gb_ref, x_ref, wa_ref, wb_ref, wo_ref, g_ref,
                         wr_ref, br_ref, x1_ref, h2_ref, ids_ref, cw_ref, rank_ref, cnt_ref):
    @pl.when(pl.program_id(0) == 0)
    def _():
        cnt_ref[...] = jnp.zeros_like(cnt_ref)

    a = jnp.dot(oa_ref[...], wa_ref[...], preferred_element_type=F32)
    b = jnp.dot(ob_ref[...], wb_ref[...], preferred_element_type=F32)
    merged = jax.nn.sigmoid(ga_ref[...]) * a + jax.nn.sigmoid(gb_ref[...]) * b
    x1 = x_ref[...] + jnp.dot(merged.astype(BF16), wo_ref[...], preferred_element_type=F32)
    x1_ref[...] = x1
    h2 = _rms_norm(x1, g_ref[...])
    h2_ref[...] = h2
    tm = h2.shape[0]

    logits = lax.dot_general(wr_ref[...], h2, NT_DIMS, preferred_element_type=F32,
                             precision=lax.Precision.HIGHEST) + br_ref[...]
    grp = logits[0:N_GROUPS, :]
    g_top, g_idx = _first_index_of_max(grp, lax.broadcasted_iota(I32, grp.shape, 0))
    p_grp = 1.0 / jnp.sum(jnp.exp(grp - g_top), axis=0, keepdims=True)
    in_grp = logits[8:8 + EXPERTS_PER_GROUP, :]
    for gi in range(1, N_GROUPS):
        rows = slice(8 + gi * EXPERTS_PER_GROUP, 8 + (gi + 1) * EXPERTS_PER_GROUP)
        in_grp = jnp.where(g_idx == gi, logits[rows, :], in_grp)
    e_iota = lax.broadcasted_iota(I32, in_grp.shape, 0)
    v0, i0 = _first_index_of_max(in_grp, e_iota)
    v1, i1 = _first_index_of_max(jnp.where(e_iota == i0, -jnp.inf, in_grp), e_iota)
    e1 = jnp.exp(v1 - v0)
    denom = 1.0 + e1
    ids = jnp.concatenate([g_idx * EXPERTS_PER_GROUP + i0, g_idx * EXPERTS_PER_GROUP + i1], axis=0)
    ids_ref[0] = ids
    cw_ref[0] = jnp.concatenate([p_grp * (1.0 / denom), p_grp * (e1 / denom)], axis=0)

    before = (lax.broadcasted_iota(I32, (tm, tm), 0) < lax.broadcasted_iota(I32, (tm, tm), 1)).astype(BF16)
    expert = lax.broadcasted_iota(I32, (N_EXPERTS, tm), 0)
    ranks = []
    for slot in range(EXPERT_TOPK):
        onehot = expert == ids[slot:slot + 1, :]
        seen = jnp.dot(onehot.astype(BF16), before, preferred_element_type=F32) + cnt_ref[...]
        ranks.append(jnp.sum(jnp.where(onehot, seen, 0.0), axis=0, keepdims=True))
        cnt_ref[...] += jnp.sum(onehot.astype(F32), axis=1, keepdims=True)
    rank_ref[0] = jnp.concatenate(ranks, axis=0).astype(I32)


def _merge_and_route(o_a, o_b, ga, gb, x, w_br_a, w_br_b, w_out, g_ffn, w_grp, b_grp, w_rt, b_rt, tm=512):
    t, d = x.shape
    n_tiles = t // tm
    wr = jnp.zeros((ROUTER_ROWS, d), F32).at[0:N_GROUPS].set(w_grp.T).at[8:].set(w_rt.T)
    br = jnp.zeros((ROUTER_ROWS, 1), F32).at[0:N_GROUPS, 0].set(b_grp).at[8:, 0].set(b_rt)
    row_spec = lambda w: pl.BlockSpec((tm, w), lambda i: (i, 0))
    slot_spec = pl.BlockSpec((1, EXPERT_TOPK, tm), lambda i: (i, 0, 0))
    slot_shape = lambda dt: jax.ShapeDtypeStruct((n_tiles, EXPERT_TOPK, tm), dt)
    return pl.pallas_call(
        _merge_router_kernel,
        grid=(n_tiles,),
        in_specs=[row_spec(MIXER_W), row_spec(MIXER_W), row_spec(d), row_spec(d), row_spec(d),
                  _const_spec((MIXER_W, d)), _const_spec((MIXER_W, d)), _const_spec((d, d)),
                  _const_spec((1, d)), _const_spec((ROUTER_ROWS, d)), _const_spec((ROUTER_ROWS, 1))],
        out_specs=[row_spec(d), row_spec(d), slot_spec, slot_spec, slot_spec,
                   pl.BlockSpec((N_EXPERTS, 1), lambda i: (0, 0))],
        out_shape=[jax.ShapeDtypeStruct((t, d), F32), jax.ShapeDtypeStruct((t, d), F32),
                   slot_shape(I32), slot_shape(F32), slot_shape(I32),
                   jax.ShapeDtypeStruct((N_EXPERTS, 1), F32)],
        compiler_params=_cparams(("arbitrary",)),
        name="merge_and_route",
    )(o_a, o_b, ga, gb, x, w_br_a.astype(BF16), w_br_b.astype(BF16), w_out.astype(BF16),
      g_ffn.reshape(1, d), wr, br)


EXPERT_ROWS = 512


def _row(ref, r):
    return ref.at[pl.ds(r, 1), :]


def _dispatch_kernel(dest_ref, h2_hbm, xs_init_hbm, xs_hbm, sem):
    del xs_init_hbm
    tm = dest_ref.shape[2]
    base = pl.program_id(0) * tm

    def copies(r):
        return [pltpu.make_async_copy(_row(h2_hbm, base + r), _row(xs_hbm, dest_ref[0, slot, r]), sem)
                for slot in range(EXPERT_TOPK)]

    def issue(r, carry):
        for cp in copies(r):
            cp.start()
        return carry

    def drain(r, carry):
        for cp in copies(r):
            cp.wait()
        return carry

    lax.fori_loop(0, tm, issue, 0)
    lax.fori_loop(0, tm, drain, 0)


def _dispatch(dest, h2, n_rows):
    n_tiles, _, tm = dest.shape
    t, d = h2.shape
    return pl.pallas_call(
        _dispatch_kernel,
        grid=(n_tiles,),
        in_specs=[pl.BlockSpec((1, EXPERT_TOPK, tm), lambda i: (i, 0, 0), memory_space=pltpu.SMEM),
                  pl.BlockSpec(memory_space=pl.ANY), pl.BlockSpec(memory_space=pl.ANY)],
        out_specs=pl.BlockSpec(memory_space=pl.ANY),
        out_shape=jax.ShapeDtypeStruct((n_rows, d), F32),
        scratch_shapes=[pltpu.SemaphoreType.DMA(())],
        input_output_aliases={2: 0},
        compiler_params=_cparams(("arbitrary",)),
        name="moe_dispatch",
    )(dest, h2, jnp.zeros((n_rows, d), F32))


def _expert_kernel(blk_exp_ref, n_used_ref, xs_ref, wgu_ref, wd_ref, ys_ref):
    del blk_exp_ref
    used = pl.program_id(0) < n_used_ref[0]

    @pl.when(used)
    def _():
        gu = jnp.dot(xs_ref[...].astype(BF16), wgu_ref[0], preferred_element_type=F32)
        gate, up = gu[:, :D_EXPERT], gu[:, D_EXPERT:]
        hid = (gate * jax.nn.sigmoid(gate)) * up
        ys_ref[...] = jnp.dot(hid.astype(BF16), wd_ref[0], preferred_element_type=F32)

    @pl.when(jnp.logical_not(used))
    def _():
        ys_ref[...] = jnp.zeros_like(ys_ref)


def _experts(blk_exp, n_used, xs, w_gate_up, w_down):
    n_rows, d = xs.shape
    n_blocks = n_rows // EXPERT_ROWS
    return pl.pallas_call(
        _expert_kernel,
        grid_spec=pltpu.PrefetchScalarGridSpec(
            num_scalar_prefetch=2,
            grid=(n_blocks,),
            in_specs=[pl.BlockSpec((EXPERT_ROWS, d), lambda i, be, nu: (i, 0)),
                      pl.BlockSpec((1, d, 2 * D_EXPERT), lambda i, be, nu: (be[i], 0, 0)),
                      pl.BlockSpec((1, D_EXPERT, d), lambda i, be, nu: (be[i], 0, 0))],
            out_specs=pl.BlockSpec((EXPERT_ROWS, d), lambda i, be, nu: (i, 0)),
        ),
        out_shape=jax.ShapeDtypeStruct((n_rows, d), F32),
        compiler_params=_cparams(("arbitrary",)),
        name="moe_experts",
    )(blk_exp, n_used, xs, w_gate_up, w_down)


def _combine_ple_kernel(dest_ref, x1_ref, cw_ref, p_ref, ys_hbm, g_ple_ref, wpg_ref, wpp_ref, g_fin_ref,
                        out_ref, y_buf, sem):
    tm = x1_ref.shape[0]

    def copies(r):
        return [pltpu.make_async_copy(_row(ys_hbm, dest_ref[0, slot, r]), _row(y_buf.at[slot], r), sem)
                for slot in range(EXPERT_TOPK)]

    def issue(r, carry):
        for cp in copies(r):
            cp.start()
        return carry

    def drain(r, carry):
        for cp in copies(r):
            cp.wait()
        return carry

    lax.fori_loop(0, tm, issue, 0)
    lax.fori_loop(0, tm, drain, 0)

    cw = cw_ref[...]
    x2 = x1_ref[...] + (y_buf[0] * cw[:, 0:1] + y_buf[1] * cw[:, 1:2])
    h3 = _rms_norm(x2, g_ple_ref[...]).astype(BF16)
    gate = jax.nn.sigmoid(jnp.dot(h3, wpg_ref[...], preferred_element_type=F32))
    proj = jnp.dot(p_ref[...].astype(BF16), wpp_ref[...], preferred_element_type=F32)
    out_ref[...] = _rms_norm(x2 + gate * proj, g_fin_ref[...])


def _combine_ple(dest, x1, cw_tok, p, ys, g_ple, w_ple_gate, w_ple_proj, g_final):
    n_tiles, _, tm = dest.shape
    t, d = x1.shape
    row_spec = lambda w: pl.BlockSpec((tm, w), lambda i: (i, 0))
    return pl.pallas_call(
        _combine_ple_kernel,
        grid=(n_tiles,),
        in_specs=[pl.BlockSpec((1, EXPERT_TOPK, tm), lambda i: (i, 0, 0), memory_space=pltpu.SMEM),
                  row_spec(d), row_spec(EXPERT_TOPK), row_spec(p.shape[1]),
                  pl.BlockSpec(memory_space=pl.ANY),
                  _const_spec((1, d)), _const_spec((d, d)), _const_spec((p.shape[1], d)), _const_spec((1, d))],
        out_specs=row_spec(d),
        out_shape=jax.ShapeDtypeStruct((t, d), F32),
        scratch_shapes=[pltpu.VMEM((EXPERT_TOPK, tm, d), F32), pltpu.SemaphoreType.DMA(())],
        compiler_params=_cparams(("arbitrary",)),
        name="combine_ple",
    )(dest, x1, cw_tok, p, ys, g_ple.reshape(1, d), w_ple_gate.astype(BF16), w_ple_proj.astype(BF16),
      g_final.reshape(1, d))


def _layer(x, p, g_attn, w_in, w_br_a, w_br_b, w_out, g_ffn, w_grp, b_grp, w_rt, b_rt,
           w_gate, w_up, w_down, g_ple, w_ple_gate, w_ple_proj, g_final):
    b, s, d = x.shape
    t = b * s
    (qa_t, va_t, qb_t, vb_t, iq_t, iw_t, ka, kb, ik, ga, gb, km) = _in_projection(x, g_attn, w_in)
    o_a = _moba_attention(qa_t, ka, va_t, km.reshape(b, s // MOBA_BLOCK, MIXER_W))
    o_b = _dsa_attention(iq_t, ik, iw_t, qb_t, kb, vb_t)
    x1, h2, ids, cw, rank, counts = _merge_and_route(
        o_a.reshape(t, MIXER_W), o_b.reshape(t, MIXER_W), ga.reshape(t, d), gb.reshape(t, d),
        x.reshape(t, d), w_br_a, w_br_b, w_out, g_ffn, w_grp, b_grp, w_rt, b_rt)

    counts = counts[:, 0].astype(I32)
    padded = ((counts + EXPERT_ROWS - 1) // EXPERT_ROWS) * EXPERT_ROWS
    pend = jnp.cumsum(padded)
    pstart = pend - padded
    n_blocks = -(-(t * EXPERT_TOPK) // EXPERT_ROWS) + N_EXPERTS
    dest = pstart[ids] + rank
    blk_exp = jnp.minimum(jnp.searchsorted(pend, jnp.arange(n_blocks, dtype=I32) * EXPERT_ROWS, side='right'),
                          N_EXPERTS - 1).astype(I32)
    n_used = (pend[-1:] // EXPERT_ROWS).astype(I32)

    xs = _dispatch(dest, h2, n_blocks * EXPERT_ROWS)
    w_gate_up = jnp.concatenate([w_gate, w_up], axis=-1).astype(BF16)
    ys = _experts(blk_exp, n_used, xs, w_gate_up, w_down.astype(BF16))
    cw_tok = jnp.swapaxes(cw, 1, 2).reshape(t, EXPERT_TOPK)
    out = _combine_ple(dest, x1, cw_tok, p.reshape(t, p.shape[-1]), ys, g_ple, w_ple_gate, w_ple_proj, g_final)
    return out.reshape(b, s, d)


def kernel(x, p, g_attn, w_in, w_br_a, w_br_b, w_out, g_ffn, w_grp, b_grp, w_rt, b_rt, w_gate, w_up, w_down, g_ple, w_ple_gate, w_ple_proj, g_final):
    depth = w_in.shape[0]
    assert depth == 1, "the final RMSNorm is fused into the last layer's kernel"
    i = 0
    return _layer(x, p[i], g_attn[i], w_in[i], w_br_a[i], w_br_b[i], w_out[i], g_ffn[i], w_grp[i], b_grp[i],
                  w_rt[i], b_rt[i], w_gate[i], w_up[i], w_down[i], g_ple[i], w_ple_gate[i], w_ple_proj[i], g_final)
```

```python
import functools

import jax
import jax.numpy as jnp
from jax import lax
from jax.experimental import pallas as pl
from jax.experimental.pallas import tpu as pltpu

F32 = jnp.float32
BF16 = jnp.bfloat16
I32 = jnp.int32

HEAD_DIM = 64
N_HEADS = 8
ROT_DIM = HEAD_DIM // 4
ROT_HALF = ROT_DIM // 2
ROPE_THETA = 500000.0
MOBA_BLOCK = 256
MOBA_TOPK = 3
IDX_HEADS = 8
IDX_DIM = 64
DSA_TOPK_MAX = 256
N_GROUPS = 4
EXPERTS_PER_GROUP = 8
N_EXPERTS = N_GROUPS * EXPERTS_PER_GROUP
D_EXPERT = 512
PLE_DIM = 256
RMS_EPS = 1e-6
MIXER_W = N_HEADS * HEAD_DIM

Q_TILE = 256
LANES = 128
SUBLANES = 8
VMEM_LIMIT = 56 * 1024 * 1024

NT_DIMS = (((1,), (1,)), ((), ()))


def _cparams(sem):
    return pltpu.CompilerParams(dimension_semantics=sem, vmem_limit_bytes=VMEM_LIMIT)


def _const_spec(shape):
    nd = len(shape)
    return pl.BlockSpec(shape, lambda *_: (0,) * nd, pipeline_mode=pl.Buffered(1))


def _rope_feature_major(z, cos_t, sin_t):
    tm = z.shape[1]
    z3 = z.reshape(N_HEADS, HEAD_DIM, tm)
    x1 = z3[:, 0:ROT_HALF, :]
    x2 = z3[:, ROT_HALF:ROT_DIM, :]
    o1 = x1 * cos_t - x2 * sin_t
    o2 = x2 * cos_t + x1 * sin_t
    return jnp.concatenate([o1, o2, z3[:, ROT_DIM:, :]], axis=1).reshape(N_HEADS * HEAD_DIM, tm)


def _rope_token_major(z, c_tab, s_lo, s_hi):
    up = pltpu.roll(z, LANES - ROT_HALF, 1)
    dn = pltpu.roll(z, ROT_HALF, 1)
    return z * c_tab + up * s_lo + dn * s_hi


def _store_blocked(ref, z):
    for c in range(z.shape[1] // Q_TILE):
        ref[0, c] = z[:, c * Q_TILE:(c + 1) * Q_TILE]


def _inproj_kernel(x_ref, g_ref, wf_ref, wiw_ref, wt_ref, wg_ref, cos_t_ref, sin_t_ref,
                   ctab_ref, slo_ref, shi_ref,
                   qa_t_ref, va_t_ref, qb_t_ref, vb_t_ref, iq_t_ref, iw_t_ref,
                   ka_ref, kb_ref, ik_ref, ga_ref, gb_ref, km_ref, *, w_scale):
    x = x_ref[0]
    ms = jnp.mean(x * x, axis=-1, keepdims=True)
    h = ((x * lax.rsqrt(ms + RMS_EPS)) * g_ref[...]).astype(BF16)
    cos_t = cos_t_ref[...]
    sin_t = sin_t_ref[...]
    q_scale = HEAD_DIM ** -0.5

    fm_outs = ((qa_t_ref, True, q_scale), (va_t_ref, False, 1.0), (qb_t_ref, True, q_scale),
               (vb_t_ref, False, 1.0), (iq_t_ref, True, IDX_DIM ** -0.5))
    for i, (ref, rope, scale) in enumerate(fm_outs):
        z = lax.dot_general(wf_ref[i * MIXER_W:(i + 1) * MIXER_W, :], h, NT_DIMS,
                            preferred_element_type=F32)
        if rope:
            z = _rope_feature_major(z, cos_t, sin_t)
        if scale != 1.0:
            z = z * scale
        _store_blocked(ref, z.astype(BF16))

    iw = lax.dot_general(wiw_ref[...], h, NT_DIMS, preferred_element_type=F32)
    _store_blocked(iw_t_ref, iw * w_scale)

    zt = jnp.dot(h, wt_ref[...], preferred_element_type=F32)
    ctab, slo, shi = ctab_ref[...], slo_ref[...], shi_ref[...]
    n_grp = zt.shape[1] // LANES
    roped = [_rope_token_major(zt[:, j * LANES:(j + 1) * LANES], ctab, slo, shi) for j in range(n_grp)]
    per_mixer = MIXER_W // LANES
    ka = jnp.concatenate(roped[:per_mixer], axis=1)
    kb = jnp.concatenate(roped[per_mixer:2 * per_mixer], axis=1)
    ka_ref[0] = ka.astype(BF16)
    kb_ref[0] = kb.astype(BF16)
    ik_ref[0] = roped[2 * per_mixer][:, :IDX_DIM].astype(BF16)
    tm = ka.shape[0]
    km_ref[0] = jnp.mean(ka.reshape(tm // MOBA_BLOCK, MOBA_BLOCK, MIXER_W), axis=1, keepdims=True)

    zg = jnp.dot(h, wg_ref[...], preferred_element_type=F32)
    d_model = zg.shape[1] // 2
    ga_ref[0] = zg[:, :d_model]
    gb_ref[0] = zg[:, d_model:]


def _rope_tables(seq):
    inv = 1.0 / (ROPE_THETA ** (jnp.arange(0, ROT_DIM, 2, dtype=F32) / ROT_DIM))
    ang = jnp.arange(seq, dtype=F32)[:, None] * inv[None, :]
    cos, sin = jnp.cos(ang), jnp.sin(ang)
    d = jnp.arange(LANES) % HEAD_DIM
    lo = d < ROT_HALF
    hi = (d >= ROT_HALF) & (d < ROT_DIM)
    f = d % ROT_HALF
    cos_l, sin_l = cos[:, f], sin[:, f]
    ctab = jnp.where(lo | hi, cos_l, 1.0)
    slo = jnp.where(lo, -sin_l, 0.0)
    shi = jnp.where(hi, sin_l, 0.0)
    return cos.T, sin.T, ctab, slo, shi


def _in_projection(x, g_attn, w_in, tm=512):
    b, s, d = x.shape
    splits = (MIXER_W,) * 6 + (IDX_HEADS * IDX_DIM, IDX_DIM, IDX_HEADS, d, d)
    offs = [0]
    for w in splits:
        offs.append(offs[-1] + w)
    wqa, wka, wva, wqb, wkb, wvb, wiq, wik, wiw, wga, wgb = (
        w_in[:, offs[i]:offs[i + 1]] for i in range(len(splits)))
    wf = jnp.concatenate([wqa, wva, wqb, wvb, wiq], axis=1).T.astype(BF16)
    wiw_t = wiw.T.astype(BF16)
    wt = jnp.concatenate([wka, wkb, wik, jnp.zeros((d, LANES - IDX_DIM), w_in.dtype)], axis=1).astype(BF16)
    wg = jnp.concatenate([wga, wgb], axis=1).astype(BF16)
    cos_t, sin_t, ctab, slo, shi = _rope_tables(s)
    n_blk = s // MOBA_BLOCK
    fm_shape = jax.ShapeDtypeStruct((b, s // Q_TILE, MIXER_W, Q_TILE), BF16)
    tok_shape = jax.ShapeDtypeStruct((b, s, MIXER_W), BF16)
    fm_spec = pl.BlockSpec((1, tm // Q_TILE, MIXER_W, Q_TILE), lambda bi, ti: (bi, ti, 0, 0))
    tok_spec = pl.BlockSpec((1, tm, MIXER_W), lambda bi, ti: (bi, ti, 0))
    gate_spec = pl.BlockSpec((1, tm, d), lambda bi, ti: (bi, ti, 0))
    outs = pl.pallas_call(
        functools.partial(_inproj_kernel, w_scale=IDX_HEADS ** -0.5),
        grid=(b, s // tm),
        in_specs=[
            pl.BlockSpec((1, tm, d), lambda bi, ti: (bi, ti, 0)),
            _const_spec((1, d)),
            _const_spec(wf.shape), _const_spec(wiw_t.shape), _const_spec(wt.shape), _const_spec(wg.shape),
            pl.BlockSpec((ROT_HALF, tm), lambda bi, ti: (0, ti)),
            pl.BlockSpec((ROT_HALF, tm), lambda bi, ti: (0, ti)),
            pl.BlockSpec((tm, LANES), lambda bi, ti: (ti, 0)),
            pl.BlockSpec((tm, LANES), lambda bi, ti: (ti, 0)),
            pl.BlockSpec((tm, LANES), lambda bi, ti: (ti, 0)),
        ],
        out_specs=[
            fm_spec, fm_spec, fm_spec, fm_spec, fm_spec,
            pl.BlockSpec((1, tm // Q_TILE, IDX_HEADS, Q_TILE), lambda bi, ti: (bi, ti, 0, 0)),
            tok_spec, tok_spec,
            pl.BlockSpec((1, tm, IDX_DIM), lambda bi, ti: (bi, ti, 0)),
            gate_spec, gate_spec,
            pl.BlockSpec((1, tm // MOBA_BLOCK, 1, MIXER_W), lambda bi, ti: (bi, ti, 0, 0)),
        ],
        out_shape=[
            fm_shape, fm_shape, fm_shape, fm_shape, fm_shape,
            jax.ShapeDtypeStruct((b, s // Q_TILE, IDX_HEADS, Q_TILE), F32),
            tok_shape, tok_shape,
            jax.ShapeDtypeStruct((b, s, IDX_DIM), BF16),
            jax.ShapeDtypeStruct((b, s, d), F32), jax.ShapeDtypeStruct((b, s, d), F32),
            jax.ShapeDtypeStruct((b, n_blk, 1, MIXER_W), F32),
        ],
        compiler_params=_cparams(("parallel", "parallel")),
        name="in_projection",
    )(x, g_attn.reshape(1, d), wf, wiw_t, wt, wg, cos_t, sin_t, ctab, slo, shi)
    return outs


MASKED = -1e30


HEADS_PER_GROUP = LANES // HEAD_DIM


def _one_head_of_pair(q_pair, hh):
    row = lax.broadcasted_iota(I32, q_pair.shape, 0)
    return jnp.where((row // HEAD_DIM) == hh, q_pair, jnp.zeros_like(q_pair))


def _head_lanes(h):
    g = h // HEADS_PER_GROUP
    return slice(g * LANES, (g + 1) * LANES)


def _head_rows(h):
    return slice(h * HEAD_DIM, (h + 1) * HEAD_DIM)


ATTN_GROUP = 4
ONES_ROWS = 16
ACC_ROWS = HEAD_DIM + ONES_ROWS


def _split_heads(q_all, qh_ref):
    for h in range(N_HEADS):
        qh_ref[h] = _one_head_of_pair(q_all[_head_lanes(h), :], h % HEADS_PER_GROUP)


def _weighted_values(s, m, v_t):
    p = jnp.exp(s - m).astype(BF16)
    lhs = jnp.concatenate([v_t, jnp.ones((ONES_ROWS, v_t.shape[1]), BF16)], axis=0)
    return jnp.dot(lhs, p, preferred_element_type=F32)


def _loop_in_pairs(n, body, init):
    carry = lax.fori_loop(0, n // 2, lambda k, c: body(2 * k, 2, c), init)
    return lax.cond(n % 2 == 1, lambda c: body(n - 1, 1, c), lambda c: c, carry)


def _store_head_output(o_t_ref, h, acc):
    o_t_ref[_head_rows(h), :] = acc[:HEAD_DIM, :] / acc[HEAD_DIM:HEAD_DIM + 1, :]


def _moba_kernel(q_t_ref, k_ref, v_t_ref, km_ref, o_ref, qh_ref, bias_ref, s_ref, o_t_ref, *acc_refs):
    i = pl.program_id(1)
    tq = q_t_ref.shape[3]
    n_blk = km_ref.shape[1]
    _split_heads(q_t_ref[0, 0], qh_ref)

    blk = lax.broadcasted_iota(I32, (n_blk, tq), 0)
    for h in range(N_HEADS):
        gate = jnp.dot(km_ref[0, :, _head_lanes(h)], qh_ref[h].astype(F32), preferred_element_type=F32)
        beaten_by = jnp.zeros((n_blk, tq), I32)
        for m_blk in range(n_blk):
            gm = gate[m_blk:m_blk + 1, :]
            beats = ((gm > gate) | ((gm == gate) & (m_blk < blk))) & (m_blk < i)
            beaten_by = beaten_by + beats.astype(I32)
        keep = (blk < i) & (beaten_by < MOBA_TOPK)
        bias_ref[h] = jnp.where(keep | (blk == i), 0.0, MASKED)

    def key_rows(j):
        return pl.ds(pl.multiple_of(j * tq, tq), tq)

    def block_bias(h, j):
        return bias_ref[h, pl.ds(j, 1), :]

    causal = lax.broadcasted_iota(I32, (tq, tq), 0) <= lax.broadcasted_iota(I32, (tq, tq), 1)
    for g in range(N_HEADS // ATTN_GROUP):
        heads = list(enumerate(range(g * ATTN_GROUP, (g + 1) * ATTN_GROUP)))

        def scores(h, j):
            return jnp.dot(k_ref[0, key_rows(j), _head_lanes(h)], qh_ref[h], preferred_element_type=F32)

        own_max = []
        for gi, h in heads:
            s = jnp.where(causal, scores(h, i), MASKED)
            s_ref[gi, key_rows(i), :] = s
            own_max.append(jnp.max(s, axis=0, keepdims=True))

        def past_scores(j0, count, mx, heads=heads, scores=scores):
            mx = list(mx)
            for j in (j0 + d for d in range(count)):
                for gi, h in heads:
                    s = scores(h, j)
                    s_ref[gi, key_rows(j), :] = s
                    mx[gi] = jnp.maximum(mx[gi], jnp.max(s, axis=0, keepdims=True) + block_bias(h, j))
            return tuple(mx)

        mx = _loop_in_pairs(i, past_scores, tuple(own_max))
        for gi, _ in heads:
            acc_refs[gi][...] = jnp.zeros(acc_refs[gi].shape, F32)

        def accumulate(j0, count, carry, heads=heads, mx=mx):
            for gi, h in heads:
                acc_refs[gi][...] += sum(
                    _weighted_values(s_ref[gi, key_rows(j0 + d), :], mx[gi] - block_bias(h, j0 + d),
                                     v_t_ref[0, j0 + d, _head_rows(h), :]) for d in range(count))
            return carry

        _loop_in_pairs(i + 1, accumulate, 0)
        for gi, h in heads:
            _store_head_output(o_t_ref, h, acc_refs[gi][...])
    o_ref[0] = o_t_ref[...].T.astype(BF16)


def _attention_scratch(s, w, tq):
    return ([pltpu.VMEM((N_HEADS, LANES, tq), BF16)],
            [pltpu.VMEM((ATTN_GROUP, s, tq), F32), pltpu.VMEM((w, tq), F32)]
            + [pltpu.VMEM((ACC_ROWS, tq), F32)] * ATTN_GROUP)


def _resident_spec(shape):
    nd = len(shape)
    return pl.BlockSpec(shape, lambda bi, i: (bi,) + (0,) * (nd - 1), pipeline_mode=pl.Buffered(1))


def _moba_attention(qa_t, ka, va_t, km):
    b, n_blk, w, tq = qa_t.shape
    s = n_blk * tq
    qh, work = _attention_scratch(s, w, tq)
    return pl.pallas_call(
        _moba_kernel,
        grid=(b, n_blk),
        in_specs=[
            pl.BlockSpec((1, 1, w, tq), lambda bi, i: (bi, i, 0, 0)),
            _resident_spec((1, s, w)),
            _resident_spec((1, n_blk, w, tq)),
            _resident_spec((1, n_blk, w)),
        ],
        out_specs=pl.BlockSpec((1, tq, w), lambda bi, i: (bi, i, 0)),
        out_shape=jax.ShapeDtypeStruct((b, s, w), BF16),
        scratch_shapes=qh + [pltpu.VMEM((N_HEADS, n_blk, tq), F32)] + work,
        compiler_params=_cparams(("parallel", "arbitrary")),
        name="moba_attention",
    )(qa_t, ka, va_t, km)


INT_MIN = -2 ** 31


def _sortable_key(s):
    bits = lax.bitcast_convert_type(s, I32)
    return bits ^ ((bits >> 31) & 0x7FFFFFFF)


def _column_count(mask):
    tk, tq = mask.shape
    return jnp.sum(mask.astype(I32).reshape(tk // SUBLANES, SUBLANES, tq), axis=0)


def _dsa_kernel(iq_t_ref, ik_ref, iw_t_ref, q_t_ref, k_ref, v_t_ref, o_ref,
                key_ref, bias_ref, qh_ref, s_ref, o_t_ref, *acc_refs, topk, index_bits):
    t = pl.program_id(1)
    tq = q_t_ref.shape[3]
    tk = tq
    n_chunks = t + 1
    qpos = t * tq + lax.broadcasted_iota(I32, (1, tq), 1)
    krow = lax.broadcasted_iota(I32, (tk, tq), 0)

    def chunk_rows(c):
        return pl.ds(pl.multiple_of(c * tk, tk), tk)

    def total(count8):
        return jnp.sum(count8, axis=0, keepdims=True)

    def count_over_chunks(pred):
        def body(c, cnt):
            return cnt + _column_count(pred(key_ref[chunk_rows(c), :], c * tk + krow))
        return total(lax.fori_loop(0, n_chunks, body, jnp.zeros((SUBLANES, tq), I32)))

    iq = iq_t_ref[0, 0]
    iw = iw_t_ref[0, 0]

    def score_chunk(c, carry):
        ik_c = ik_ref[0, chunk_rows(c), :]
        score = jnp.zeros((tk, tq), F32)
        for h in range(IDX_HEADS):
            rel = jnp.dot(ik_c, iq[h * IDX_DIM:(h + 1) * IDX_DIM, :], preferred_element_type=F32)
            score = score + jnp.maximum(rel, 0.0) * iw[h:h + 1, :]
        key_ref[chunk_rows(c), :] = jnp.where(c * tk + krow <= qpos, _sortable_key(score), INT_MIN)
        return carry

    lax.fori_loop(0, n_chunks, score_chunk, 0)

    def value_bit(b, prefix):
        cand = prefix | jnp.left_shift(jnp.int32(1), 31 - b)
        cand_key = cand ^ INT_MIN
        n_ge = count_over_chunks(lambda keys, kpos: keys >= cand_key)
        return jnp.where(n_ge >= topk, cand, prefix)

    thr = lax.fori_loop(0, 32, value_bit, jnp.zeros((1, tq), I32)) ^ INT_MIN

    n_gt = count_over_chunks(lambda keys, kpos: keys > thr)
    n_eq = count_over_chunks(lambda keys, kpos: keys == thr)
    need = topk - n_gt
    has_surplus = jnp.any((n_eq > need) & (thr != INT_MIN))

    def last_tie_index():
        def index_bit(b, prefix):
            cand = prefix | jnp.left_shift(jnp.int32(1), index_bits - 1 - b)
            n_before = count_over_chunks(lambda keys, kpos: (keys == thr) & (kpos < cand))
            return jnp.where(n_before < need, cand, prefix)
        return lax.fori_loop(0, index_bits, index_bit, jnp.zeros((1, tq), I32))

    tie_end = lax.cond(has_surplus, last_tie_index, lambda: jnp.full((1, tq), 2 ** index_bits, I32))

    def bias_chunk(c, carry):
        keys = key_ref[chunk_rows(c), :]
        kpos = c * tk + krow
        chosen = ((keys > thr) | ((keys == thr) & (kpos <= tie_end))) & (kpos <= qpos)
        bias_ref[chunk_rows(c), :] = jnp.where(chosen, 0.0, MASKED)
        return carry

    lax.fori_loop(0, n_chunks, bias_chunk, 0)

    _split_heads(q_t_ref[0, 0], qh_ref)
    for g in range(N_HEADS // ATTN_GROUP):
        heads = list(enumerate(range(g * ATTN_GROUP, (g + 1) * ATTN_GROUP)))

        def masked_scores(c0, count, mx, heads=heads):
            mx = list(mx)
            for c in (c0 + d for d in range(count)):
                for gi, h in heads:
                    s = jnp.dot(k_ref[0, chunk_rows(c), _head_lanes(h)], qh_ref[h], preferred_element_type=F32)
                    s = s + bias_ref[chunk_rows(c), :]
                    s_ref[gi, chunk_rows(c), :] = s
                    mx[gi] = jnp.maximum(mx[gi], jnp.max(s, axis=0, keepdims=True))
            return tuple(mx)

        mx = _loop_in_pairs(n_chunks, masked_scores, (jnp.full((1, tq), MASKED, F32),) * ATTN_GROUP)
        for gi, _ in heads:
            acc_refs[gi][...] = jnp.zeros(acc_refs[gi].shape, F32)

        def accumulate(c0, count, carry, heads=heads, mx=mx):
            for gi, h in heads:
                acc_refs[gi][...] += sum(
                    _weighted_values(s_ref[gi, chunk_rows(c0 + d), :], mx[gi], v_t_ref[0, c0 + d, _head_rows(h), :])
                    for d in range(count))
            return carry

        _loop_in_pairs(n_chunks, accumulate, 0)
        for gi, h in heads:
            _store_head_output(o_t_ref, h, acc_refs[gi][...])
    o_ref[0] = o_t_ref[...].T.astype(BF16)


def _dsa_attention(iq_t, ik, iw_t, qb_t, kb, vb_t):
    b, n_blk, w, tq = qb_t.shape
    s = n_blk * tq
    topk = min(DSA_TOPK_MAX, s // 4)
    qh, work = _attention_scratch(s, w, tq)
    tile_spec = lambda rows: pl.BlockSpec((1, 1, rows, tq), lambda bi, ti: (bi, ti, 0, 0))
    return pl.pallas_call(
        functools.partial(_dsa_kernel, topk=topk, index_bits=max(1, (s - 1).bit_length())),
        grid=(b, n_blk),
        in_specs=[
            tile_spec(IDX_HEADS * IDX_DIM),
            _resident_spec((1, s, IDX_DIM)),
            tile_spec(IDX_HEADS),
            tile_spec(w),
            _resident_spec((1, s, w)),
            _resident_spec((1, n_blk, w, tq)),
        ],
        out_specs=pl.BlockSpec((1, tq, w), lambda bi, ti: (bi, ti, 0)),
        out_shape=jax.ShapeDtypeStruct((b, s, w), BF16),
        scratch_shapes=[pltpu.VMEM((s, tq), I32), pltpu.VMEM((s, tq), F32)] + qh + work,
        compiler_params=_cparams(("parallel", "arbitrary")),
        name="dsa_attention",
    )(iq_t, ik, iw_t, qb_t, kb, vb_t)


ROUTER_ROWS = 8 + N_EXPERTS
EXPERT_TOPK = 2


def _rms_norm(x, g):
    ms = jnp.mean(x * x, axis=-1, keepdims=True)
    return (x * lax.rsqrt(ms + RMS_EPS)) * g


def _first_index_of_max(vals, idx):
    top = jnp.max(vals, axis=0, keepdims=True)
    first = jnp.min(jnp.where(vals == top, idx, vals.shape[0]), axis=0, keepdims=True)
    return top, first


def _merge_router_kernel(oa_ref, ob_ref, ga_ref, gb_ref, x_ref, wa_ref, wb_ref, wo_ref, g_ref,
                         wr_ref, br_ref, x1_ref, h2_ref, ids_ref, cw_ref, rank_ref, cnt_ref):
    @pl.when(pl.program_id(0) == 0)
    def _():
        cnt_ref[...] = jnp.zeros_like(cnt_ref)

    a = jnp.dot(oa_ref[...], wa_ref[...], preferred_element_type=F32)
    b = jnp.dot(ob_ref[...], wb_ref[...], preferred_element_type=F32)
    merged = jax.nn.sigmoid(ga_ref[...]) * a + jax.nn.sigmoid(gb_ref[...]) * b
    x1 = x_ref[...] + jnp.dot(merged.astype(BF16), wo_ref[...], preferred_element_type=F32)
    x1_ref[...] = x1
    h2 = _rms_norm(x1, g_ref[...])
    h2_ref[...] = h2
    tm = h2.shape[0]

    logits = lax.dot_general(wr_ref[...], h2, NT_DIMS, preferred_element_type=F32,
                             precision=lax.Precision.HIGHEST) + br_ref[...]
    grp = logits[0:N_GROUPS, :]
    g_top, g_idx = _first_index_of_max(grp, lax.broadcasted_iota(I32, grp.shape, 0))
    p_grp = 1.0 / jnp.sum(jnp.exp(grp - g_top), axis=0, keepdims=True)
    in_grp = logits[8:8 + EXPERTS_PER_GROUP, :]
    for gi in range(1, N_GROUPS):
        rows = slice(8 + gi * EXPERTS_PER_GROUP, 8 + (gi + 1) * EXPERTS_PER_GROUP)
        in_grp = jnp.where(g_idx == gi, logits[rows, :], in_grp)
    e_iota = lax.broadcasted_iota(I32, in_grp.shape, 0)
    v0, i0 = _first_index_of_max(in_grp, e_iota)
    v1, i1 = _first_index_of_max(jnp.where(e_iota == i0, -jnp.inf, in_grp), e_iota)
    e1 = jnp.exp(v1 - v0)
    denom = 1.0 + e1
    ids = jnp.concatenate([g_idx * EXPERTS_PER_GROUP + i0, g_idx * EXPERTS_PER_GROUP + i1], axis=0)
    ids_ref[0] = ids
    cw_ref[0] = jnp.concatenate([p_grp * (1.0 / denom), p_grp * (e1 / denom)], axis=0)

    before = (lax.broadcasted_iota(I32, (tm, tm), 0) < lax.broadcasted_iota(I32, (tm, tm), 1)).astype(BF16)
    expert = lax.broadcasted_iota(I32, (N_EXPERTS, tm), 0)
    ranks = []
    for slot in range(EXPERT_TOPK):
        onehot = expert == ids[slot:slot + 1, :]
        seen = jnp.dot(onehot.astype(BF16), before, preferred_element_type=F32) + cnt_ref[...]
        ranks.append(jnp.sum(jnp.where(onehot, seen, 0.0), axis=0, keepdims=True))
        cnt_ref[...] += jnp.sum(onehot.astype(F32), axis=1, keepdims=True)
    rank_ref[0] = jnp.concatenate(ranks, axis=0).astype(I32)


def _merge_and_route(o_a, o_b, ga, gb, x, w_br_a, w_br_b, w_out, g_ffn, w_grp, b_grp, w_rt, b_rt, tm=512):
    t, d = x.shape
    n_tiles = t // tm
    wr = jnp.zeros((ROUTER_ROWS, d), F32).at[0:N_GROUPS].set(w_grp.T).at[8:].set(w_rt.T)
    br = jnp.zeros((ROUTER_ROWS, 1), F32).at[0:N_GROUPS, 0].set(b_grp).at[8:, 0].set(b_rt)
    row_spec = lambda w: pl.BlockSpec((tm, w), lambda i: (i, 0))
    slot_spec = pl.BlockSpec((1, EXPERT_TOPK, tm), lambda i: (i, 0, 0))
    slot_shape = lambda dt: jax.ShapeDtypeStruct((n_tiles, EXPERT_TOPK, tm), dt)
    return pl.pallas_call(
        _merge_router_kernel,
        grid=(n_tiles,),
        in_specs=[row_spec(MIXER_W), row_spec(MIXER_W), row_spec(d), row_spec(d), row_spec(d),
                  _const_spec((MIXER_W, d)), _const_spec((MIXER_W, d)), _const_spec((d, d)),
                  _const_spec((1, d)), _const_spec((ROUTER_ROWS, d)), _const_spec((ROUTER_ROWS, 1))],
        out_specs=[row_spec(d), row_spec(d), slot_spec, slot_spec, slot_spec,
                   pl.BlockSpec((N_EXPERTS, 1), lambda i: (0, 0))],
        out_shape=[jax.ShapeDtypeStruct((t, d), F32), jax.ShapeDtypeStruct((t, d), F32),
                   slot_shape(I32), slot_shape(F32), slot_shape(I32),
                   jax.ShapeDtypeStruct((N_EXPERTS, 1), F32)],
        compiler_params=_cparams(("arbitrary",)),
        name="merge_and_route",
    )(o_a, o_b, ga, gb, x, w_br_a.astype(BF16), w_br_b.astype(BF16), w_out.astype(BF16),
      g_ffn.reshape(1, d), wr, br)


EXPERT_ROWS = 512


def _row(ref, r):
    return ref.at[pl.ds(r, 1), :]


def _dispatch_kernel(dest_ref, h2_ref, xs_init_hbm, xs_hbm, sem):
    del xs_init_hbm
    tm = dest_ref.shape[2]

    def copies(r):
        return [pltpu.make_async_copy(_row(h2_ref, r), _row(xs_hbm, dest_ref[0, slot, r]), sem)
                for slot in range(EXPERT_TOPK)]

    def issue(r, carry):
        for cp in copies(r):
            cp.start()
        return carry

    def drain(r, carry):
        for cp in copies(r):
            cp.wait()
        return carry

    lax.fori_loop(0, tm, issue, 0)
    lax.fori_loop(0, tm, drain, 0)


def _dispatch(dest, h2, n_rows):
    n_tiles, _, tm = dest.shape
    t, d = h2.shape
    return pl.pallas_call(
        _dispatch_kernel,
        grid=(n_tiles,),
        in_specs=[pl.BlockSpec((1, EXPERT_TOPK, tm), lambda i: (i, 0, 0), memory_space=pltpu.SMEM),
                  pl.BlockSpec((tm, d), lambda i: (i, 0)), pl.BlockSpec(memory_space=pl.ANY)],
        out_specs=pl.BlockSpec(memory_space=pl.ANY),
        out_shape=jax.ShapeDtypeStruct((n_rows, d), F32),
        scratch_shapes=[pltpu.SemaphoreType.DMA(())],
        input_output_aliases={2: 0},
        compiler_params=_cparams(("arbitrary",)),
        name="moe_dispatch",
    )(dest, h2, jnp.zeros((n_rows, d), F32))


def _expert_kernel(blk_exp_ref, n_used_ref, xs_ref, wgu_ref, wd_ref, ys_ref):
    del blk_exp_ref
    used = pl.program_id(0) < n_used_ref[0]

    @pl.when(used)
    def _():
        gu = jnp.dot(xs_ref[...].astype(BF16), wgu_ref[0], preferred_element_type=F32)
        gate, up = gu[:, :D_EXPERT], gu[:, D_EXPERT:]
        hid = (gate * jax.nn.sigmoid(gate)) * up
        ys_ref[...] = jnp.dot(hid.astype(BF16), wd_ref[0], preferred_element_type=F32)

    @pl.when(jnp.logical_not(used))
    def _():
        ys_ref[...] = jnp.zeros_like(ys_ref)


def _experts(blk_exp, n_used, xs, w_gate_up, w_down):
    n_rows, d = xs.shape
    n_blocks = n_rows // EXPERT_ROWS
    return pl.pallas_call(
        _expert_kernel,
        grid_spec=pltpu.PrefetchScalarGridSpec(
            num_scalar_prefetch=2,
            grid=(n_blocks,),
            in_specs=[pl.BlockSpec((EXPERT_ROWS, d), lambda i, be, nu: (i, 0)),
                      pl.BlockSpec((1, d, 2 * D_EXPERT), lambda i, be, nu: (be[i], 0, 0)),
                      pl.BlockSpec((1, D_EXPERT, d), lambda i, be, nu: (be[i], 0, 0))],
            out_specs=pl.BlockSpec((EXPERT_ROWS, d), lambda i, be, nu: (i, 0)),
        ),
        out_shape=jax.ShapeDtypeStruct((n_rows, d), F32),
        compiler_params=_cparams(("arbitrary",)),
        name="moe_experts",
    )(blk_exp, n_used, xs, w_gate_up, w_down)


def _combine_ple_kernel(dest_ref, x1_ref, cw_ref, p_ref, ys_hbm, g_ple_ref, wpg_ref, wpp_ref, g_fin_ref,
                        out_ref, y_buf, sem):
    tm = x1_ref.shape[0]

    def copies(r):
        return [pltpu.make_async_copy(_row(ys_hbm, dest_ref[0, slot, r]), _row(y_buf.at[slot], r), sem)
                for slot in range(EXPERT_TOPK)]

    def issue(r, carry):
        for cp in copies(r):
            cp.start()
        return carry

    def drain(r, carry):
        for cp in copies(r):
            cp.wait()
        return carry

    lax.fori_loop(0, tm, issue, 0)
    lax.fori_loop(0, tm, drain, 0)

    cw = cw_ref[...]
    x2 = x1_ref[...] + (y_buf[0] * cw[:, 0:1] + y_buf[1] * cw[:, 1:2])
    h3 = _rms_norm(x2, g_ple_ref[...]).astype(BF16)
    gate = jax.nn.sigmoid(jnp.dot(h3, wpg_ref[...], preferred_element_type=F32))
    proj = jnp.dot(p_ref[...].astype(BF16), wpp_ref[...], preferred_element_type=F32)
    out_ref[...] = _rms_norm(x2 + gate * proj, g_fin_ref[...])


def _combine_ple(dest, x1, cw_tok, p, ys, g_ple, w_ple_gate, w_ple_proj, g_final):
    n_tiles, _, tm = dest.shape
    t, d = x1.shape
    row_spec = lambda w: pl.BlockSpec((tm, w), lambda i: (i, 0))
    return pl.pallas_call(
        _combine_ple_kernel,
        grid=(n_tiles,),
        in_specs=[pl.BlockSpec((1, EXPERT_TOPK, tm), lambda i: (i, 0, 0), memory_space=pltpu.SMEM),
                  row_spec(d), row_spec(EXPERT_TOPK), row_spec(p.shape[1]),
                  pl.BlockSpec(memory_space=pl.ANY),
                  _const_spec((1, d)), _const_spec((d, d)), _const_spec((p.shape[1], d)), _const_spec((1, d))],
        out_specs=row_spec(d),
        out_shape=jax.ShapeDtypeStruct((t, d), F32),
        scratch_shapes=[pltpu.VMEM((EXPERT_TOPK, tm, d), F32), pltpu.SemaphoreType.DMA(())],
        compiler_params=_cparams(("arbitrary",)),
        name="combine_ple",
    )(dest, x1, cw_tok, p, ys, g_ple.reshape(1, d), w_ple_gate.astype(BF16), w_ple_proj.astype(BF16),
      g_final.reshape(1, d))


def _layer(x, p, g_attn, w_in, w_br_a, w_br_b, w_out, g_ffn, w_grp, b_grp, w_rt, b_rt,
           w_gate, w_up, w_down, g_ple, w_ple_gate, w_ple_proj, g_final):
    b, s, d = x.shape
    t = b * s
    (qa_t, va_t, qb_t, vb_t, iq_t, iw_t, ka, kb, ik, ga, gb, km) = _in_projection(x, g_attn, w_in)
    o_a = _moba_attention(qa_t, ka, va_t, km.reshape(b, s // MOBA_BLOCK, MIXER_W))
    o_b = _dsa_attention(iq_t, ik, iw_t, qb_t, kb, vb_t)
    x1, h2, ids, cw, rank, counts = _merge_and_route(
        o_a.reshape(t, MIXER_W), o_b.reshape(t, MIXER_W), ga.reshape(t, d), gb.reshape(t, d),
        x.reshape(t, d), w_br_a, w_br_b, w_out, g_ffn, w_grp, b_grp, w_rt, b_rt)

    counts = counts[:, 0].astype(I32)
    padded = ((counts + EXPERT_ROWS - 1) // EXPERT_ROWS) * EXPERT_ROWS
    pend = jnp.cumsum(padded)
    pstart = pend - padded
    n_blocks = -(-(t * EXPERT_TOPK) // EXPERT_ROWS) + N_EXPERTS
    experts = jnp.arange(N_EXPERTS, dtype=I32)
    dest = jnp.sum(jnp.where(ids[..., None] == experts, pstart, 0), axis=-1) + rank
    block_row0 = jnp.arange(n_blocks, dtype=I32) * EXPERT_ROWS
    blk_exp = jnp.minimum(jnp.sum((pend[None, :] <= block_row0[:, None]).astype(I32), axis=1), N_EXPERTS - 1)
    n_used = (pend[-1:] // EXPERT_ROWS).astype(I32)

    xs = _dispatch(dest, h2, n_blocks * EXPERT_ROWS)
    w_gate_up = jnp.concatenate([w_gate, w_up], axis=-1).astype(BF16)
    ys = _experts(blk_exp, n_used, xs, w_gate_up, w_down.astype(BF16))
    cw_tok = jnp.swapaxes(cw, 1, 2).reshape(t, EXPERT_TOPK)
    out = _combine_ple(dest, x1, cw_tok, p.reshape(t, p.shape[-1]), ys, g_ple, w_ple_gate, w_ple_proj, g_final)
    return out.reshape(b, s, d)


def kernel(x, p, g_attn, w_in, w_br_a, w_br_b, w_out, g_ffn, w_grp, b_grp, w_rt, b_rt, w_gate, w_up, w_down, g_ple, w_ple_gate, w_ple_proj, g_final):
    depth = w_in.shape[0]
    assert depth == 1, "the final RMSNorm is fused into the last layer's kernel"
    i = 0
    return _layer(x, p[i], g_attn[i], w_in[i], w_br_a[i], w_br_b[i], w_out[i], g_ffn[i], w_grp[i], b_grp[i],
                  w_rt[i], b_rt[i], w_gate[i], w_up[i], w_down[i], g_ple[i], w_ple_gate[i], w_ple_proj[i], g_final)
```

```python
import functools

import jax
import jax.numpy as jnp
from jax import lax
from jax.experimental import pallas as pl
from jax.experimental.pallas import tpu as pltpu

F32 = jnp.float32
BF16 = jnp.bfloat16
I32 = jnp.int32

HEAD_DIM = 64
N_HEADS = 8
ROT_DIM = HEAD_DIM // 4
ROT_HALF = ROT_DIM // 2
ROPE_THETA = 500000.0
MOBA_BLOCK = 256
MOBA_TOPK = 3
IDX_HEADS = 8
IDX_DIM = 64
DSA_TOPK_MAX = 256
N_GROUPS = 4
EXPERTS_PER_GROUP = 8
N_EXPERTS = N_GROUPS * EXPERTS_PER_GROUP
D_EXPERT = 512
PLE_DIM = 256
RMS_EPS = 1e-6
MIXER_W = N_HEADS * HEAD_DIM

Q_TILE = 256
LANES = 128
SUBLANES = 8
VMEM_LIMIT = 56 * 1024 * 1024

NT_DIMS = (((1,), (1,)), ((), ()))


def _cparams(sem):
    return pltpu.CompilerParams(dimension_semantics=sem, vmem_limit_bytes=VMEM_LIMIT)


def _const_spec(shape):
    nd = len(shape)
    return pl.BlockSpec(shape, lambda *_: (0,) * nd, pipeline_mode=pl.Buffered(1))


def _rope_feature_major(z, cos_t, sin_t):
    tm = z.shape[1]
    z3 = z.reshape(N_HEADS, HEAD_DIM, tm)
    x1 = z3[:, 0:ROT_HALF, :]
    x2 = z3[:, ROT_HALF:ROT_DIM, :]
    o1 = x1 * cos_t - x2 * sin_t
    o2 = x2 * cos_t + x1 * sin_t
    return jnp.concatenate([o1, o2, z3[:, ROT_DIM:, :]], axis=1).reshape(N_HEADS * HEAD_DIM, tm)


def _rope_token_major(z, c_tab, s_lo, s_hi):
    up = pltpu.roll(z, LANES - ROT_HALF, 1)
    dn = pltpu.roll(z, ROT_HALF, 1)
    return z * c_tab + up * s_lo + dn * s_hi


def _store_blocked(ref, z):
    for c in range(z.shape[1] // Q_TILE):
        ref[0, c] = z[:, c * Q_TILE:(c + 1) * Q_TILE]


def _inproj_kernel(x_ref, g_ref, wf_ref, wiw_ref, wt_ref, wg_ref, cos_t_ref, sin_t_ref,
                   ctab_ref, slo_ref, shi_ref,
                   qa_t_ref, va_t_ref, qb_t_ref, vb_t_ref, iq_t_ref, iw_t_ref,
                   ka_ref, kb_ref, ik_ref, ga_ref, gb_ref, km_ref, *, w_scale):
    x = x_ref[0]
    ms = jnp.mean(x * x, axis=-1, keepdims=True)
    h = ((x * lax.rsqrt(ms + RMS_EPS)) * g_ref[...]).astype(BF16)
    cos_t = cos_t_ref[...]
    sin_t = sin_t_ref[...]
    q_scale = HEAD_DIM ** -0.5

    fm_outs = ((qa_t_ref, True, q_scale), (va_t_ref, False, 1.0), (qb_t_ref, True, q_scale),
               (vb_t_ref, False, 1.0), (iq_t_ref, True, IDX_DIM ** -0.5))
    for i, (ref, rope, scale) in enumerate(fm_outs):
        z = lax.dot_general(wf_ref[i * MIXER_W:(i + 1) * MIXER_W, :], h, NT_DIMS,
                            preferred_element_type=F32)
        if rope:
            z = _rope_feature_major(z, cos_t, sin_t)
        if scale != 1.0:
            z = z * scale
        _store_blocked(ref, z.astype(BF16))

    iw = lax.dot_general(wiw_ref[...], h, NT_DIMS, preferred_element_type=F32)
    _store_blocked(iw_t_ref, iw * w_scale)

    zt = jnp.dot(h, wt_ref[...], preferred_element_type=F32)
    ctab, slo, shi = ctab_ref[...], slo_ref[...], shi_ref[...]
    n_grp = zt.shape[1] // LANES
    roped = [_rope_token_major(zt[:, j * LANES:(j + 1) * LANES], ctab, slo, shi) for j in range(n_grp)]
    per_mixer = MIXER_W // LANES
    ka = jnp.concatenate(roped[:per_mixer], axis=1)
    kb = jnp.concatenate(roped[per_mixer:2 * per_mixer], axis=1)
    ka_ref[0] = ka.astype(BF16)
    kb_ref[0] = kb.astype(BF16)
    ik_ref[0] = roped[2 * per_mixer][:, :IDX_DIM].astype(BF16)
    tm = ka.shape[0]
    km_ref[0] = jnp.mean(ka.reshape(tm // MOBA_BLOCK, MOBA_BLOCK, MIXER_W), axis=1, keepdims=True)

    zg = jnp.dot(h, wg_ref[...], preferred_element_type=F32)
    d_model = zg.shape[1] // 2
    ga_ref[0] = zg[:, :d_model]
    gb_ref[0] = zg[:, d_model:]


def _rope_tables(seq):
    inv = 1.0 / (ROPE_THETA ** (jnp.arange(0, ROT_DIM, 2, dtype=F32) / ROT_DIM))
    ang = jnp.arange(seq, dtype=F32)[:, None] * inv[None, :]
    cos, sin = jnp.cos(ang), jnp.sin(ang)
    d = jnp.arange(LANES) % HEAD_DIM
    lo = d < ROT_HALF
    hi = (d >= ROT_HALF) & (d < ROT_DIM)
    f = d % ROT_HALF
    cos_l, sin_l = cos[:, f], sin[:, f]
    ctab = jnp.where(lo | hi, cos_l, 1.0)
    slo = jnp.where(lo, -sin_l, 0.0)
    shi = jnp.where(hi, sin_l, 0.0)
    return cos.T, sin.T, ctab, slo, shi


def _in_projection(x, g_attn, w_in, tm=512):
    b, s, d = x.shape
    splits = (MIXER_W,) * 6 + (IDX_HEADS * IDX_DIM, IDX_DIM, IDX_HEADS, d, d)
    offs = [0]
    for w in splits:
        offs.append(offs[-1] + w)
    wqa, wka, wva, wqb, wkb, wvb, wiq, wik, wiw, wga, wgb = (
        w_in[:, offs[i]:offs[i + 1]] for i in range(len(splits)))
    wf = jnp.concatenate([wqa, wva, wqb, wvb, wiq], axis=1).T.astype(BF16)
    wiw_t = wiw.T.astype(BF16)
    wt = jnp.concatenate([wka, wkb, wik, jnp.zeros((d, LANES - IDX_DIM), w_in.dtype)], axis=1).astype(BF16)
    wg = jnp.concatenate([wga, wgb], axis=1).astype(BF16)
    cos_t, sin_t, ctab, slo, shi = _rope_tables(s)
    n_blk = s // MOBA_BLOCK
    fm_shape = jax.ShapeDtypeStruct((b, s // Q_TILE, MIXER_W, Q_TILE), BF16)
    tok_shape = jax.ShapeDtypeStruct((b, s, MIXER_W), BF16)
    fm_spec = pl.BlockSpec((1, tm // Q_TILE, MIXER_W, Q_TILE), lambda bi, ti: (bi, ti, 0, 0))
    tok_spec = pl.BlockSpec((1, tm, MIXER_W), lambda bi, ti: (bi, ti, 0))
    gate_spec = pl.BlockSpec((1, tm, d), lambda bi, ti: (bi, ti, 0))
    outs = pl.pallas_call(
        functools.partial(_inproj_kernel, w_scale=IDX_HEADS ** -0.5),
        grid=(b, s // tm),
        in_specs=[
            pl.BlockSpec((1, tm, d), lambda bi, ti: (bi, ti, 0)),
            _const_spec((1, d)),
            _const_spec(wf.shape), _const_spec(wiw_t.shape), _const_spec(wt.shape), _const_spec(wg.shape),
            pl.BlockSpec((ROT_HALF, tm), lambda bi, ti: (0, ti)),
            pl.BlockSpec((ROT_HALF, tm), lambda bi, ti: (0, ti)),
            pl.BlockSpec((tm, LANES), lambda bi, ti: (ti, 0)),
            pl.BlockSpec((tm, LANES), lambda bi, ti: (ti, 0)),
            pl.BlockSpec((tm, LANES), lambda bi, ti: (ti, 0)),
        ],
        out_specs=[
            fm_spec, fm_spec, fm_spec, fm_spec, fm_spec,
            pl.BlockSpec((1, tm // Q_TILE, IDX_HEADS, Q_TILE), lambda bi, ti: (bi, ti, 0, 0)),
            tok_spec, tok_spec,
            pl.BlockSpec((1, tm, IDX_DIM), lambda bi, ti: (bi, ti, 0)),
            gate_spec, gate_spec,
            pl.BlockSpec((1, tm // MOBA_BLOCK, 1, MIXER_W), lambda bi, ti: (bi, ti, 0, 0)),
        ],
        out_shape=[
            fm_shape, fm_shape, fm_shape, fm_shape, fm_shape,
            jax.ShapeDtypeStruct((b, s // Q_TILE, IDX_HEADS, Q_TILE), F32),
            tok_shape, tok_shape,
            jax.ShapeDtypeStruct((b, s, IDX_DIM), BF16),
            jax.ShapeDtypeStruct((b, s, d), F32), jax.ShapeDtypeStruct((b, s, d), F32),
            jax.ShapeDtypeStruct((b, n_blk, 1, MIXER_W), F32),
        ],
        compiler_params=_cparams(("parallel", "parallel")),
        name="in_projection",
    )(x, g_attn.reshape(1, d), wf, wiw_t, wt, wg, cos_t, sin_t, ctab, slo, shi)
    return outs


MASKED = -1e30


HEADS_PER_GROUP = LANES // HEAD_DIM


def _one_head_of_pair(q_pair, hh):
    row = lax.broadcasted_iota(I32, q_pair.shape, 0)
    return jnp.where((row // HEAD_DIM) == hh, q_pair, jnp.zeros_like(q_pair))


def _head_lanes(h):
    g = h // HEADS_PER_GROUP
    return slice(g * LANES, (g + 1) * LANES)


def _head_rows(h):
    return slice(h * HEAD_DIM, (h + 1) * HEAD_DIM)


ATTN_GROUP = 4
ONES_ROWS = 16
ACC_ROWS = HEAD_DIM + ONES_ROWS


def _split_heads(q_all, qh_ref):
    for h in range(N_HEADS):
        qh_ref[h] = _one_head_of_pair(q_all[_head_lanes(h), :], h % HEADS_PER_GROUP)


def _weighted_values(s, m, v_t):
    p = jnp.exp(s - m).astype(BF16)
    lhs = jnp.concatenate([v_t, jnp.ones((ONES_ROWS, v_t.shape[1]), BF16)], axis=0)
    return jnp.dot(lhs, p, preferred_element_type=F32)


def _loop_in_groups(n, body, init, width=4):
    carry = lax.fori_loop(0, n // width, lambda k, c: body(width * k, width, c), init)
    done = (n // width) * width
    part = width // 2
    while part >= 1:
        has_part = ((n - done) // part) % 2 == 1
        carry = lax.cond(has_part, lambda c, done=done, part=part: body(done, part, c), lambda c: c, carry)
        done = done + jnp.where(has_part, part, 0)
        part //= 2
    return carry


def _store_head_output(o_t_ref, h, acc):
    o_t_ref[_head_rows(h), :] = acc[:HEAD_DIM, :] / acc[HEAD_DIM:HEAD_DIM + 1, :]


def _moba_kernel(q_t_ref, k_ref, v_t_ref, km_ref, o_ref, qh_ref, bias_ref, s_ref, o_t_ref, *acc_refs):
    i = pl.program_id(1)
    tq = q_t_ref.shape[3]
    n_blk = km_ref.shape[1]
    _split_heads(q_t_ref[0, 0], qh_ref)

    blk = lax.broadcasted_iota(I32, (n_blk, tq), 0)
    for h in range(N_HEADS):
        gate = jnp.dot(km_ref[0, :, _head_lanes(h)], qh_ref[h].astype(F32), preferred_element_type=F32)
        beaten_by = jnp.zeros((n_blk, tq), I32)
        for m_blk in range(n_blk):
            gm = gate[m_blk:m_blk + 1, :]
            beats = ((gm > gate) | ((gm == gate) & (m_blk < blk))) & (m_blk < i)
            beaten_by = beaten_by + beats.astype(I32)
        keep = (blk < i) & (beaten_by < MOBA_TOPK)
        bias_ref[h] = jnp.where(keep | (blk == i), 0.0, MASKED)

    def key_rows(j):
        return pl.ds(pl.multiple_of(j * tq, tq), tq)

    def block_bias(h, j):
        return bias_ref[h, pl.ds(j, 1), :]

    causal = lax.broadcasted_iota(I32, (tq, tq), 0) <= lax.broadcasted_iota(I32, (tq, tq), 1)
    for g in range(N_HEADS // ATTN_GROUP):
        heads = list(enumerate(range(g * ATTN_GROUP, (g + 1) * ATTN_GROUP)))

        def scores(h, j):
            return jnp.dot(k_ref[0, key_rows(j), _head_lanes(h)], qh_ref[h], preferred_element_type=F32)

        own_max = []
        for gi, h in heads:
            s = jnp.where(causal, scores(h, i), MASKED)
            s_ref[gi, key_rows(i), :] = s
            own_max.append(jnp.max(s, axis=0, keepdims=True))

        def past_scores(j0, count, mx, heads=heads, scores=scores):
            mx = list(mx)
            for j in (j0 + d for d in range(count)):
                for gi, h in heads:
                    s = scores(h, j)
                    s_ref[gi, key_rows(j), :] = s
                    mx[gi] = jnp.maximum(mx[gi], jnp.max(s, axis=0, keepdims=True) + block_bias(h, j))
            return tuple(mx)

        mx = _loop_in_groups(i, past_scores, tuple(own_max))
        for gi, _ in heads:
            acc_refs[gi][...] = jnp.zeros(acc_refs[gi].shape, F32)

        def accumulate(j0, count, carry, heads=heads, mx=mx):
            for gi, h in heads:
                acc_refs[gi][...] += sum(
                    _weighted_values(s_ref[gi, key_rows(j0 + d), :], mx[gi] - block_bias(h, j0 + d),
                                     v_t_ref[0, j0 + d, _head_rows(h), :]) for d in range(count))
            return carry

        _loop_in_groups(i + 1, accumulate, 0)
        for gi, h in heads:
            _store_head_output(o_t_ref, h, acc_refs[gi][...])
    o_ref[0] = o_t_ref[...].T.astype(BF16)


def _attention_scratch(s, w, tq):
    return ([pltpu.VMEM((N_HEADS, LANES, tq), BF16)],
            [pltpu.VMEM((ATTN_GROUP, s, tq), F32), pltpu.VMEM((w, tq), F32)]
            + [pltpu.VMEM((ACC_ROWS, tq), F32)] * ATTN_GROUP)


def _resident_spec(shape):
    nd = len(shape)
    return pl.BlockSpec(shape, lambda bi, i: (bi,) + (0,) * (nd - 1), pipeline_mode=pl.Buffered(1))


def _moba_attention(qa_t, ka, va_t, km):
    b, n_blk, w, tq = qa_t.shape
    s = n_blk * tq
    qh, work = _attention_scratch(s, w, tq)
    return pl.pallas_call(
        _moba_kernel,
        grid=(b, n_blk),
        in_specs=[
            pl.BlockSpec((1, 1, w, tq), lambda bi, i: (bi, i, 0, 0)),
            _resident_spec((1, s, w)),
            _resident_spec((1, n_blk, w, tq)),
            _resident_spec((1, n_blk, w)),
        ],
        out_specs=pl.BlockSpec((1, tq, w), lambda bi, i: (bi, i, 0)),
        out_shape=jax.ShapeDtypeStruct((b, s, w), BF16),
        scratch_shapes=qh + [pltpu.VMEM((N_HEADS, n_blk, tq), F32)] + work,
        compiler_params=_cparams(("parallel", "arbitrary")),
        name="moba_attention",
    )(qa_t, ka, va_t, km)


INT_MIN = -2 ** 31


def _sortable_key(s):
    bits = lax.bitcast_convert_type(s, I32)
    return bits ^ ((bits >> 31) & 0x7FFFFFFF)


def _column_count(mask):
    tk, tq = mask.shape
    return jnp.sum(mask.astype(I32).reshape(tk // SUBLANES, SUBLANES, tq), axis=0)


I16 = jnp.int16
HALF_BITS = 16
HALF_MASK = 2 ** HALF_BITS - 1
I16_MIN = -2 ** (HALF_BITS - 1)
PACKED_ROWS = 2 * SUBLANES


def _packed_column_count(mask):
    ones = mask.astype(I16)
    cnt = ones[0:PACKED_ROWS, :]
    for r in range(1, ones.shape[0] // PACKED_ROWS):
        cnt = cnt + ones[r * PACKED_ROWS:(r + 1) * PACKED_ROWS, :]
    return cnt


def _packed_total(count16):
    return jnp.sum(count16.astype(I32), axis=0, keepdims=True)


def _dsa_kernel(iq_t_ref, ik_ref, iw_t_ref, q_t_ref, k_ref, v_t_ref, o_ref,
                key_ref, hi_ref, lo_ref, lo2_ref, bias_ref, qh_ref, s_ref, o_t_ref, *acc_refs,
                topk, index_bits):
    t = pl.program_id(1)
    tq = q_t_ref.shape[3]
    tk = tq
    n_chunks = t + 1
    qpos = t * tq + lax.broadcasted_iota(I32, (1, tq), 1)
    krow = lax.broadcasted_iota(I32, (tk, tq), 0)

    def chunk_rows(c):
        return pl.ds(pl.multiple_of(c * tk, tk), tk)

    def total(count8):
        return jnp.sum(count8, axis=0, keepdims=True)

    def count_over_chunks(pred):
        def body(c, cnt):
            return cnt + _column_count(pred(key_ref[chunk_rows(c), :], c * tk + krow))
        return total(lax.fori_loop(0, n_chunks, body, jnp.zeros((SUBLANES, tq), I32)))

    iq = iq_t_ref[0, 0]
    iw = iw_t_ref[0, 0]

    def score_chunks(c0, count, carry):
        for c in (c0 + d for d in range(count)):
            ik_c = ik_ref[0, chunk_rows(c), :]
            score = jnp.zeros((tk, tq), F32)
            for h in range(IDX_HEADS):
                rel = jnp.dot(ik_c, iq[h * IDX_DIM:(h + 1) * IDX_DIM, :], preferred_element_type=F32)
                score = score + jnp.maximum(rel, 0.0) * iw[h:h + 1, :]
            key = jnp.where(c * tk + krow <= qpos, _sortable_key(score), INT_MIN)
            key_ref[chunk_rows(c), :] = key
            hi_ref[chunk_rows(c), :] = (key >> HALF_BITS).astype(I16)
            lo_ref[chunk_rows(c), :] = ((key & HALF_MASK) + I16_MIN).astype(I16)
        return carry

    _loop_in_groups(n_chunks, score_chunks, 0)

    def kth_largest_half(half_ref, kth):
        def half_bit(b, prefix):
            cand = prefix | jnp.left_shift(jnp.int32(1), HALF_BITS - 1 - b)
            cand_half = (cand + I16_MIN).astype(I16)

            def body(c, cnt):
                return cnt + _packed_column_count(half_ref[chunk_rows(c), :] >= cand_half)

            n_ge = _packed_total(lax.fori_loop(0, n_chunks, body, jnp.zeros((PACKED_ROWS, tq), I16)))
            return jnp.where(n_ge >= kth, cand, prefix)

        return lax.fori_loop(0, HALF_BITS, half_bit, jnp.zeros((1, tq), I32)) + I16_MIN

    thr_hi = kth_largest_half(hi_ref, topk)
    thr_hi_half = thr_hi.astype(I16)

    def above_hi(c, cnt):
        hi = hi_ref[chunk_rows(c), :]
        lo2_ref[chunk_rows(c), :] = jnp.where(hi == thr_hi_half, lo_ref[chunk_rows(c), :], jnp.int16(I16_MIN))
        return cnt + _packed_column_count(hi > thr_hi_half)

    n_gt_hi = _packed_total(lax.fori_loop(0, n_chunks, above_hi, jnp.zeros((PACKED_ROWS, tq), I16)))
    thr_lo = kth_largest_half(lo2_ref, topk - n_gt_hi)
    thr = jnp.left_shift(thr_hi, HALF_BITS) | (thr_lo - I16_MIN)

    n_gt = count_over_chunks(lambda keys, kpos: keys > thr)
    n_eq = count_over_chunks(lambda keys, kpos: keys == thr)
    need = topk - n_gt
    has_surplus = jnp.any((n_eq > need) & (thr != INT_MIN))

    def last_tie_index():
        def index_bit(b, prefix):
            cand = prefix | jnp.left_shift(jnp.int32(1), index_bits - 1 - b)
            n_before = count_over_chunks(lambda keys, kpos: (keys == thr) & (kpos < cand))
            return jnp.where(n_before < need, cand, prefix)
        return lax.fori_loop(0, index_bits, index_bit, jnp.zeros((1, tq), I32))

    tie_end = lax.cond(has_surplus, last_tie_index, lambda: jnp.full((1, tq), 2 ** index_bits, I32))

    def bias_chunk(c, carry):
        keys = key_ref[chunk_rows(c), :]
        kpos = c * tk + krow
        chosen = ((keys > thr) | ((keys == thr) & (kpos <= tie_end))) & (kpos <= qpos)
        bias_ref[chunk_rows(c), :] = jnp.where(chosen, 0.0, MASKED)
        return carry

    lax.fori_loop(0, n_chunks, bias_chunk, 0)

    _split_heads(q_t_ref[0, 0], qh_ref)
    for g in range(N_HEADS // ATTN_GROUP):
        heads = list(enumerate(range(g * ATTN_GROUP, (g + 1) * ATTN_GROUP)))

        def masked_scores(c0, count, mx, heads=heads):
            mx = list(mx)
            for c in (c0 + d for d in range(count)):
                for gi, h in heads:
                    s = jnp.dot(k_ref[0, chunk_rows(c), _head_lanes(h)], qh_ref[h], preferred_element_type=F32)
                    s = s + bias_ref[chunk_rows(c), :]
                    s_ref[gi, chunk_rows(c), :] = s
                    mx[gi] = jnp.maximum(mx[gi], jnp.max(s, axis=0, keepdims=True))
            return tuple(mx)

        mx = _loop_in_groups(n_chunks, masked_scores, (jnp.full((1, tq), MASKED, F32),) * ATTN_GROUP)
        for gi, _ in heads:
            acc_refs[gi][...] = jnp.zeros(acc_refs[gi].shape, F32)

        def accumulate(c0, count, carry, heads=heads, mx=mx):
            for gi, h in heads:
                acc_refs[gi][...] += sum(
                    _weighted_values(s_ref[gi, chunk_rows(c0 + d), :], mx[gi], v_t_ref[0, c0 + d, _head_rows(h), :])
                    for d in range(count))
            return carry

        _loop_in_groups(n_chunks, accumulate, 0)
        for gi, h in heads:
            _store_head_output(o_t_ref, h, acc_refs[gi][...])
    o_ref[0] = o_t_ref[...].T.astype(BF16)


def _dsa_attention(iq_t, ik, iw_t, qb_t, kb, vb_t):
    b, n_blk, w, tq = qb_t.shape
    s = n_blk * tq
    topk = min(DSA_TOPK_MAX, s // 4)
    qh, work = _attention_scratch(s, w, tq)
    tile_spec = lambda rows: pl.BlockSpec((1, 1, rows, tq), lambda bi, ti: (bi, ti, 0, 0))
    return pl.pallas_call(
        functools.partial(_dsa_kernel, topk=topk, index_bits=max(1, (s - 1).bit_length())),
        grid=(b, n_blk),
        in_specs=[
            tile_spec(IDX_HEADS * IDX_DIM),
            _resident_spec((1, s, IDX_DIM)),
            tile_spec(IDX_HEADS),
            tile_spec(w),
            _resident_spec((1, s, w)),
            _resident_spec((1, n_blk, w, tq)),
        ],
        out_specs=pl.BlockSpec((1, tq, w), lambda bi, ti: (bi, ti, 0)),
        out_shape=jax.ShapeDtypeStruct((b, s, w), BF16),
        scratch_shapes=[pltpu.VMEM((s, tq), I32)] + [pltpu.VMEM((s, tq), I16)] * 3
        + [pltpu.VMEM((s, tq), F32)] + qh + work,
        compiler_params=_cparams(("parallel", "arbitrary")),
        name="dsa_attention",
    )(iq_t, ik, iw_t, qb_t, kb, vb_t)


ROUTER_ROWS = 8 + N_EXPERTS
EXPERT_TOPK = 2


def _rms_norm(x, g):
    ms = jnp.mean(x * x, axis=-1, keepdims=True)
    return (x * lax.rsqrt(ms + RMS_EPS)) * g


def _first_index_of_max(vals, idx):
    top = jnp.max(vals, axis=0, keepdims=True)
    first = jnp.min(jnp.where(vals == top, idx, vals.shape[0]), axis=0, keepdims=True)
    return top, first


def _merge_router_kernel(oa_ref, ob_ref, ga_ref, gb_ref, x_ref, wa_ref, wb_ref, wo_ref, g_ref,
                         wr_ref, br_ref, x1_ref, h2_ref, ids_ref, cw_ref, rank_ref, cnt_ref):
    @pl.when(pl.program_id(0) == 0)
    def _():
        cnt_ref[...] = jnp.zeros_like(cnt_ref)

    a = jnp.dot(oa_ref[...], wa_ref[...], preferred_element_type=F32)
    b = jnp.dot(ob_ref[...], wb_ref[...], preferred_element_type=F32)
    merged = jax.nn.sigmoid(ga_ref[...]) * a + jax.nn.sigmoid(gb_ref[...]) * b
    x1 = x_ref[...] + jnp.dot(merged.astype(BF16), wo_ref[...], preferred_element_type=F32)
    x1_ref[...] = x1
    h2 = _rms_norm(x1, g_ref[...])
    h2_ref[...] = h2
    tm = h2.shape[0]

    logits = lax.dot_general(wr_ref[...], h2, NT_DIMS, preferred_element_type=F32,
                             precision=lax.Precision.HIGHEST) + br_ref[...]
    grp = logits[0:N_GROUPS, :]
    g_top, g_idx = _first_index_of_max(grp, lax.broadcasted_iota(I32, grp.shape, 0))
    p_grp = 1.0 / jnp.sum(jnp.exp(grp - g_top), axis=0, keepdims=True)
    in_grp = logits[8:8 + EXPERTS_PER_GROUP, :]
    for gi in range(1, N_GROUPS):
        rows = slice(8 + gi * EXPERTS_PER_GROUP, 8 + (gi + 1) * EXPERTS_PER_GROUP)
        in_grp = jnp.where(g_idx == gi, logits[rows, :], in_grp)
    e_iota = lax.broadcasted_iota(I32, in_grp.shape, 0)
    v0, i0 = _first_index_of_max(in_grp, e_iota)
    v1, i1 = _first_index_of_max(jnp.where(e_iota == i0, -jnp.inf, in_grp), e_iota)
    e1 = jnp.exp(v1 - v0)
    denom = 1.0 + e1
    ids = jnp.concatenate([g_idx * EXPERTS_PER_GROUP + i0, g_idx * EXPERTS_PER_GROUP + i1], axis=0)
    ids_ref[0] = ids
    cw_ref[0] = jnp.concatenate([p_grp * (1.0 / denom), p_grp * (e1 / denom)], axis=0)

    before = (lax.broadcasted_iota(I32, (tm, tm), 0) < lax.broadcasted_iota(I32, (tm, tm), 1)).astype(BF16)
    expert = lax.broadcasted_iota(I32, (N_EXPERTS, tm), 0)
    ranks = []
    for slot in range(EXPERT_TOPK):
        onehot = expert == ids[slot:slot + 1, :]
        seen = jnp.dot(onehot.astype(BF16), before, preferred_element_type=F32) + cnt_ref[...]
        ranks.append(jnp.sum(jnp.where(onehot, seen, 0.0), axis=0, keepdims=True))
        cnt_ref[...] += jnp.sum(onehot.astype(F32), axis=1, keepdims=True)
    rank_ref[0] = jnp.concatenate(ranks, axis=0).astype(I32)


def _merge_and_route(o_a, o_b, ga, gb, x, w_br_a, w_br_b, w_out, g_ffn, w_grp, b_grp, w_rt, b_rt, tm=512):
    t, d = x.shape
    n_tiles = t // tm
    wr = jnp.zeros((ROUTER_ROWS, d), F32).at[0:N_GROUPS].set(w_grp.T).at[8:].set(w_rt.T)
    br = jnp.zeros((ROUTER_ROWS, 1), F32).at[0:N_GROUPS, 0].set(b_grp).at[8:, 0].set(b_rt)
    row_spec = lambda w: pl.BlockSpec((tm, w), lambda i: (i, 0))
    slot_spec = pl.BlockSpec((1, EXPERT_TOPK, tm), lambda i: (i, 0, 0))
    slot_shape = lambda dt: jax.ShapeDtypeStruct((n_tiles, EXPERT_TOPK, tm), dt)
    return pl.pallas_call(
        _merge_router_kernel,
        grid=(n_tiles,),
        in_specs=[row_spec(MIXER_W), row_spec(MIXER_W), row_spec(d), row_spec(d), row_spec(d),
                  _const_spec((MIXER_W, d)), _const_spec((MIXER_W, d)), _const_spec((d, d)),
                  _const_spec((1, d)), _const_spec((ROUTER_ROWS, d)), _const_spec((ROUTER_ROWS, 1))],
        out_specs=[row_spec(d), row_spec(d), slot_spec, slot_spec, slot_spec,
                   pl.BlockSpec((N_EXPERTS, 1), lambda i: (0, 0))],
        out_shape=[jax.ShapeDtypeStruct((t, d), F32), jax.ShapeDtypeStruct((t, d), F32),
                   slot_shape(I32), slot_shape(F32), slot_shape(I32),
                   jax.ShapeDtypeStruct((N_EXPERTS, 1), F32)],
        compiler_params=_cparams(("arbitrary",)),
        name="merge_and_route",
    )(o_a, o_b, ga, gb, x, w_br_a.astype(BF16), w_br_b.astype(BF16), w_out.astype(BF16),
      g_ffn.reshape(1, d), wr, br)


EXPERT_ROWS = 512
DMA_ISSUE_UNROLL = 8


def _row(ref, r):
    return ref.at[pl.ds(r, 1), :]


def _tile_dest(dest_ref, tm, slot, r):
    return dest_ref[(pl.program_id(0) * EXPERT_TOPK + slot) * tm + r]


def _dispatch_kernel(dest_ref, h2_ref, xs_init_hbm, xs_hbm, sem):
    del xs_init_hbm
    tm = h2_ref.shape[0]

    def issue(r, carry):
        for slot in range(EXPERT_TOPK):
            pltpu.make_async_copy(_row(h2_ref, r), _row(xs_hbm, _tile_dest(dest_ref, tm, slot, r)), sem).start()
        return carry

    lax.fori_loop(0, tm, issue, 0, unroll=DMA_ISSUE_UNROLL)
    for slot in range(EXPERT_TOPK):
        pltpu.make_async_copy(h2_ref, xs_hbm.at[pl.ds(0, tm), :], sem).wait()


def _dispatch(dest, h2, n_rows, tm):
    t, d = h2.shape
    return pl.pallas_call(
        _dispatch_kernel,
        grid_spec=pltpu.PrefetchScalarGridSpec(
            num_scalar_prefetch=1,
            grid=(t // tm,),
            in_specs=[pl.BlockSpec((tm, d), lambda i, dest: (i, 0)), pl.BlockSpec(memory_space=pl.ANY)],
            out_specs=pl.BlockSpec(memory_space=pl.ANY),
            scratch_shapes=[pltpu.SemaphoreType.DMA(())],
        ),
        out_shape=jax.ShapeDtypeStruct((n_rows, d), F32),
        input_output_aliases={2: 0},
        compiler_params=_cparams(("arbitrary",)),
        name="moe_dispatch",
    )(dest, h2, jnp.zeros((n_rows, d), F32))


def _expert_kernel(blk_exp_ref, n_used_ref, xs_ref, wgu_ref, wd_ref, ys_ref):
    del blk_exp_ref
    used = pl.program_id(0) < n_used_ref[0]

    @pl.when(used)
    def _():
        gu = jnp.dot(xs_ref[...].astype(BF16), wgu_ref[0], preferred_element_type=F32)
        gate, up = gu[:, :D_EXPERT], gu[:, D_EXPERT:]
        hid = (gate * jax.nn.sigmoid(gate)) * up
        ys_ref[...] = jnp.dot(hid.astype(BF16), wd_ref[0], preferred_element_type=F32)

    @pl.when(jnp.logical_not(used))
    def _():
        ys_ref[...] = jnp.zeros_like(ys_ref)


def _experts(blk_exp, n_used, xs, w_gate_up, w_down):
    n_rows, d = xs.shape
    n_blocks = n_rows // EXPERT_ROWS
    return pl.pallas_call(
        _expert_kernel,
        grid_spec=pltpu.PrefetchScalarGridSpec(
            num_scalar_prefetch=2,
            grid=(n_blocks,),
            in_specs=[pl.BlockSpec((EXPERT_ROWS, d), lambda i, be, nu: (i, 0)),
                      pl.BlockSpec((1, d, 2 * D_EXPERT), lambda i, be, nu: (be[i], 0, 0)),
                      pl.BlockSpec((1, D_EXPERT, d), lambda i, be, nu: (be[i], 0, 0))],
            out_specs=pl.BlockSpec((EXPERT_ROWS, d), lambda i, be, nu: (i, 0)),
        ),
        out_shape=jax.ShapeDtypeStruct((n_rows, d), F32),
        compiler_params=_cparams(("arbitrary",)),
        name="moe_experts",
    )(blk_exp, n_used, xs, w_gate_up, w_down)


def _combine_ple_kernel(dest_ref, x1_ref, cw_ref, p_ref, ys_hbm, g_ple_ref, wpg_ref, wpp_ref, g_fin_ref,
                        out_ref, y_buf, sem):
    tm = x1_ref.shape[0]

    def issue(r, carry):
        for slot in range(EXPERT_TOPK):
            pltpu.make_async_copy(_row(ys_hbm, _tile_dest(dest_ref, tm, slot, r)), _row(y_buf.at[slot], r),
                                  sem).start()
        return carry

    lax.fori_loop(0, tm, issue, 0, unroll=DMA_ISSUE_UNROLL)
    for slot in range(EXPERT_TOPK):
        pltpu.make_async_copy(ys_hbm.at[pl.ds(0, tm), :], y_buf.at[slot], sem).wait()

    cw = cw_ref[...]
    x2 = x1_ref[...] + (y_buf[0] * cw[:, 0:1] + y_buf[1] * cw[:, 1:2])
    h3 = _rms_norm(x2, g_ple_ref[...]).astype(BF16)
    gate = jax.nn.sigmoid(jnp.dot(h3, wpg_ref[...], preferred_element_type=F32))
    proj = jnp.dot(p_ref[...].astype(BF16), wpp_ref[...], preferred_element_type=F32)
    out_ref[...] = _rms_norm(x2 + gate * proj, g_fin_ref[...])


def _combine_ple(dest, x1, cw_tok, p, ys, g_ple, w_ple_gate, w_ple_proj, g_final, tm):
    t, d = x1.shape
    row_spec = lambda w: pl.BlockSpec((tm, w), lambda i, dest: (i, 0))
    return pl.pallas_call(
        _combine_ple_kernel,
        grid_spec=pltpu.PrefetchScalarGridSpec(
            num_scalar_prefetch=1,
            grid=(t // tm,),
            in_specs=[row_spec(d), row_spec(EXPERT_TOPK), row_spec(p.shape[1]),
                      pl.BlockSpec(memory_space=pl.ANY),
                      _const_spec((1, d)), _const_spec((d, d)), _const_spec((p.shape[1], d)), _const_spec((1, d))],
            out_specs=row_spec(d),
            scratch_shapes=[pltpu.VMEM((EXPERT_TOPK, tm, d), F32), pltpu.SemaphoreType.DMA(())],
        ),
        out_shape=jax.ShapeDtypeStruct((t, d), F32),
        compiler_params=_cparams(("arbitrary",)),
        name="combine_ple",
    )(dest, x1, cw_tok, p, ys, g_ple.reshape(1, d), w_ple_gate.astype(BF16), w_ple_proj.astype(BF16),
      g_final.reshape(1, d))


def _layer(x, p, g_attn, w_in, w_br_a, w_br_b, w_out, g_ffn, w_grp, b_grp, w_rt, b_rt,
           w_gate, w_up, w_down, g_ple, w_ple_gate, w_ple_proj, g_final):
    b, s, d = x.shape
    t = b * s
    (qa_t, va_t, qb_t, vb_t, iq_t, iw_t, ka, kb, ik, ga, gb, km) = _in_projection(x, g_attn, w_in)
    o_a = _moba_attention(qa_t, ka, va_t, km.reshape(b, s // MOBA_BLOCK, MIXER_W))
    o_b = _dsa_attention(iq_t, ik, iw_t, qb_t, kb, vb_t)
    x1, h2, ids, cw, rank, counts = _merge_and_route(
        o_a.reshape(t, MIXER_W), o_b.reshape(t, MIXER_W), ga.reshape(t, d), gb.reshape(t, d),
        x.reshape(t, d), w_br_a, w_br_b, w_out, g_ffn, w_grp, b_grp, w_rt, b_rt)

    counts = counts[:, 0].astype(I32)
    padded = ((counts + EXPERT_ROWS - 1) // EXPERT_ROWS) * EXPERT_ROWS
    pend = jnp.cumsum(padded)
    pstart = pend - padded
    n_blocks = -(-(t * EXPERT_TOPK) // EXPERT_ROWS) + N_EXPERTS
    experts = jnp.arange(N_EXPERTS, dtype=I32)
    dest = jnp.sum(jnp.where(ids[..., None] == experts, pstart, 0), axis=-1) + rank
    block_row0 = jnp.arange(n_blocks, dtype=I32) * EXPERT_ROWS
    blk_exp = jnp.minimum(jnp.sum((pend[None, :] <= block_row0[:, None]).astype(I32), axis=1), N_EXPERTS - 1)
    n_used = (pend[-1:] // EXPERT_ROWS).astype(I32)

    route_tile = dest.shape[2]
    dest = dest.reshape(-1)
    xs = _dispatch(dest, h2, n_blocks * EXPERT_ROWS, route_tile)
    w_gate_up = jnp.concatenate([w_gate, w_up], axis=-1).astype(BF16)
    ys = _experts(blk_exp, n_used, xs, w_gate_up, w_down.astype(BF16))
    cw_tok = jnp.swapaxes(cw, 1, 2).reshape(t, EXPERT_TOPK)
    out = _combine_ple(dest, x1, cw_tok, p.reshape(t, p.shape[-1]), ys, g_ple, w_ple_gate, w_ple_proj, g_final,
                       route_tile)
    return out.reshape(b, s, d)


def kernel(x, p, g_attn, w_in, w_br_a, w_br_b, w_out, g_ffn, w_grp, b_grp, w_rt, b_rt, w_gate, w_up, w_down, g_ple, w_ple_gate, w_ple_proj, g_final):
    depth = w_in.shape[0]
    assert depth == 1, "the final RMSNorm is fused into the last layer's kernel"
    i = 0
    return _layer(x, p[i], g_attn[i], w_in[i], w_br_a[i], w_br_b[i], w_out[i], g_ffn[i], w_grp[i], b_grp[i],
                  w_rt[i], b_rt[i], w_gate[i], w_up[i], w_down[i], g_ple[i], w_ple_gate[i], w_ple_proj[i], g_final)
```

```python
import functools

import jax
import jax.numpy as jnp
from jax import lax
from jax.experimental import pallas as pl
from jax.experimental.pallas import tpu as pltpu

F32 = jnp.float32
BF16 = jnp.bfloat16
I32 = jnp.int32

HEAD_DIM = 64
N_HEADS = 8
ROT_DIM = HEAD_DIM // 4
ROT_HALF = ROT_DIM // 2
ROPE_THETA = 500000.0
MOBA_BLOCK = 256
MOBA_TOPK = 3
IDX_HEADS = 8
IDX_DIM = 64
DSA_TOPK_MAX = 256
N_GROUPS = 4
EXPERTS_PER_GROUP = 8
N_EXPERTS = N_GROUPS * EXPERTS_PER_GROUP
D_EXPERT = 512
PLE_DIM = 256
RMS_EPS = 1e-6
MIXER_W = N_HEADS * HEAD_DIM

Q_TILE = 256
LANES = 128
SUBLANES = 8
VMEM_LIMIT = 56 * 1024 * 1024

NT_DIMS = (((1,), (1,)), ((), ()))


def _cparams(sem):
    return pltpu.CompilerParams(dimension_semantics=sem, vmem_limit_bytes=VMEM_LIMIT)


def _const_spec(shape):
    nd = len(shape)
    return pl.BlockSpec(shape, lambda *_: (0,) * nd, pipeline_mode=pl.Buffered(1))


def _rope_feature_major(z, cos_t, sin_t):
    tm = z.shape[1]
    z3 = z.reshape(N_HEADS, HEAD_DIM, tm)
    x1 = z3[:, 0:ROT_HALF, :]
    x2 = z3[:, ROT_HALF:ROT_DIM, :]
    o1 = x1 * cos_t - x2 * sin_t
    o2 = x2 * cos_t + x1 * sin_t
    return jnp.concatenate([o1, o2, z3[:, ROT_DIM:, :]], axis=1).reshape(N_HEADS * HEAD_DIM, tm)


def _rope_token_major(z, c_tab, s_lo, s_hi):
    up = pltpu.roll(z, LANES - ROT_HALF, 1)
    dn = pltpu.roll(z, ROT_HALF, 1)
    return z * c_tab + up * s_lo + dn * s_hi


def _store_blocked(ref, z):
    for c in range(z.shape[1] // Q_TILE):
        ref[0, c] = z[:, c * Q_TILE:(c + 1) * Q_TILE]


def _inproj_kernel(x_ref, g_ref, wf_ref, wiw_ref, wt_ref, wg_ref, cos_t_ref, sin_t_ref,
                   ctab_ref, slo_ref, shi_ref,
                   qa_t_ref, va_t_ref, qb_t_ref, vb_t_ref, iq_t_ref, iw_t_ref,
                   ka_ref, kb_ref, ik_ref, ga_ref, gb_ref, km_ref, *, w_scale):
    x = x_ref[0]
    ms = jnp.mean(x * x, axis=-1, keepdims=True)
    h = ((x * lax.rsqrt(ms + RMS_EPS)) * g_ref[...]).astype(BF16)
    cos_t = cos_t_ref[...]
    sin_t = sin_t_ref[...]
    q_scale = HEAD_DIM ** -0.5

    fm_outs = ((qa_t_ref, True, q_scale), (va_t_ref, False, 1.0), (qb_t_ref, True, q_scale),
               (vb_t_ref, False, 1.0), (iq_t_ref, True, IDX_DIM ** -0.5))
    for i, (ref, rope, scale) in enumerate(fm_outs):
        z = lax.dot_general(wf_ref[i * MIXER_W:(i + 1) * MIXER_W, :], h, NT_DIMS,
                            preferred_element_type=F32)
        if rope:
            z = _rope_feature_major(z, cos_t, sin_t)
        if scale != 1.0:
            z = z * scale
        _store_blocked(ref, z.astype(BF16))

    iw = lax.dot_general(wiw_ref[...], h, NT_DIMS, preferred_element_type=F32)
    _store_blocked(iw_t_ref, iw * w_scale)

    zt = jnp.dot(h, wt_ref[...], preferred_element_type=F32)
    ctab, slo, shi = ctab_ref[...], slo_ref[...], shi_ref[...]
    n_grp = zt.shape[1] // LANES
    roped = [_rope_token_major(zt[:, j * LANES:(j + 1) * LANES], ctab, slo, shi) for j in range(n_grp)]
    per_mixer = MIXER_W // LANES
    ka = jnp.concatenate(roped[:per_mixer], axis=1)
    kb = jnp.concatenate(roped[per_mixer:2 * per_mixer], axis=1)
    ka_ref[0] = ka.astype(BF16)
    kb_ref[0] = kb.astype(BF16)
    ik_ref[0] = roped[2 * per_mixer][:, :IDX_DIM].astype(BF16)
    tm = ka.shape[0]
    km_ref[0] = jnp.mean(ka.reshape(tm // MOBA_BLOCK, MOBA_BLOCK, MIXER_W), axis=1, keepdims=True)

    zg = jnp.dot(h, wg_ref[...], preferred_element_type=F32)
    d_model = zg.shape[1] // 2
    ga_ref[0] = zg[:, :d_model]
    gb_ref[0] = zg[:, d_model:]


def _rope_tables(seq):
    inv = 1.0 / (ROPE_THETA ** (jnp.arange(0, ROT_DIM, 2, dtype=F32) / ROT_DIM))
    ang = jnp.arange(seq, dtype=F32)[:, None] * inv[None, :]
    cos, sin = jnp.cos(ang), jnp.sin(ang)
    d = jnp.arange(LANES) % HEAD_DIM
    lo = d < ROT_HALF
    hi = (d >= ROT_HALF) & (d < ROT_DIM)
    f = d % ROT_HALF
    cos_l, sin_l = cos[:, f], sin[:, f]
    ctab = jnp.where(lo | hi, cos_l, 1.0)
    slo = jnp.where(lo, -sin_l, 0.0)
    shi = jnp.where(hi, sin_l, 0.0)
    return cos.T, sin.T, ctab, slo, shi


def _in_projection(x, g_attn, w_in, tm=512):
    b, s, d = x.shape
    splits = (MIXER_W,) * 6 + (IDX_HEADS * IDX_DIM, IDX_DIM, IDX_HEADS, d, d)
    offs = [0]
    for w in splits:
        offs.append(offs[-1] + w)
    wqa, wka, wva, wqb, wkb, wvb, wiq, wik, wiw, wga, wgb = (
        w_in[:, offs[i]:offs[i + 1]] for i in range(len(splits)))
    wf = jnp.concatenate([wqa, wva, wqb, wvb, wiq], axis=1).T.astype(BF16)
    wiw_t = wiw.T.astype(BF16)
    wt = jnp.concatenate([wka, wkb, wik, jnp.zeros((d, LANES - IDX_DIM), w_in.dtype)], axis=1).astype(BF16)
    wg = jnp.concatenate([wga, wgb], axis=1).astype(BF16)
    cos_t, sin_t, ctab, slo, shi = _rope_tables(s)
    n_blk = s // MOBA_BLOCK
    fm_shape = jax.ShapeDtypeStruct((b, s // Q_TILE, MIXER_W, Q_TILE), BF16)
    tok_shape = jax.ShapeDtypeStruct((b, s, MIXER_W), BF16)
    fm_spec = pl.BlockSpec((1, tm // Q_TILE, MIXER_W, Q_TILE), lambda bi, ti: (bi, ti, 0, 0))
    tok_spec = pl.BlockSpec((1, tm, MIXER_W), lambda bi, ti: (bi, ti, 0))
    gate_spec = pl.BlockSpec((1, tm, d), lambda bi, ti: (bi, ti, 0))
    outs = pl.pallas_call(
        functools.partial(_inproj_kernel, w_scale=IDX_HEADS ** -0.5),
        grid=(b, s // tm),
        in_specs=[
            pl.BlockSpec((1, tm, d), lambda bi, ti: (bi, ti, 0)),
            _const_spec((1, d)),
            _const_spec(wf.shape), _const_spec(wiw_t.shape), _const_spec(wt.shape), _const_spec(wg.shape),
            pl.BlockSpec((ROT_HALF, tm), lambda bi, ti: (0, ti)),
            pl.BlockSpec((ROT_HALF, tm), lambda bi, ti: (0, ti)),
            pl.BlockSpec((tm, LANES), lambda bi, ti: (ti, 0)),
            pl.BlockSpec((tm, LANES), lambda bi, ti: (ti, 0)),
            pl.BlockSpec((tm, LANES), lambda bi, ti: (ti, 0)),
        ],
        out_specs=[
            fm_spec, fm_spec, fm_spec, fm_spec, fm_spec,
            pl.BlockSpec((1, tm // Q_TILE, IDX_HEADS, Q_TILE), lambda bi, ti: (bi, ti, 0, 0)),
            tok_spec, tok_spec,
            pl.BlockSpec((1, tm, IDX_DIM), lambda bi, ti: (bi, ti, 0)),
            gate_spec, gate_spec,
            pl.BlockSpec((1, tm // MOBA_BLOCK, 1, MIXER_W), lambda bi, ti: (bi, ti, 0, 0)),
        ],
        out_shape=[
            fm_shape, fm_shape, fm_shape, fm_shape, fm_shape,
            jax.ShapeDtypeStruct((b, s // Q_TILE, IDX_HEADS, Q_TILE), F32),
            tok_shape, tok_shape,
            jax.ShapeDtypeStruct((b, s, IDX_DIM), BF16),
            jax.ShapeDtypeStruct((b, s, d), F32), jax.ShapeDtypeStruct((b, s, d), F32),
            jax.ShapeDtypeStruct((b, n_blk, 1, MIXER_W), F32),
        ],
        compiler_params=_cparams(("parallel", "parallel")),
        name="in_projection",
    )(x, g_attn.reshape(1, d), wf, wiw_t, wt, wg, cos_t, sin_t, ctab, slo, shi)
    return outs


MASKED = -1e30


HEADS_PER_GROUP = LANES // HEAD_DIM


def _one_head_of_pair(q_pair, hh):
    row = lax.broadcasted_iota(I32, q_pair.shape, 0)
    return jnp.where((row // HEAD_DIM) == hh, q_pair, jnp.zeros_like(q_pair))


def _head_lanes(h):
    g = h // HEADS_PER_GROUP
    return slice(g * LANES, (g + 1) * LANES)


def _head_rows(h):
    return slice(h * HEAD_DIM, (h + 1) * HEAD_DIM)


ATTN_GROUP = 4
ONES_ROWS = 16
ACC_ROWS = HEAD_DIM + ONES_ROWS


def _split_heads(q_all, qh_ref):
    for h in range(N_HEADS):
        qh_ref[h] = _one_head_of_pair(q_all[_head_lanes(h), :], h % HEADS_PER_GROUP)


def _weighted_values(s, m, v_t):
    p = jnp.exp(s - m).astype(BF16)
    lhs = jnp.concatenate([v_t, jnp.ones((ONES_ROWS, v_t.shape[1]), BF16)], axis=0)
    return jnp.dot(lhs, p, preferred_element_type=F32)


def _loop_in_groups(n, body, init, width=4):
    carry = lax.fori_loop(0, n // width, lambda k, c: body(width * k, width, c), init)
    done = (n // width) * width
    part = width // 2
    while part >= 1:
        has_part = ((n - done) // part) % 2 == 1
        carry = lax.cond(has_part, lambda c, done=done, part=part: body(done, part, c), lambda c: c, carry)
        done = done + jnp.where(has_part, part, 0)
        part //= 2
    return carry


def _store_head_output(o_t_ref, h, acc):
    o_t_ref[_head_rows(h), :] = acc[:HEAD_DIM, :] / acc[HEAD_DIM:HEAD_DIM + 1, :]


def _moba_kernel(q_t_ref, k_ref, v_t_ref, km_ref, o_ref, qh_ref, bias_ref, s_ref, o_t_ref, *acc_refs):
    i = pl.program_id(1)
    tq = q_t_ref.shape[3]
    n_blk = km_ref.shape[1]
    _split_heads(q_t_ref[0, 0], qh_ref)

    blk = lax.broadcasted_iota(I32, (n_blk, tq), 0)
    for h in range(N_HEADS):
        gate = jnp.dot(km_ref[0, :, _head_lanes(h)], qh_ref[h].astype(F32), preferred_element_type=F32)
        gate = jnp.where(blk < i, gate, -jnp.inf)
        keep = blk == i
        for _ in range(MOBA_TOPK):
            _, first = _first_index_of_max(gate, blk)
            taken = (blk == first) & (blk < i)
            keep = keep | taken
            gate = jnp.where(taken, -jnp.inf, gate)
        bias_ref[h] = jnp.where(keep, 0.0, MASKED)

    def key_rows(j):
        return pl.ds(pl.multiple_of(j * tq, tq), tq)

    def block_bias(h, j):
        return bias_ref[h, pl.ds(j, 1), :]

    causal = lax.broadcasted_iota(I32, (tq, tq), 0) <= lax.broadcasted_iota(I32, (tq, tq), 1)
    for g in range(N_HEADS // ATTN_GROUP):
        heads = list(enumerate(range(g * ATTN_GROUP, (g + 1) * ATTN_GROUP)))

        def scores(h, j):
            return jnp.dot(k_ref[0, key_rows(j), _head_lanes(h)], qh_ref[h], preferred_element_type=F32)

        own_max = []
        for gi, h in heads:
            s = jnp.where(causal, scores(h, i), MASKED)
            s_ref[gi, key_rows(i), :] = s
            own_max.append(jnp.max(s, axis=0, keepdims=True))

        def past_scores(j0, count, mx, heads=heads, scores=scores):
            mx = list(mx)
            for j in (j0 + d for d in range(count)):
                for gi, h in heads:
                    s = scores(h, j)
                    s_ref[gi, key_rows(j), :] = s
                    mx[gi] = jnp.maximum(mx[gi], jnp.max(s, axis=0, keepdims=True) + block_bias(h, j))
            return tuple(mx)

        mx = _loop_in_groups(i, past_scores, tuple(own_max))
        for gi, _ in heads:
            acc_refs[gi][...] = jnp.zeros(acc_refs[gi].shape, F32)

        def accumulate(j0, count, carry, heads=heads, mx=mx):
            for gi, h in heads:
                acc_refs[gi][...] += sum(
                    _weighted_values(s_ref[gi, key_rows(j0 + d), :], mx[gi] - block_bias(h, j0 + d),
                                     v_t_ref[0, j0 + d, _head_rows(h), :]) for d in range(count))
            return carry

        _loop_in_groups(i + 1, accumulate, 0)
        for gi, h in heads:
            _store_head_output(o_t_ref, h, acc_refs[gi][...])
    o_ref[0] = o_t_ref[...].T.astype(BF16)


def _attention_scratch(s, w, tq):
    return ([pltpu.VMEM((N_HEADS, LANES, tq), BF16)],
            [pltpu.VMEM((ATTN_GROUP, s, tq), F32), pltpu.VMEM((w, tq), F32)]
            + [pltpu.VMEM((ACC_ROWS, tq), F32)] * ATTN_GROUP)


def _resident_spec(shape):
    nd = len(shape)
    return pl.BlockSpec(shape, lambda bi, i: (bi,) + (0,) * (nd - 1), pipeline_mode=pl.Buffered(1))


def _moba_attention(qa_t, ka, va_t, km):
    b, n_blk, w, tq = qa_t.shape
    s = n_blk * tq
    qh, work = _attention_scratch(s, w, tq)
    return pl.pallas_call(
        _moba_kernel,
        grid=(b, n_blk),
        in_specs=[
            pl.BlockSpec((1, 1, w, tq), lambda bi, i: (bi, i, 0, 0)),
            _resident_spec((1, s, w)),
            _resident_spec((1, n_blk, w, tq)),
            _resident_spec((1, n_blk, w)),
        ],
        out_specs=pl.BlockSpec((1, tq, w), lambda bi, i: (bi, i, 0)),
        out_shape=jax.ShapeDtypeStruct((b, s, w), BF16),
        scratch_shapes=qh + [pltpu.VMEM((N_HEADS, n_blk, tq), F32)] + work,
        compiler_params=_cparams(("parallel", "arbitrary")),
        name="moba_attention",
    )(qa_t, ka, va_t, km)


INT_MIN = -2 ** 31


def _sortable_key(s):
    bits = lax.bitcast_convert_type(s, I32)
    return bits ^ ((bits >> 31) & 0x7FFFFFFF)


def _column_count(mask):
    tk, tq = mask.shape
    return jnp.sum(mask.astype(I32).reshape(tk // SUBLANES, SUBLANES, tq), axis=0)


I16 = jnp.int16
HALF_BITS = 16
I16_MIN = -2 ** (HALF_BITS - 1)
PACKED_ROWS = 2 * SUBLANES
SEARCH_GROUP = 4


def _packed_column_count(mask):
    ones = mask.astype(I16)
    parts = [ones[r:r + PACKED_ROWS, :] for r in range(0, ones.shape[0], PACKED_ROWS)]
    while len(parts) > 1:
        parts = [a + b for a, b in zip(parts[::2], parts[1::2])] + parts[len(parts) - len(parts) % 2:]
    return parts[0]


def _packed_total(count16):
    return jnp.sum(count16.astype(I32), axis=0, keepdims=True)


def _dsa_kernel(iq_t_ref, ik_ref, iw_t_ref, q_t_ref, k_ref, v_t_ref, o_ref,
                key_ref, hi_ref, lo_ref, lo2_ref, bias_ref, qh_ref, s_ref, o_t_ref, *acc_refs,
                topk, index_bits):
    t = pl.program_id(1)
    tq = q_t_ref.shape[3]
    tk = tq
    n_chunks = t + 1
    qpos = t * tq + lax.broadcasted_iota(I32, (1, tq), 1)
    krow = lax.broadcasted_iota(I32, (tk, tq), 0)

    def chunk_rows(c):
        return pl.ds(pl.multiple_of(c * tk, tk), tk)

    def total(count8):
        return jnp.sum(count8, axis=0, keepdims=True)

    def count_over_chunks(pred):
        def body(c, cnt):
            return cnt + _column_count(pred(key_ref[chunk_rows(c), :], c * tk + krow))
        return total(lax.fori_loop(0, n_chunks, body, jnp.zeros((SUBLANES, tq), I32)))

    iq = iq_t_ref[0, 0]
    iw = iw_t_ref[0, 0]

    def score_chunks(c0, count, carry):
        for c in (c0 + d for d in range(count)):
            ik_c = ik_ref[0, chunk_rows(c), :]
            score = jnp.zeros((tk, tq), F32)
            for h in range(IDX_HEADS):
                rel = jnp.dot(ik_c, iq[h * IDX_DIM:(h + 1) * IDX_DIM, :], preferred_element_type=F32)
                score = score + jnp.maximum(rel, 0.0) * iw[h:h + 1, :]
            key = jnp.where(c * tk + krow <= qpos, _sortable_key(score), INT_MIN)
            key_ref[chunk_rows(c), :] = key
            hi_ref[chunk_rows(c), :] = (key >> HALF_BITS).astype(I16)
            lo_ref[chunk_rows(c), :] = key.astype(I16) ^ jnp.int16(I16_MIN)
        return carry

    _loop_in_groups(n_chunks, score_chunks, 0)

    n_groups = (n_chunks + SEARCH_GROUP - 1) // SEARCH_GROUP
    for d in range(SEARCH_GROUP - 1):
        @pl.when(n_chunks + d < n_groups * SEARCH_GROUP)
        def _(d=d):
            lowest = jnp.full((tk, tq), I16_MIN, I16)
            hi_ref[chunk_rows(n_chunks + d), :] = lowest
            lo_ref[chunk_rows(n_chunks + d), :] = lowest

    def packed_counts(n_counts, chunk_masks):
        def body(g, cnts):
            cnts = list(cnts)
            for c in (g * SEARCH_GROUP + d for d in range(SEARCH_GROUP)):
                for i, mask in enumerate(chunk_masks(c)):
                    cnts[i] = cnts[i] + _packed_column_count(mask)
            return tuple(cnts)
        zero = jnp.zeros((PACKED_ROWS, tq), I16)
        return [_packed_total(c) for c in lax.fori_loop(0, n_groups, body, (zero,) * n_counts)]

    def kth_largest_half(half_ref, kth):
        def half_bit(b, prefix):
            cand = prefix | jnp.left_shift(jnp.int32(1), HALF_BITS - 1 - b)
            cand_half = (cand + I16_MIN).astype(I16)
            n_ge, = packed_counts(1, lambda c: [half_ref[chunk_rows(c), :] >= cand_half])
            return jnp.where(n_ge >= kth, cand, prefix)

        return lax.fori_loop(0, HALF_BITS, half_bit, jnp.zeros((1, tq), I32)) + I16_MIN

    thr_hi = kth_largest_half(hi_ref, topk)
    thr_hi_half = thr_hi.astype(I16)

    def above_hi(c):
        hi = hi_ref[chunk_rows(c), :]
        lo2_ref[chunk_rows(c), :] = jnp.where(hi == thr_hi_half, lo_ref[chunk_rows(c), :], jnp.int16(I16_MIN))
        return [hi > thr_hi_half]

    n_gt_hi, = packed_counts(1, above_hi)
    thr_lo = kth_largest_half(lo2_ref, topk - n_gt_hi)
    thr_lo_half = thr_lo.astype(I16)
    thr = jnp.left_shift(thr_hi, HALF_BITS) | (thr_lo - I16_MIN)

    n_gt_lo, n_eq = packed_counts(2, lambda c: [
        lo2_ref[chunk_rows(c), :] > thr_lo_half,
        (hi_ref[chunk_rows(c), :] == thr_hi_half) & (lo_ref[chunk_rows(c), :] == thr_lo_half)])
    need = topk - (n_gt_hi + n_gt_lo)
    has_surplus = jnp.any((n_eq > need) & (thr != INT_MIN))

    def last_tie_index():
        def index_bit(b, prefix):
            cand = prefix | jnp.left_shift(jnp.int32(1), index_bits - 1 - b)
            n_before = count_over_chunks(lambda keys, kpos: (keys == thr) & (kpos < cand))
            return jnp.where(n_before < need, cand, prefix)
        return lax.fori_loop(0, index_bits, index_bit, jnp.zeros((1, tq), I32))

    def bias_with_ties():
        tie_end = last_tie_index()

        def bias_chunk(c, carry):
            keys = key_ref[chunk_rows(c), :]
            kpos = c * tk + krow
            chosen = ((keys > thr) | ((keys == thr) & (kpos <= tie_end))) & (kpos <= qpos)
            bias_ref[chunk_rows(c), :] = jnp.where(chosen, 0.0, MASKED)
            return carry

        lax.fori_loop(0, n_chunks, bias_chunk, 0)

    def bias_without_ties():
        floor = jnp.maximum(thr, INT_MIN + 1)

        def bias_chunks(c0, count, carry):
            for c in (c0 + d for d in range(count)):
                bias_ref[chunk_rows(c), :] = jnp.where(key_ref[chunk_rows(c), :] >= floor, 0.0, MASKED)
            return carry

        _loop_in_groups(n_chunks, bias_chunks, 0)

    lax.cond(has_surplus, bias_with_ties, bias_without_ties)

    _split_heads(q_t_ref[0, 0], qh_ref)
    for g in range(N_HEADS // ATTN_GROUP):
        heads = list(enumerate(range(g * ATTN_GROUP, (g + 1) * ATTN_GROUP)))

        def masked_scores(c0, count, mx, heads=heads):
            mx = list(mx)
            for c in (c0 + d for d in range(count)):
                for gi, h in heads:
                    s = jnp.dot(k_ref[0, chunk_rows(c), _head_lanes(h)], qh_ref[h], preferred_element_type=F32)
                    s = s + bias_ref[chunk_rows(c), :]
                    s_ref[gi, chunk_rows(c), :] = s
                    mx[gi] = jnp.maximum(mx[gi], jnp.max(s, axis=0, keepdims=True))
            return tuple(mx)

        mx = _loop_in_groups(n_chunks, masked_scores, (jnp.full((1, tq), MASKED, F32),) * ATTN_GROUP)
        for gi, _ in heads:
            acc_refs[gi][...] = jnp.zeros(acc_refs[gi].shape, F32)

        def accumulate(c0, count, carry, heads=heads, mx=mx):
            for gi, h in heads:
                acc_refs[gi][...] += sum(
                    _weighted_values(s_ref[gi, chunk_rows(c0 + d), :], mx[gi], v_t_ref[0, c0 + d, _head_rows(h), :])
                    for d in range(count))
            return carry

        _loop_in_groups(n_chunks, accumulate, 0)
        for gi, h in heads:
            _store_head_output(o_t_ref, h, acc_refs[gi][...])
    o_ref[0] = o_t_ref[...].T.astype(BF16)


def _dsa_attention(iq_t, ik, iw_t, qb_t, kb, vb_t):
    b, n_blk, w, tq = qb_t.shape
    s = n_blk * tq
    topk = min(DSA_TOPK_MAX, s // 4)
    assert n_blk % SEARCH_GROUP == 0, "the packed search pads its chunk range to whole groups"
    qh, work = _attention_scratch(s, w, tq)
    tile_spec = lambda rows: pl.BlockSpec((1, 1, rows, tq), lambda bi, ti: (bi, ti, 0, 0))
    return pl.pallas_call(
        functools.partial(_dsa_kernel, topk=topk, index_bits=max(1, (s - 1).bit_length())),
        grid=(b, n_blk),
        in_specs=[
            tile_spec(IDX_HEADS * IDX_DIM),
            _resident_spec((1, s, IDX_DIM)),
            tile_spec(IDX_HEADS),
            tile_spec(w),
            _resident_spec((1, s, w)),
            _resident_spec((1, n_blk, w, tq)),
        ],
        out_specs=pl.BlockSpec((1, tq, w), lambda bi, ti: (bi, ti, 0)),
        out_shape=jax.ShapeDtypeStruct((b, s, w), BF16),
        scratch_shapes=[pltpu.VMEM((s, tq), I32)] + [pltpu.VMEM((s, tq), I16)] * 3
        + [pltpu.VMEM((s, tq), F32)] + qh + work,
        compiler_params=_cparams(("parallel", "arbitrary")),
        name="dsa_attention",
    )(iq_t, ik, iw_t, qb_t, kb, vb_t)


ROUTER_ROWS = 8 + N_EXPERTS
EXPERT_TOPK = 2


def _rms_norm(x, g):
    ms = jnp.mean(x * x, axis=-1, keepdims=True)
    return (x * lax.rsqrt(ms + RMS_EPS)) * g


def _first_index_of_max(vals, idx):
    top = jnp.max(vals, axis=0, keepdims=True)
    first = jnp.min(jnp.where(vals == top, idx, vals.shape[0]), axis=0, keepdims=True)
    return top, first


def _merge_router_kernel(oa_ref, ob_ref, ga_ref, gb_ref, x_ref, wa_ref, wb_ref, wo_ref, g_ref,
                         wr_ref, br_ref, x1_ref, h2_ref, ids_ref, cw_ref, rank_ref, cnt_ref):
    @pl.when(pl.program_id(0) == 0)
    def _():
        cnt_ref[...] = jnp.zeros_like(cnt_ref)

    a = jnp.dot(oa_ref[...], wa_ref[...], preferred_element_type=F32)
    b = jnp.dot(ob_ref[...], wb_ref[...], preferred_element_type=F32)
    merged = jax.nn.sigmoid(ga_ref[...]) * a + jax.nn.sigmoid(gb_ref[...]) * b
    x1 = x_ref[...] + jnp.dot(merged.astype(BF16), wo_ref[...], preferred_element_type=F32)
    x1_ref[...] = x1
    h2 = _rms_norm(x1, g_ref[...])
    h2_ref[...] = h2
    tm = h2.shape[0]

    logits = lax.dot_general(wr_ref[...], h2, NT_DIMS, preferred_element_type=F32,
                             precision=lax.Precision.HIGHEST) + br_ref[...]
    grp = logits[0:N_GROUPS, :]
    g_top, g_idx = _first_index_of_max(grp, lax.broadcasted_iota(I32, grp.shape, 0))
    p_grp = 1.0 / jnp.sum(jnp.exp(grp - g_top), axis=0, keepdims=True)
    in_grp = logits[8:8 + EXPERTS_PER_GROUP, :]
    for gi in range(1, N_GROUPS):
        rows = slice(8 + gi * EXPERTS_PER_GROUP, 8 + (gi + 1) * EXPERTS_PER_GROUP)
        in_grp = jnp.where(g_idx == gi, logits[rows, :], in_grp)
    e_iota = lax.broadcasted_iota(I32, in_grp.shape, 0)
    v0, i0 = _first_index_of_max(in_grp, e_iota)
    v1, i1 = _first_index_of_max(jnp.where(e_iota == i0, -jnp.inf, in_grp), e_iota)
    e1 = jnp.exp(v1 - v0)
    denom = 1.0 + e1
    ids = jnp.concatenate([g_idx * EXPERTS_PER_GROUP + i0, g_idx * EXPERTS_PER_GROUP + i1], axis=0)
    ids_ref[0] = ids
    cw_ref[0] = jnp.concatenate([p_grp * (1.0 / denom), p_grp * (e1 / denom)], axis=0)

    before = (lax.broadcasted_iota(I32, (tm, tm), 0) < lax.broadcasted_iota(I32, (tm, tm), 1)).astype(BF16)
    expert = lax.broadcasted_iota(I32, (N_EXPERTS, tm), 0)
    ranks = []
    for slot in range(EXPERT_TOPK):
        onehot = expert == ids[slot:slot + 1, :]
        seen = jnp.dot(onehot.astype(BF16), before, preferred_element_type=F32) + cnt_ref[...]
        ranks.append(jnp.sum(jnp.where(onehot, seen, 0.0), axis=0, keepdims=True))
        cnt_ref[...] += jnp.sum(onehot.astype(F32), axis=1, keepdims=True)
    rank_ref[0] = jnp.concatenate(ranks, axis=0).astype(I32)


def _merge_and_route(o_a, o_b, ga, gb, x, w_br_a, w_br_b, w_out, g_ffn, w_grp, b_grp, w_rt, b_rt, tm=512):
    t, d = x.shape
    n_tiles = t // tm
    wr = jnp.zeros((ROUTER_ROWS, d), F32).at[0:N_GROUPS].set(w_grp.T).at[8:].set(w_rt.T)
    br = jnp.zeros((ROUTER_ROWS, 1), F32).at[0:N_GROUPS, 0].set(b_grp).at[8:, 0].set(b_rt)
    row_spec = lambda w: pl.BlockSpec((tm, w), lambda i: (i, 0))
    slot_spec = pl.BlockSpec((1, EXPERT_TOPK, tm), lambda i: (i, 0, 0))
    slot_shape = lambda dt: jax.ShapeDtypeStruct((n_tiles, EXPERT_TOPK, tm), dt)
    return pl.pallas_call(
        _merge_router_kernel,
        grid=(n_tiles,),
        in_specs=[row_spec(MIXER_W), row_spec(MIXER_W), row_spec(d), row_spec(d), row_spec(d),
                  _const_spec((MIXER_W, d)), _const_spec((MIXER_W, d)), _const_spec((d, d)),
                  _const_spec((1, d)), _const_spec((ROUTER_ROWS, d)), _const_spec((ROUTER_ROWS, 1))],
        out_specs=[row_spec(d), row_spec(d), slot_spec, slot_spec, slot_spec,
                   pl.BlockSpec((N_EXPERTS, 1), lambda i: (0, 0))],
        out_shape=[jax.ShapeDtypeStruct((t, d), F32), jax.ShapeDtypeStruct((t, d), F32),
                   slot_shape(I32), slot_shape(F32), slot_shape(I32),
                   jax.ShapeDtypeStruct((N_EXPERTS, 1), F32)],
        compiler_params=_cparams(("arbitrary",)),
        name="merge_and_route",
    )(o_a, o_b, ga, gb, x, w_br_a.astype(BF16), w_br_b.astype(BF16), w_out.astype(BF16),
      g_ffn.reshape(1, d), wr, br)


EXPERT_ROWS = 512
DMA_ISSUE_UNROLL = 8


def _row(ref, r):
    return ref.at[pl.ds(r, 1), :]


def _tile_dest(dest_ref, tm, slot, r):
    return dest_ref[(pl.program_id(0) * EXPERT_TOPK + slot) * tm + r]


def _dispatch_kernel(dest_ref, h2_ref, xs_init_hbm, xs_hbm, sem):
    del xs_init_hbm
    tm = h2_ref.shape[0]

    def issue(r, carry):
        for slot in range(EXPERT_TOPK):
            pltpu.make_async_copy(_row(h2_ref, r), _row(xs_hbm, _tile_dest(dest_ref, tm, slot, r)), sem).start()
        return carry

    lax.fori_loop(0, tm, issue, 0, unroll=DMA_ISSUE_UNROLL)
    for slot in range(EXPERT_TOPK):
        pltpu.make_async_copy(h2_ref, xs_hbm.at[pl.ds(0, tm), :], sem).wait()


def _dispatch(dest, h2, n_rows, tm):
    t, d = h2.shape
    return pl.pallas_call(
        _dispatch_kernel,
        grid_spec=pltpu.PrefetchScalarGridSpec(
            num_scalar_prefetch=1,
            grid=(t // tm,),
            in_specs=[pl.BlockSpec((tm, d), lambda i, dest: (i, 0)), pl.BlockSpec(memory_space=pl.ANY)],
            out_specs=pl.BlockSpec(memory_space=pl.ANY),
            scratch_shapes=[pltpu.SemaphoreType.DMA(())],
        ),
        out_shape=jax.ShapeDtypeStruct((n_rows, d), F32),
        input_output_aliases={2: 0},
        compiler_params=_cparams(("arbitrary",)),
        name="moe_dispatch",
    )(dest, h2, jnp.zeros((n_rows, d), F32))


def _expert_kernel(blk_exp_ref, n_used_ref, xs_ref, wg_ref, wu_ref, wd_ref, ys_ref, wg16, wu16, wd16):
    i = pl.program_id(0)
    used = i < n_used_ref[0]

    @pl.when((i == 0) | (blk_exp_ref[i] != blk_exp_ref[jnp.maximum(i - 1, 0)]))
    def _():
        wg16[...] = wg_ref[0].astype(BF16)
        wu16[...] = wu_ref[0].astype(BF16)
        wd16[...] = wd_ref[0].astype(BF16)

    @pl.when(used)
    def _():
        xb = xs_ref[...].astype(BF16)
        gate = jnp.dot(xb, wg16[...], preferred_element_type=F32)
        up = jnp.dot(xb, wu16[...], preferred_element_type=F32)
        hid = (gate * jax.nn.sigmoid(gate)) * up
        ys_ref[...] = jnp.dot(hid.astype(BF16), wd16[...], preferred_element_type=F32)

    @pl.when(jnp.logical_not(used))
    def _():
        ys_ref[...] = jnp.zeros_like(ys_ref)


def _experts(blk_exp, n_used, xs, w_gate, w_up, w_down):
    n_rows, d = xs.shape
    n_blocks = n_rows // EXPERT_ROWS
    expert_spec = lambda rows, cols: pl.BlockSpec((1, rows, cols), lambda i, be, nu: (be[i], 0, 0))
    return pl.pallas_call(
        _expert_kernel,
        grid_spec=pltpu.PrefetchScalarGridSpec(
            num_scalar_prefetch=2,
            grid=(n_blocks,),
            in_specs=[pl.BlockSpec((EXPERT_ROWS, d), lambda i, be, nu: (i, 0)),
                      expert_spec(d, D_EXPERT), expert_spec(d, D_EXPERT), expert_spec(D_EXPERT, d)],
            out_specs=pl.BlockSpec((EXPERT_ROWS, d), lambda i, be, nu: (i, 0)),
            scratch_shapes=[pltpu.VMEM((d, D_EXPERT), BF16), pltpu.VMEM((d, D_EXPERT), BF16),
                            pltpu.VMEM((D_EXPERT, d), BF16)],
        ),
        out_shape=jax.ShapeDtypeStruct((n_rows, d), F32),
        compiler_params=_cparams(("arbitrary",)),
        name="moe_experts",
    )(blk_exp, n_used, xs, w_gate, w_up, w_down)


def _combine_ple_kernel(dest_ref, x1_ref, cw_ref, p_ref, ys_hbm, g_ple_ref, wpg_ref, wpp_ref, g_fin_ref,
                        out_ref, y_buf, sem):
    tm = x1_ref.shape[0]

    def issue(r, carry):
        for slot in range(EXPERT_TOPK):
            pltpu.make_async_copy(_row(ys_hbm, _tile_dest(dest_ref, tm, slot, r)), _row(y_buf.at[slot], r),
                                  sem).start()
        return carry

    lax.fori_loop(0, tm, issue, 0, unroll=DMA_ISSUE_UNROLL)
    for slot in range(EXPERT_TOPK):
        pltpu.make_async_copy(ys_hbm.at[pl.ds(0, tm), :], y_buf.at[slot], sem).wait()

    cw = cw_ref[...]
    x2 = x1_ref[...] + (y_buf[0] * cw[:, 0:1] + y_buf[1] * cw[:, 1:2])
    h3 = _rms_norm(x2, g_ple_ref[...]).astype(BF16)
    gate = jax.nn.sigmoid(jnp.dot(h3, wpg_ref[...], preferred_element_type=F32))
    proj = jnp.dot(p_ref[...].astype(BF16), wpp_ref[...], preferred_element_type=F32)
    out_ref[...] = _rms_norm(x2 + gate * proj, g_fin_ref[...])


def _combine_ple(dest, x1, cw_tok, p, ys, g_ple, w_ple_gate, w_ple_proj, g_final, tm):
    t, d = x1.shape
    row_spec = lambda w: pl.BlockSpec((tm, w), lambda i, dest: (i, 0))
    return pl.pallas_call(
        _combine_ple_kernel,
        grid_spec=pltpu.PrefetchScalarGridSpec(
            num_scalar_prefetch=1,
            grid=(t // tm,),
            in_specs=[row_spec(d), row_spec(EXPERT_TOPK), row_spec(p.shape[1]),
                      pl.BlockSpec(memory_space=pl.ANY),
                      _const_spec((1, d)), _const_spec((d, d)), _const_spec((p.shape[1], d)), _const_spec((1, d))],
            out_specs=row_spec(d),
            scratch_shapes=[pltpu.VMEM((EXPERT_TOPK, tm, d), F32), pltpu.SemaphoreType.DMA(())],
        ),
        out_shape=jax.ShapeDtypeStruct((t, d), F32),
        compiler_params=_cparams(("arbitrary",)),
        name="combine_ple",
    )(dest, x1, cw_tok, p, ys, g_ple.reshape(1, d), w_ple_gate.astype(BF16), w_ple_proj.astype(BF16),
      g_final.reshape(1, d))


def _layer(x, p, g_attn, w_in, w_br_a, w_br_b, w_out, g_ffn, w_grp, b_grp, w_rt, b_rt,
           w_gate, w_up, w_down, g_ple, w_ple_gate, w_ple_proj, g_final):
    b, s, d = x.shape
    t = b * s
    (qa_t, va_t, qb_t, vb_t, iq_t, iw_t, ka, kb, ik, ga, gb, km) = _in_projection(x, g_attn, w_in)
    o_a = _moba_attention(qa_t, ka, va_t, km.reshape(b, s // MOBA_BLOCK, MIXER_W))
    o_b = _dsa_attention(iq_t, ik, iw_t, qb_t, kb, vb_t)
    x1, h2, ids, cw, rank, counts = _merge_and_route(
        o_a.reshape(t, MIXER_W), o_b.reshape(t, MIXER_W), ga.reshape(t, d), gb.reshape(t, d),
        x.reshape(t, d), w_br_a, w_br_b, w_out, g_ffn, w_grp, b_grp, w_rt, b_rt)

    counts = counts[:, 0].astype(I32)
    padded = ((counts + EXPERT_ROWS - 1) // EXPERT_ROWS) * EXPERT_ROWS
    pend = jnp.cumsum(padded)
    pstart = pend - padded
    n_blocks = -(-(t * EXPERT_TOPK) // EXPERT_ROWS) + N_EXPERTS
    experts = jnp.arange(N_EXPERTS, dtype=I32)
    dest = jnp.sum(jnp.where(ids[..., None] == experts, pstart, 0), axis=-1) + rank
    block_row0 = jnp.arange(n_blocks, dtype=I32) * EXPERT_ROWS
    blk_exp = jnp.minimum(jnp.sum((pend[None, :] <= block_row0[:, None]).astype(I32), axis=1), N_EXPERTS - 1)
    n_used = (pend[-1:] // EXPERT_ROWS).astype(I32)

    route_tile = dest.shape[2]
    dest = dest.reshape(-1)
    xs = _dispatch(dest, h2, n_blocks * EXPERT_ROWS, route_tile)
    ys = _experts(blk_exp, n_used, xs, w_gate, w_up, w_down)
    cw_tok = jnp.swapaxes(cw, 1, 2).reshape(t, EXPERT_TOPK)
    out = _combine_ple(dest, x1, cw_tok, p.reshape(t, p.shape[-1]), ys, g_ple, w_ple_gate, w_ple_proj, g_final,
                       route_tile)
    return out.reshape(b, s, d)


def kernel(x, p, g_attn, w_in, w_br_a, w_br_b, w_out, g_ffn, w_grp, b_grp, w_rt, b_rt, w_gate, w_up, w_down, g_ple, w_ple_gate, w_ple_proj, g_final):
    depth = w_in.shape[0]
    assert depth == 1, "the final RMSNorm is fused into the last layer's kernel"
    i = 0
    return _layer(x, p[i], g_attn[i], w_in[i], w_br_a[i], w_br_b[i], w_out[i], g_ffn[i], w_grp[i], b_grp[i],
                  w_rt[i], b_rt[i], w_gate[i], w_up[i], w_down[i], g_ple[i], w_ple_gate[i], w_ple_proj[i], g_final)
```

```python
import functools

import jax
import jax.numpy as jnp
from jax import lax
from jax.experimental import pallas as pl
from jax.experimental.pallas import tpu as pltpu

F32 = jnp.float32
BF16 = jnp.bfloat16
I32 = jnp.int32

HEAD_DIM = 64
N_HEADS = 8
ROT_DIM = HEAD_DIM // 4
ROT_HALF = ROT_DIM // 2
ROPE_THETA = 500000.0
MOBA_BLOCK = 256
MOBA_TOPK = 3
IDX_HEADS = 8
IDX_DIM = 64
DSA_TOPK_MAX = 256
N_GROUPS = 4
EXPERTS_PER_GROUP = 8
N_EXPERTS = N_GROUPS * EXPERTS_PER_GROUP
D_EXPERT = 512
PLE_DIM = 256
RMS_EPS = 1e-6
MIXER_W = N_HEADS * HEAD_DIM

Q_TILE = 256
LANES = 128
SUBLANES = 8
VMEM_LIMIT = 56 * 1024 * 1024

NT_DIMS = (((1,), (1,)), ((), ()))


def _cparams(sem):
    return pltpu.CompilerParams(dimension_semantics=sem, vmem_limit_bytes=VMEM_LIMIT)


def _const_spec(shape):
    nd = len(shape)
    return pl.BlockSpec(shape, lambda *_: (0,) * nd, pipeline_mode=pl.Buffered(1))


def _rope_feature_major(z, cos_t, sin_t):
    tm = z.shape[1]
    z3 = z.reshape(N_HEADS, HEAD_DIM, tm)
    x1 = z3[:, 0:ROT_HALF, :]
    x2 = z3[:, ROT_HALF:ROT_DIM, :]
    o1 = x1 * cos_t - x2 * sin_t
    o2 = x2 * cos_t + x1 * sin_t
    return jnp.concatenate([o1, o2, z3[:, ROT_DIM:, :]], axis=1).reshape(N_HEADS * HEAD_DIM, tm)


def _rope_token_major(z, c_tab, s_lo, s_hi):
    up = pltpu.roll(z, LANES - ROT_HALF, 1)
    dn = pltpu.roll(z, ROT_HALF, 1)
    return z * c_tab + up * s_lo + dn * s_hi


def _store_blocked(ref, z):
    for c in range(z.shape[1] // Q_TILE):
        ref[0, c] = z[:, c * Q_TILE:(c + 1) * Q_TILE]


def _inproj_kernel(x_ref, g_ref, wf_ref, wiw_ref, wt_ref, wg_ref, cos_t_ref, sin_t_ref,
                   ctab_ref, slo_ref, shi_ref,
                   qa_t_ref, va_t_ref, qb_t_ref, vb_t_ref, iq_t_ref, iw_t_ref,
                   ka_ref, kb_ref, ik_ref, ga_ref, gb_ref, km_ref, *, w_scale):
    x = x_ref[0]
    ms = jnp.mean(x * x, axis=-1, keepdims=True)
    h = ((x * lax.rsqrt(ms + RMS_EPS)) * g_ref[...]).astype(BF16)
    cos_t = cos_t_ref[...]
    sin_t = sin_t_ref[...]
    q_scale = HEAD_DIM ** -0.5

    fm_outs = ((qa_t_ref, True, q_scale), (va_t_ref, False, 1.0), (qb_t_ref, True, q_scale),
               (vb_t_ref, False, 1.0), (iq_t_ref, True, IDX_DIM ** -0.5))
    for i, (ref, rope, scale) in enumerate(fm_outs):
        z = lax.dot_general(wf_ref[i * MIXER_W:(i + 1) * MIXER_W, :], h, NT_DIMS,
                            preferred_element_type=F32)
        if rope:
            z = _rope_feature_major(z, cos_t, sin_t)
        if scale != 1.0:
            z = z * scale
        _store_blocked(ref, z.astype(BF16))

    iw = lax.dot_general(wiw_ref[...], h, NT_DIMS, preferred_element_type=F32)
    _store_blocked(iw_t_ref, iw * w_scale)

    zt = jnp.dot(h, wt_ref[...], preferred_element_type=F32)
    ctab, slo, shi = ctab_ref[...], slo_ref[...], shi_ref[...]
    n_grp = zt.shape[1] // LANES
    roped = [_rope_token_major(zt[:, j * LANES:(j + 1) * LANES], ctab, slo, shi) for j in range(n_grp)]
    per_mixer = MIXER_W // LANES
    ka = jnp.concatenate(roped[:per_mixer], axis=1)
    kb = jnp.concatenate(roped[per_mixer:2 * per_mixer], axis=1)
    ka_ref[0] = ka.astype(BF16)
    kb_ref[0] = kb.astype(BF16)
    ik_ref[0] = roped[2 * per_mixer][:, :IDX_DIM].astype(BF16)
    tm = ka.shape[0]
    km_ref[0] = jnp.mean(ka.reshape(tm // MOBA_BLOCK, MOBA_BLOCK, MIXER_W), axis=1, keepdims=True)

    zg = jnp.dot(h, wg_ref[...], preferred_element_type=F32)
    d_model = zg.shape[1] // 2
    ga_ref[0] = zg[:, :d_model]
    gb_ref[0] = zg[:, d_model:]


def _rope_tables(seq):
    inv = 1.0 / (ROPE_THETA ** (jnp.arange(0, ROT_DIM, 2, dtype=F32) / ROT_DIM))
    ang = jnp.arange(seq, dtype=F32)[:, None] * inv[None, :]
    cos, sin = jnp.cos(ang), jnp.sin(ang)
    d = jnp.arange(LANES) % HEAD_DIM
    lo = d < ROT_HALF
    hi = (d >= ROT_HALF) & (d < ROT_DIM)
    f = d % ROT_HALF
    cos_l, sin_l = cos[:, f], sin[:, f]
    ctab = jnp.where(lo | hi, cos_l, 1.0)
    slo = jnp.where(lo, -sin_l, 0.0)
    shi = jnp.where(hi, sin_l, 0.0)
    return cos.T, sin.T, ctab, slo, shi


def _in_projection(x, g_attn, w_in, tm=512):
    b, s, d = x.shape
    splits = (MIXER_W,) * 6 + (IDX_HEADS * IDX_DIM, IDX_DIM, IDX_HEADS, d, d)
    offs = [0]
    for w in splits:
        offs.append(offs[-1] + w)
    wqa, wka, wva, wqb, wkb, wvb, wiq, wik, wiw, wga, wgb = (
        w_in[:, offs[i]:offs[i + 1]] for i in range(len(splits)))
    wf = jnp.concatenate([wqa, wva, wqb, wvb, wiq], axis=1).T.astype(BF16)
    wiw_t = wiw.T.astype(BF16)
    wt = jnp.concatenate([wka, wkb, wik, jnp.zeros((d, LANES - IDX_DIM), w_in.dtype)], axis=1).astype(BF16)
    wg = jnp.concatenate([wga, wgb], axis=1).astype(BF16)
    cos_t, sin_t, ctab, slo, shi = _rope_tables(s)
    n_blk = s // MOBA_BLOCK
    fm_shape = jax.ShapeDtypeStruct((b, s // Q_TILE, MIXER_W, Q_TILE), BF16)
    tok_shape = jax.ShapeDtypeStruct((b, s, MIXER_W), BF16)
    fm_spec = pl.BlockSpec((1, tm // Q_TILE, MIXER_W, Q_TILE), lambda bi, ti: (bi, ti, 0, 0))
    tok_spec = pl.BlockSpec((1, tm, MIXER_W), lambda bi, ti: (bi, ti, 0))
    gate_spec = pl.BlockSpec((1, tm, d), lambda bi, ti: (bi, ti, 0))
    outs = pl.pallas_call(
        functools.partial(_inproj_kernel, w_scale=IDX_HEADS ** -0.5),
        grid=(b, s // tm),
        in_specs=[
            pl.BlockSpec((1, tm, d), lambda bi, ti: (bi, ti, 0)),
            _const_spec((1, d)),
            _const_spec(wf.shape), _const_spec(wiw_t.shape), _const_spec(wt.shape), _const_spec(wg.shape),
            pl.BlockSpec((ROT_HALF, tm), lambda bi, ti: (0, ti)),
            pl.BlockSpec((ROT_HALF, tm), lambda bi, ti: (0, ti)),
            pl.BlockSpec((tm, LANES), lambda bi, ti: (ti, 0)),
            pl.BlockSpec((tm, LANES), lambda bi, ti: (ti, 0)),
            pl.BlockSpec((tm, LANES), lambda bi, ti: (ti, 0)),
        ],
        out_specs=[
            fm_spec, fm_spec, fm_spec, fm_spec, fm_spec,
            pl.BlockSpec((1, tm // Q_TILE, IDX_HEADS, Q_TILE), lambda bi, ti: (bi, ti, 0, 0)),
            tok_spec, tok_spec,
            pl.BlockSpec((1, tm, IDX_DIM), lambda bi, ti: (bi, ti, 0)),
            gate_spec, gate_spec,
            pl.BlockSpec((1, tm // MOBA_BLOCK, 1, MIXER_W), lambda bi, ti: (bi, ti, 0, 0)),
        ],
        out_shape=[
            fm_shape, fm_shape, fm_shape, fm_shape, fm_shape,
            jax.ShapeDtypeStruct((b, s // Q_TILE, IDX_HEADS, Q_TILE), F32),
            tok_shape, tok_shape,
            jax.ShapeDtypeStruct((b, s, IDX_DIM), BF16),
            jax.ShapeDtypeStruct((b, s, d), F32), jax.ShapeDtypeStruct((b, s, d), F32),
            jax.ShapeDtypeStruct((b, n_blk, 1, MIXER_W), F32),
        ],
        compiler_params=_cparams(("parallel", "parallel")),
        name="in_projection",
    )(x, g_attn.reshape(1, d), wf, wiw_t, wt, wg, cos_t, sin_t, ctab, slo, shi)
    return outs


MASKED = -1e30


HEADS_PER_GROUP = LANES // HEAD_DIM


def _one_head_of_pair(q_pair, hh):
    row = lax.broadcasted_iota(I32, q_pair.shape, 0)
    return jnp.where((row // HEAD_DIM) == hh, q_pair, jnp.zeros_like(q_pair))


def _head_lanes(h):
    g = h // HEADS_PER_GROUP
    return slice(g * LANES, (g + 1) * LANES)


def _head_rows(h):
    return slice(h * HEAD_DIM, (h + 1) * HEAD_DIM)


ATTN_GROUP = 2
ONES_ROWS = 16
ACC_ROWS = HEAD_DIM + ONES_ROWS


def _split_heads(q_all, qh_ref):
    for h in range(N_HEADS):
        qh_ref[h] = _one_head_of_pair(q_all[_head_lanes(h), :], h % HEADS_PER_GROUP)


def _weighted_values(s, m, v_t):
    p = jnp.exp(s - m).astype(BF16)
    lhs = jnp.concatenate([v_t, jnp.ones((ONES_ROWS, v_t.shape[1]), BF16)], axis=0)
    return jnp.dot(lhs, p, preferred_element_type=F32)


def _loop_in_groups(n, body, init, width=4):
    carry = lax.fori_loop(0, n // width, lambda k, c: body(width * k, width, c), init)
    done = (n // width) * width
    part = width // 2
    while part >= 1:
        has_part = ((n - done) // part) % 2 == 1
        carry = lax.cond(has_part, lambda c, done=done, part=part: body(done, part, c), lambda c: c, carry)
        done = done + jnp.where(has_part, part, 0)
        part //= 2
    return carry


def _store_head_output(o_t_ref, h, acc):
    o_t_ref[_head_rows(h), :] = acc[:HEAD_DIM, :] / acc[HEAD_DIM:HEAD_DIM + 1, :]


def _moba_kernel(q_t_ref, k_ref, v_t_ref, km_ref, o_ref, qh_ref, bias_ref, s_even_ref, s_odd_ref, o_t_ref,
                 *acc_refs):
    s_refs = (s_even_ref, s_odd_ref)
    i = pl.program_id(1)
    tq = q_t_ref.shape[3]
    n_blk = km_ref.shape[1]
    _split_heads(q_t_ref[0, 0], qh_ref)

    blk = lax.broadcasted_iota(I32, (n_blk, tq), 0)
    for h in range(N_HEADS):
        gate = jnp.dot(km_ref[0, :, _head_lanes(h)], qh_ref[h].astype(F32), preferred_element_type=F32)
        gate = jnp.where(blk < i, gate, -jnp.inf)
        keep = blk == i
        for _ in range(MOBA_TOPK):
            _, first = _first_index_of_max(gate, blk)
            taken = (blk == first) & (blk < i)
            keep = keep | taken
            gate = jnp.where(taken, -jnp.inf, gate)
        bias_ref[h] = jnp.where(keep, 0.0, MASKED)

    def key_rows(j):
        return pl.ds(pl.multiple_of(j * tq, tq), tq)

    def block_bias(h, j):
        return bias_ref[h, pl.ds(j, 1), :]

    causal = lax.broadcasted_iota(I32, (tq, tq), 0) <= lax.broadcasted_iota(I32, (tq, tq), 1)

    def scores(h, j):
        return jnp.dot(k_ref[0, key_rows(j), _head_lanes(h)], qh_ref[h], preferred_element_type=F32)

    def own_block_scores(g):
        own_max = []
        for gi, h in _group_heads(g):
            s = jnp.where(causal, scores(h, i), MASKED)
            s_refs[g % 2][gi, key_rows(i), :] = s
            own_max.append(jnp.max(s, axis=0, keepdims=True))
        return tuple(own_max)

    def past_scores(g, j, mx):
        for gi, h in _group_heads(g):
            s = scores(h, j)
            s_refs[g % 2][gi, key_rows(j), :] = s
            mx[gi] = jnp.maximum(mx[gi], jnp.max(s, axis=0, keepdims=True) + block_bias(h, j))

    def weighted(g, gi, h, j, mx):
        return _weighted_values(s_refs[g % 2][gi, key_rows(j), :], mx[gi] - block_bias(h, j),
                                v_t_ref[0, j, _head_rows(h), :])

    def finish(g, mx):
        for gi, h in _group_heads(g):
            _store_head_output(o_t_ref, h, acc_refs[gi][...] + weighted(g, gi, h, i, mx))

    _attention_pipeline(i, own_block_scores, past_scores, weighted, finish, acc_refs)
    o_ref[0] = o_t_ref[...].T.astype(BF16)


def _group_heads(g):
    return list(enumerate(range(g * ATTN_GROUP, (g + 1) * ATTN_GROUP)))


def _attention_pipeline(n, first_max, score_step, weighted, finish, acc_refs):
    n_groups = N_HEADS // ATTN_GROUP
    prev_mx = None
    for g in range(n_groups + 1):
        scoring, weighting = g < n_groups, g > 0
        if weighting:
            for acc_ref in acc_refs:
                acc_ref[...] = jnp.zeros(acc_ref.shape, F32)

        def body(j0, count, mx, g=g, scoring=scoring, weighting=weighting, prev_mx=prev_mx):
            mx = list(mx)
            partial = [0.0] * ATTN_GROUP
            for d in range(count):
                if scoring:
                    score_step(g, j0 + d, mx)
                if weighting:
                    for gi, h in _group_heads(g - 1):
                        partial[gi] = partial[gi] + weighted(g - 1, gi, h, j0 + d, prev_mx)
            if weighting:
                for gi, _ in _group_heads(g - 1):
                    acc_refs[gi][...] += partial[gi]
            return tuple(mx)

        mx = _loop_in_groups(n, body, first_max(g) if scoring else ())
        if weighting:
            finish(g - 1, prev_mx)
        prev_mx = mx


def _attention_scratch(s, w, tq):
    return ([pltpu.VMEM((N_HEADS, LANES, tq), BF16)],
            [pltpu.VMEM((ATTN_GROUP, s, tq), F32)] * 2 + [pltpu.VMEM((w, tq), F32)]
            + [pltpu.VMEM((ACC_ROWS, tq), F32)] * ATTN_GROUP)


def _resident_spec(shape):
    nd = len(shape)
    return pl.BlockSpec(shape, lambda bi, i: (bi,) + (0,) * (nd - 1), pipeline_mode=pl.Buffered(1))


def _moba_attention(qa_t, ka, va_t, km):
    b, n_blk, w, tq = qa_t.shape
    s = n_blk * tq
    qh, work = _attention_scratch(s, w, tq)
    return pl.pallas_call(
        _moba_kernel,
        grid=(b, n_blk),
        in_specs=[
            pl.BlockSpec((1, 1, w, tq), lambda bi, i: (bi, i, 0, 0)),
            _resident_spec((1, s, w)),
            _resident_spec((1, n_blk, w, tq)),
            _resident_spec((1, n_blk, w)),
        ],
        out_specs=pl.BlockSpec((1, tq, w), lambda bi, i: (bi, i, 0)),
        out_shape=jax.ShapeDtypeStruct((b, s, w), BF16),
        scratch_shapes=qh + [pltpu.VMEM((N_HEADS, n_blk, tq), F32)] + work,
        compiler_params=_cparams(("parallel", "arbitrary")),
        name="moba_attention",
    )(qa_t, ka, va_t, km)


INT_MIN = -2 ** 31


def _sortable_key(s):
    bits = lax.bitcast_convert_type(s, I32)
    return bits ^ ((bits >> 31) & 0x7FFFFFFF)


def _column_count(mask):
    tk, tq = mask.shape
    return jnp.sum(mask.astype(I32).reshape(tk // SUBLANES, SUBLANES, tq), axis=0)


I16 = jnp.int16
HALF_BITS = 16
I16_MIN = -2 ** (HALF_BITS - 1)
PACKED_ROWS = 2 * SUBLANES
SEARCH_GROUP = 4


def _packed_column_count(mask):
    ones = mask.astype(I16)
    parts = [ones[r:r + PACKED_ROWS, :] for r in range(0, ones.shape[0], PACKED_ROWS)]
    while len(parts) > 1:
        parts = [a + b for a, b in zip(parts[::2], parts[1::2])] + parts[len(parts) - len(parts) % 2:]
    return parts[0]


def _packed_total(count16):
    return jnp.sum(count16.astype(I32), axis=0, keepdims=True)


def _dsa_kernel(iq_t_ref, ik_ref, iw_t_ref, q_t_ref, k_ref, v_t_ref, o_ref,
                key_ref, hi_ref, lo_ref, lo2_ref, bias_ref, qh_ref, s_even_ref, s_odd_ref, o_t_ref, *acc_refs,
                topk, index_bits):
    s_refs = (s_even_ref, s_odd_ref)
    t = pl.program_id(1)
    tq = q_t_ref.shape[3]
    tk = tq
    n_chunks = t + 1
    qpos = t * tq + lax.broadcasted_iota(I32, (1, tq), 1)
    krow = lax.broadcasted_iota(I32, (tk, tq), 0)

    def chunk_rows(c):
        return pl.ds(pl.multiple_of(c * tk, tk), tk)

    def total(count8):
        return jnp.sum(count8, axis=0, keepdims=True)

    def count_over_chunks(pred):
        def body(c, cnt):
            return cnt + _column_count(pred(key_ref[chunk_rows(c), :], c * tk + krow))
        return total(lax.fori_loop(0, n_chunks, body, jnp.zeros((SUBLANES, tq), I32)))

    iq = iq_t_ref[0, 0]
    iw = iw_t_ref[0, 0]

    def score_chunks(c0, count, carry):
        for c in (c0 + d for d in range(count)):
            ik_c = ik_ref[0, chunk_rows(c), :]
            score = jnp.zeros((tk, tq), F32)
            for h in range(IDX_HEADS):
                rel = jnp.dot(ik_c, iq[h * IDX_DIM:(h + 1) * IDX_DIM, :], preferred_element_type=F32)
                score = score + jnp.maximum(rel, 0.0) * iw[h:h + 1, :]
            key = jnp.where(c * tk + krow <= qpos, _sortable_key(score), INT_MIN)
            key_ref[chunk_rows(c), :] = key
            hi_ref[chunk_rows(c), :] = (key >> HALF_BITS).astype(I16)
            lo_ref[chunk_rows(c), :] = key.astype(I16) ^ jnp.int16(I16_MIN)
        return carry

    _loop_in_groups(n_chunks, score_chunks, 0)

    n_groups = (n_chunks + SEARCH_GROUP - 1) // SEARCH_GROUP
    for d in range(SEARCH_GROUP - 1):
        @pl.when(n_chunks + d < n_groups * SEARCH_GROUP)
        def _(d=d):
            lowest = jnp.full((tk, tq), I16_MIN, I16)
            hi_ref[chunk_rows(n_chunks + d), :] = lowest
            lo_ref[chunk_rows(n_chunks + d), :] = lowest

    def packed_counts(n_counts, chunk_masks):
        def body(g, cnts):
            cnts = list(cnts)
            for c in (g * SEARCH_GROUP + d for d in range(SEARCH_GROUP)):
                for i, mask in enumerate(chunk_masks(c)):
                    cnts[i] = cnts[i] + _packed_column_count(mask)
            return tuple(cnts)
        zero = jnp.zeros((PACKED_ROWS, tq), I16)
        return [_packed_total(c) for c in lax.fori_loop(0, n_groups, body, (zero,) * n_counts)]

    def kth_largest_half(half_ref, kth):
        def half_bit(b, prefix):
            cand = prefix | jnp.left_shift(jnp.int32(1), HALF_BITS - 1 - b)
            cand_half = (cand + I16_MIN).astype(I16)
            n_ge, = packed_counts(1, lambda c: [half_ref[chunk_rows(c), :] >= cand_half])
            return jnp.where(n_ge >= kth, cand, prefix)

        return lax.fori_loop(0, HALF_BITS, half_bit, jnp.zeros((1, tq), I32)) + I16_MIN

    thr_hi = kth_largest_half(hi_ref, topk)
    thr_hi_half = thr_hi.astype(I16)

    def above_hi(c):
        hi = hi_ref[chunk_rows(c), :]
        lo2_ref[chunk_rows(c), :] = jnp.where(hi == thr_hi_half, lo_ref[chunk_rows(c), :], jnp.int16(I16_MIN))
        return [hi > thr_hi_half]

    n_gt_hi, = packed_counts(1, above_hi)
    thr_lo = kth_largest_half(lo2_ref, topk - n_gt_hi)
    thr_lo_half = thr_lo.astype(I16)
    thr = jnp.left_shift(thr_hi, HALF_BITS) | (thr_lo - I16_MIN)

    n_gt_lo, n_eq = packed_counts(2, lambda c: [
        lo2_ref[chunk_rows(c), :] > thr_lo_half,
        (hi_ref[chunk_rows(c), :] == thr_hi_half) & (lo_ref[chunk_rows(c), :] == thr_lo_half)])
    need = topk - (n_gt_hi + n_gt_lo)
    has_surplus = jnp.any((n_eq > need) & (thr != INT_MIN))

    def last_tie_index():
        def index_bit(b, prefix):
            cand = prefix | jnp.left_shift(jnp.int32(1), index_bits - 1 - b)
            n_before = count_over_chunks(lambda keys, kpos: (keys == thr) & (kpos < cand))
            return jnp.where(n_before < need, cand, prefix)
        return lax.fori_loop(0, index_bits, index_bit, jnp.zeros((1, tq), I32))

    def bias_with_ties():
        tie_end = last_tie_index()

        def bias_chunk(c, carry):
            keys = key_ref[chunk_rows(c), :]
            kpos = c * tk + krow
            chosen = ((keys > thr) | ((keys == thr) & (kpos <= tie_end))) & (kpos <= qpos)
            bias_ref[chunk_rows(c), :] = jnp.where(chosen, 0.0, MASKED)
            return carry

        lax.fori_loop(0, n_chunks, bias_chunk, 0)

    def bias_without_ties():
        floor = jnp.maximum(thr, INT_MIN + 1)

        def bias_chunks(c0, count, carry):
            for c in (c0 + d for d in range(count)):
                bias_ref[chunk_rows(c), :] = jnp.where(key_ref[chunk_rows(c), :] >= floor, 0.0, MASKED)
            return carry

        _loop_in_groups(n_chunks, bias_chunks, 0)

    lax.cond(has_surplus, bias_with_ties, bias_without_ties)

    _split_heads(q_t_ref[0, 0], qh_ref)

    def masked_scores(g, c, mx):
        for gi, h in _group_heads(g):
            s = jnp.dot(k_ref[0, chunk_rows(c), _head_lanes(h)], qh_ref[h], preferred_element_type=F32)
            s = s + bias_ref[chunk_rows(c), :]
            s_refs[g % 2][gi, chunk_rows(c), :] = s
            mx[gi] = jnp.maximum(mx[gi], jnp.max(s, axis=0, keepdims=True))

    def weighted(g, gi, h, c, mx):
        return _weighted_values(s_refs[g % 2][gi, chunk_rows(c), :], mx[gi], v_t_ref[0, c, _head_rows(h), :])

    def finish(g, mx):
        for gi, h in _group_heads(g):
            _store_head_output(o_t_ref, h, acc_refs[gi][...])

    _attention_pipeline(n_chunks, lambda g: (jnp.full((1, tq), MASKED, F32),) * ATTN_GROUP,
                        masked_scores, weighted, finish, acc_refs)
    o_ref[0] = o_t_ref[...].T.astype(BF16)


def _dsa_attention(iq_t, ik, iw_t, qb_t, kb, vb_t):
    b, n_blk, w, tq = qb_t.shape
    s = n_blk * tq
    topk = min(DSA_TOPK_MAX, s // 4)
    assert n_blk % SEARCH_GROUP == 0, "the packed search pads its chunk range to whole groups"
    qh, work = _attention_scratch(s, w, tq)
    tile_spec = lambda rows: pl.BlockSpec((1, 1, rows, tq), lambda bi, ti: (bi, ti, 0, 0))
    return pl.pallas_call(
        functools.partial(_dsa_kernel, topk=topk, index_bits=max(1, (s - 1).bit_length())),
        grid=(b, n_blk),
        in_specs=[
            tile_spec(IDX_HEADS * IDX_DIM),
            _resident_spec((1, s, IDX_DIM)),
            tile_spec(IDX_HEADS),
            tile_spec(w),
            _resident_spec((1, s, w)),
            _resident_spec((1, n_blk, w, tq)),
        ],
        out_specs=pl.BlockSpec((1, tq, w), lambda bi, ti: (bi, ti, 0)),
        out_shape=jax.ShapeDtypeStruct((b, s, w), BF16),
        scratch_shapes=[pltpu.VMEM((s, tq), I32)] + [pltpu.VMEM((s, tq), I16)] * 3
        + [pltpu.VMEM((s, tq), F32)] + qh + work,
        compiler_params=_cparams(("parallel", "arbitrary")),
        name="dsa_attention",
    )(iq_t, ik, iw_t, qb_t, kb, vb_t)


ROUTER_ROWS = 8 + N_EXPERTS
EXPERT_TOPK = 2


def _rms_norm(x, g):
    ms = jnp.mean(x * x, axis=-1, keepdims=True)
    return (x * lax.rsqrt(ms + RMS_EPS)) * g


def _first_index_of_max(vals, idx):
    top = jnp.max(vals, axis=0, keepdims=True)
    first = jnp.min(jnp.where(vals == top, idx, vals.shape[0]), axis=0, keepdims=True)
    return top, first


def _merge_router_kernel(oa_ref, ob_ref, ga_ref, gb_ref, x_ref, wa_ref, wb_ref, wo_ref, g_ref,
                         wr_ref, br_ref, x1_ref, h2_ref, ids_ref, cw_ref, rank_ref, cnt_ref):
    @pl.when(pl.program_id(0) == 0)
    def _():
        cnt_ref[...] = jnp.zeros_like(cnt_ref)

    a = jnp.dot(oa_ref[...], wa_ref[...], preferred_element_type=F32)
    b = jnp.dot(ob_ref[...], wb_ref[...], preferred_element_type=F32)
    merged = jax.nn.sigmoid(ga_ref[...]) * a + jax.nn.sigmoid(gb_ref[...]) * b
    x1 = x_ref[...] + jnp.dot(merged.astype(BF16), wo_ref[...], preferred_element_type=F32)
    x1_ref[...] = x1
    h2 = _rms_norm(x1, g_ref[...])
    h2_ref[...] = h2
    tm = h2.shape[0]

    logits = lax.dot_general(wr_ref[...], h2, NT_DIMS, preferred_element_type=F32,
                             precision=lax.Precision.HIGHEST) + br_ref[...]
    grp = logits[0:N_GROUPS, :]
    g_top, g_idx = _first_index_of_max(grp, lax.broadcasted_iota(I32, grp.shape, 0))
    p_grp = 1.0 / jnp.sum(jnp.exp(grp - g_top), axis=0, keepdims=True)
    in_grp = logits[8:8 + EXPERTS_PER_GROUP, :]
    for gi in range(1, N_GROUPS):
        rows = slice(8 + gi * EXPERTS_PER_GROUP, 8 + (gi + 1) * EXPERTS_PER_GROUP)
        in_grp = jnp.where(g_idx == gi, logits[rows, :], in_grp)
    e_iota = lax.broadcasted_iota(I32, in_grp.shape, 0)
    v0, i0 = _first_index_of_max(in_grp, e_iota)
    v1, i1 = _first_index_of_max(jnp.where(e_iota == i0, -jnp.inf, in_grp), e_iota)
    e1 = jnp.exp(v1 - v0)
    denom = 1.0 + e1
    ids = jnp.concatenate([g_idx * EXPERTS_PER_GROUP + i0, g_idx * EXPERTS_PER_GROUP + i1], axis=0)
    ids_ref[0] = ids
    cw_ref[0] = jnp.concatenate([p_grp * (1.0 / denom), p_grp * (e1 / denom)], axis=0)

    before = (lax.broadcasted_iota(I32, (tm, tm), 0) < lax.broadcasted_iota(I32, (tm, tm), 1)).astype(BF16)
    expert = lax.broadcasted_iota(I32, (N_EXPERTS, tm), 0)
    ranks = []
    for slot in range(EXPERT_TOPK):
        onehot = expert == ids[slot:slot + 1, :]
        seen = jnp.dot(onehot.astype(BF16), before, preferred_element_type=F32) + cnt_ref[...]
        ranks.append(jnp.sum(jnp.where(onehot, seen, 0.0), axis=0, keepdims=True))
        cnt_ref[...] += jnp.sum(onehot.astype(F32), axis=1, keepdims=True)
    rank_ref[0] = jnp.concatenate(ranks, axis=0).astype(I32)


def _merge_and_route(o_a, o_b, ga, gb, x, w_br_a, w_br_b, w_out, g_ffn, w_grp, b_grp, w_rt, b_rt, tm=512):
    t, d = x.shape
    n_tiles = t // tm
    wr = jnp.zeros((ROUTER_ROWS, d), F32).at[0:N_GROUPS].set(w_grp.T).at[8:].set(w_rt.T)
    br = jnp.zeros((ROUTER_ROWS, 1), F32).at[0:N_GROUPS, 0].set(b_grp).at[8:, 0].set(b_rt)
    row_spec = lambda w: pl.BlockSpec((tm, w), lambda i: (i, 0))
    slot_spec = pl.BlockSpec((1, EXPERT_TOPK, tm), lambda i: (i, 0, 0))
    slot_shape = lambda dt: jax.ShapeDtypeStruct((n_tiles, EXPERT_TOPK, tm), dt)
    return pl.pallas_call(
        _merge_router_kernel,
        grid=(n_tiles,),
        in_specs=[row_spec(MIXER_W), row_spec(MIXER_W), row_spec(d), row_spec(d), row_spec(d),
                  _const_spec((MIXER_W, d)), _const_spec((MIXER_W, d)), _const_spec((d, d)),
                  _const_spec((1, d)), _const_spec((ROUTER_ROWS, d)), _const_spec((ROUTER_ROWS, 1))],
        out_specs=[row_spec(d), row_spec(d), slot_spec, slot_spec, slot_spec,
                   pl.BlockSpec((N_EXPERTS, 1), lambda i: (0, 0))],
        out_shape=[jax.ShapeDtypeStruct((t, d), F32), jax.ShapeDtypeStruct((t, d), F32),
                   slot_shape(I32), slot_shape(F32), slot_shape(I32),
                   jax.ShapeDtypeStruct((N_EXPERTS, 1), F32)],
        compiler_params=_cparams(("arbitrary",)),
        name="merge_and_route",
    )(o_a, o_b, ga, gb, x, w_br_a.astype(BF16), w_br_b.astype(BF16), w_out.astype(BF16),
      g_ffn.reshape(1, d), wr, br)


EXPERT_ROWS = 512
DMA_ISSUE_UNROLL = 8


def _row(ref, r):
    return ref.at[pl.ds(r, 1), :]


def _tile_dest(dest_ref, tm, slot, r):
    return dest_ref[(pl.program_id(0) * EXPERT_TOPK + slot) * tm + r]


def _dispatch_kernel(dest_ref, h2_ref, xs_init_hbm, xs_hbm, sem):
    del xs_init_hbm
    tm = h2_ref.shape[0]

    for r in range(tm):
        for slot in range(EXPERT_TOPK):
            pltpu.make_async_copy(_row(h2_ref, r), _row(xs_hbm, _tile_dest(dest_ref, tm, slot, r)), sem).start()
    for slot in range(EXPERT_TOPK):
        pltpu.make_async_copy(h2_ref, xs_hbm.at[pl.ds(0, tm), :], sem).wait()


def _dispatch(dest, h2, n_rows, tm):
    t, d = h2.shape
    return pl.pallas_call(
        _dispatch_kernel,
        grid_spec=pltpu.PrefetchScalarGridSpec(
            num_scalar_prefetch=1,
            grid=(t // tm,),
            in_specs=[pl.BlockSpec((tm, d), lambda i, dest: (i, 0)), pl.BlockSpec(memory_space=pl.ANY)],
            out_specs=pl.BlockSpec(memory_space=pl.ANY),
            scratch_shapes=[pltpu.SemaphoreType.DMA(())],
        ),
        out_shape=jax.ShapeDtypeStruct((n_rows, d), F32),
        input_output_aliases={2: 0},
        compiler_params=_cparams(("arbitrary",)),
        name="moe_dispatch",
    )(dest, h2, jnp.zeros((n_rows, d), F32))


def _expert_kernel(blk_exp_ref, n_used_ref, xs_ref, wg_ref, wu_ref, wd_ref, ys_ref, wg16, wu16, wd16):
    i = pl.program_id(0)
    used = i < n_used_ref[0]

    @pl.when((i == 0) | (blk_exp_ref[i] != blk_exp_ref[jnp.maximum(i - 1, 0)]))
    def _():
        wg16[...] = wg_ref[0].astype(BF16)
        wu16[...] = wu_ref[0].astype(BF16)
        wd16[...] = wd_ref[0].astype(BF16)

    @pl.when(used)
    def _():
        xb = xs_ref[...].astype(BF16)
        gate = jnp.dot(xb, wg16[...], preferred_element_type=F32)
        up = jnp.dot(xb, wu16[...], preferred_element_type=F32)
        hid = (gate * jax.nn.sigmoid(gate)) * up
        ys_ref[...] = jnp.dot(hid.astype(BF16), wd16[...], preferred_element_type=F32)

    @pl.when(jnp.logical_not(used))
    def _():
        ys_ref[...] = jnp.zeros_like(ys_ref)


def _experts(blk_exp, n_used, xs, w_gate, w_up, w_down):
    n_rows, d = xs.shape
    n_blocks = n_rows // EXPERT_ROWS
    expert_spec = lambda rows, cols: pl.BlockSpec((1, rows, cols), lambda i, be, nu: (be[i], 0, 0))
    return pl.pallas_call(
        _expert_kernel,
        grid_spec=pltpu.PrefetchScalarGridSpec(
            num_scalar_prefetch=2,
            grid=(n_blocks,),
            in_specs=[pl.BlockSpec((EXPERT_ROWS, d), lambda i, be, nu: (i, 0)),
                      expert_spec(d, D_EXPERT), expert_spec(d, D_EXPERT), expert_spec(D_EXPERT, d)],
            out_specs=pl.BlockSpec((EXPERT_ROWS, d), lambda i, be, nu: (i, 0)),
            scratch_shapes=[pltpu.VMEM((d, D_EXPERT), BF16), pltpu.VMEM((d, D_EXPERT), BF16),
                            pltpu.VMEM((D_EXPERT, d), BF16)],
        ),
        out_shape=jax.ShapeDtypeStruct((n_rows, d), F32),
        compiler_params=_cparams(("arbitrary",)),
        name="moe_experts",
    )(blk_exp, n_used, xs, w_gate, w_up, w_down)


def _combine_ple_kernel(dest_ref, x1_ref, cw_ref, p_ref, ys_hbm, g_ple_ref, wpg_ref, wpp_ref, g_fin_ref,
                        out_ref, y_even, y_odd, sem):
    tm = y_even.shape[1]
    i = pl.program_id(0)
    last_tile = 2 * pl.num_programs(0) - 1
    bufs = ((y_even, 0), (y_odd, 1))

    def start_row(tile, buf, sem_slot, r):
        for slot in range(EXPERT_TOPK):
            src = _row(ys_hbm, dest_ref[(tile * EXPERT_TOPK + slot) * tm + r])
            pltpu.make_async_copy(src, _row(buf.at[slot], r), sem.at[sem_slot]).start()

    def wait_tile(buf, sem_slot):
        for slot in range(EXPERT_TOPK):
            pltpu.make_async_copy(ys_hbm.at[pl.ds(0, tm), :], buf.at[slot], sem.at[sem_slot]).wait()

    def combine(half, buf):
        rows = slice(half * tm, (half + 1) * tm)
        cw = cw_ref[rows, :]
        x2 = x1_ref[rows, :] + (buf[0] * cw[:, 0:1] + buf[1] * cw[:, 1:2])
        h3 = _rms_norm(x2, g_ple_ref[...]).astype(BF16)
        gate = jax.nn.sigmoid(jnp.dot(h3, wpg_ref[...], preferred_element_type=F32))
        proj = jnp.dot(p_ref[rows, :].astype(BF16), wpp_ref[...], preferred_element_type=F32)
        out_ref[rows, :] = _rms_norm(x2 + gate * proj, g_fin_ref[...])

    @pl.when(i == 0)
    def _():
        def issue(r, carry):
            start_row(0, *bufs[0], r)
            return carry
        lax.fori_loop(0, tm, issue, 0, unroll=DMA_ISSUE_UNROLL)

    for half in range(2):
        buf, sem_slot = bufs[half]
        wait_tile(buf, sem_slot)
        combine(half, buf)
        next_tile = jnp.minimum(2 * i + half + 1, last_tile)
        for r in range(tm):
            start_row(next_tile, *bufs[1 - half], r)

    @pl.when(i == pl.num_programs(0) - 1)
    def _():
        wait_tile(*bufs[0])


def _combine_ple(dest, x1, cw_tok, p, ys, g_ple, w_ple_gate, w_ple_proj, g_final, tm):
    t, d = x1.shape
    assert t % (2 * tm) == 0
    row_spec = lambda w: pl.BlockSpec((2 * tm, w), lambda i, dest: (i, 0))
    return pl.pallas_call(
        _combine_ple_kernel,
        grid_spec=pltpu.PrefetchScalarGridSpec(
            num_scalar_prefetch=1,
            grid=(t // (2 * tm),),
            in_specs=[row_spec(d), row_spec(EXPERT_TOPK), row_spec(p.shape[1]),
                      pl.BlockSpec(memory_space=pl.ANY),
                      _const_spec((1, d)), _const_spec((d, d)), _const_spec((p.shape[1], d)), _const_spec((1, d))],
            out_specs=row_spec(d),
            scratch_shapes=[pltpu.VMEM((EXPERT_TOPK, tm, d), F32), pltpu.VMEM((EXPERT_TOPK, tm, d), F32),
                            pltpu.SemaphoreType.DMA((2,))],
        ),
        out_shape=jax.ShapeDtypeStruct((t, d), F32),
        compiler_params=_cparams(("arbitrary",)),
        name="combine_ple",
    )(dest, x1, cw_tok, p, ys, g_ple.reshape(1, d), w_ple_gate.astype(BF16), w_ple_proj.astype(BF16),
      g_final.reshape(1, d))


def _layer(x, p, g_attn, w_in, w_br_a, w_br_b, w_out, g_ffn, w_grp, b_grp, w_rt, b_rt,
           w_gate, w_up, w_down, g_ple, w_ple_gate, w_ple_proj, g_final):
    b, s, d = x.shape
    t = b * s
    (qa_t, va_t, qb_t, vb_t, iq_t, iw_t, ka, kb, ik, ga, gb, km) = _in_projection(x, g_attn, w_in)
    o_a = _moba_attention(qa_t, ka, va_t, km.reshape(b, s // MOBA_BLOCK, MIXER_W))
    o_b = _dsa_attention(iq_t, ik, iw_t, qb_t, kb, vb_t)
    x1, h2, ids, cw, rank, counts = _merge_and_route(
        o_a.reshape(t, MIXER_W), o_b.reshape(t, MIXER_W), ga.reshape(t, d), gb.reshape(t, d),
        x.reshape(t, d), w_br_a, w_br_b, w_out, g_ffn, w_grp, b_grp, w_rt, b_rt)

    counts = counts[:, 0].astype(I32)
    padded = ((counts + EXPERT_ROWS - 1) // EXPERT_ROWS) * EXPERT_ROWS
    pend = jnp.cumsum(padded)
    pstart = pend - padded
    n_blocks = -(-(t * EXPERT_TOPK) // EXPERT_ROWS) + N_EXPERTS
    experts = jnp.arange(N_EXPERTS, dtype=I32)
    dest = jnp.sum(jnp.where(ids[..., None] == experts, pstart, 0), axis=-1) + rank
    block_row0 = jnp.arange(n_blocks, dtype=I32) * EXPERT_ROWS
    blk_exp = jnp.minimum(jnp.sum((pend[None, :] <= block_row0[:, None]).astype(I32), axis=1), N_EXPERTS - 1)
    n_used = (pend[-1:] // EXPERT_ROWS).astype(I32)

    route_tile = dest.shape[2]
    dest = dest.reshape(-1)
    xs = _dispatch(dest, h2, n_blocks * EXPERT_ROWS, route_tile)
    ys = _experts(blk_exp, n_used, xs, w_gate, w_up, w_down)
    cw_tok = jnp.swapaxes(cw, 1, 2).reshape(t, EXPERT_TOPK)
    out = _combine_ple(dest, x1, cw_tok, p.reshape(t, p.shape[-1]), ys, g_ple, w_ple_gate, w_ple_proj, g_final,
                       route_tile)
    return out.reshape(b, s, d)


def kernel(x, p, g_attn, w_in, w_br_a, w_br_b, w_out, g_ffn, w_grp, b_grp, w_rt, b_rt, w_gate, w_up, w_down, g_ple, w_ple_gate, w_ple_proj, g_final):
    depth = w_in.shape[0]
    assert depth == 1, "the final RMSNorm is fused into the last layer's kernel"
    i = 0
    return _layer(x, p[i], g_attn[i], w_in[i], w_br_a[i], w_br_b[i], w_out[i], g_ffn[i], w_grp[i], b_grp[i],
                  w_rt[i], b_rt[i], w_gate[i], w_up[i], w_down[i], g_ple[i], w_ple_gate[i], w_ple_proj[i], g_final)
```

```python
import functools

import jax
import jax.numpy as jnp
from jax import lax
from jax.experimental import pallas as pl
from jax.experimental.pallas import tpu as pltpu

F32 = jnp.float32
BF16 = jnp.bfloat16
I32 = jnp.int32

HEAD_DIM = 64
N_HEADS = 8
ROT_DIM = HEAD_DIM // 4
ROT_HALF = ROT_DIM // 2
ROPE_THETA = 500000.0
MOBA_BLOCK = 256
MOBA_TOPK = 3
IDX_HEADS = 8
IDX_DIM = 64
DSA_TOPK_MAX = 256
N_GROUPS = 4
EXPERTS_PER_GROUP = 8
N_EXPERTS = N_GROUPS * EXPERTS_PER_GROUP
D_EXPERT = 512
PLE_DIM = 256
RMS_EPS = 1e-6
MIXER_W = N_HEADS * HEAD_DIM

Q_TILE = 256
LANES = 128
SUBLANES = 8
VMEM_LIMIT = 56 * 1024 * 1024

NT_DIMS = (((1,), (1,)), ((), ()))


def _cparams(sem):
    return pltpu.CompilerParams(dimension_semantics=sem, vmem_limit_bytes=VMEM_LIMIT)


def _const_spec(shape):
    nd = len(shape)
    return pl.BlockSpec(shape, lambda *_: (0,) * nd, pipeline_mode=pl.Buffered(1))


def _rope_feature_major(z, cos_t, sin_t):
    tm = z.shape[1]
    z3 = z.reshape(N_HEADS, HEAD_DIM, tm)
    x1 = z3[:, 0:ROT_HALF, :]
    x2 = z3[:, ROT_HALF:ROT_DIM, :]
    o1 = x1 * cos_t - x2 * sin_t
    o2 = x2 * cos_t + x1 * sin_t
    return jnp.concatenate([o1, o2, z3[:, ROT_DIM:, :]], axis=1).reshape(N_HEADS * HEAD_DIM, tm)


def _rope_token_major(z, c_tab, s_lo, s_hi):
    up = pltpu.roll(z, LANES - ROT_HALF, 1)
    dn = pltpu.roll(z, ROT_HALF, 1)
    return z * c_tab + up * s_lo + dn * s_hi


def _store_blocked(ref, z):
    for c in range(z.shape[1] // Q_TILE):
        ref[0, c] = z[:, c * Q_TILE:(c + 1) * Q_TILE]


def _inproj_kernel(x_ref, g_ref, wf_ref, wiw_ref, wt_ref, wg_ref, cos_t_ref, sin_t_ref,
                   ctab_ref, slo_ref, shi_ref,
                   qa_t_ref, va_t_ref, qb_t_ref, vb_t_ref, iq_t_ref, iw_t_ref,
                   ka_ref, kb_ref, ik_ref, ga_ref, gb_ref, km_ref, *, w_scale):
    x = x_ref[0]
    ms = jnp.mean(x * x, axis=-1, keepdims=True)
    h = ((x * lax.rsqrt(ms + RMS_EPS)) * g_ref[...]).astype(BF16)
    cos_t = cos_t_ref[...]
    sin_t = sin_t_ref[...]
    q_scale = HEAD_DIM ** -0.5

    fm_outs = ((qa_t_ref, True, q_scale), (va_t_ref, False, 1.0), (qb_t_ref, True, q_scale),
               (vb_t_ref, False, 1.0), (iq_t_ref, True, IDX_DIM ** -0.5))
    for i, (ref, rope, scale) in enumerate(fm_outs):
        z = lax.dot_general(wf_ref[i * MIXER_W:(i + 1) * MIXER_W, :], h, NT_DIMS,
                            preferred_element_type=F32)
        if rope:
            z = _rope_feature_major(z, cos_t, sin_t)
        if scale != 1.0:
            z = z * scale
        _store_blocked(ref, z.astype(BF16))

    iw = lax.dot_general(wiw_ref[...], h, NT_DIMS, preferred_element_type=F32)
    _store_blocked(iw_t_ref, iw * w_scale)

    zt = jnp.dot(h, wt_ref[...], preferred_element_type=F32)
    ctab, slo, shi = ctab_ref[...], slo_ref[...], shi_ref[...]
    n_grp = zt.shape[1] // LANES
    roped = [_rope_token_major(zt[:, j * LANES:(j + 1) * LANES], ctab, slo, shi) for j in range(n_grp)]
    per_mixer = MIXER_W // LANES
    ka = jnp.concatenate(roped[:per_mixer], axis=1)
    kb = jnp.concatenate(roped[per_mixer:2 * per_mixer], axis=1)
    ka_ref[0] = ka.astype(BF16)
    kb_ref[0] = kb.astype(BF16)
    ik_ref[0] = roped[2 * per_mixer][:, :IDX_DIM].astype(BF16)
    tm = ka.shape[0]
    km_ref[0] = jnp.mean(ka.reshape(tm // MOBA_BLOCK, MOBA_BLOCK, MIXER_W), axis=1, keepdims=True)

    zg = jnp.dot(h, wg_ref[...], preferred_element_type=F32)
    d_model = zg.shape[1] // 2
    ga_ref[0] = zg[:, :d_model]
    gb_ref[0] = zg[:, d_model:]


def _rope_tables(seq):
    inv = 1.0 / (ROPE_THETA ** (jnp.arange(0, ROT_DIM, 2, dtype=F32) / ROT_DIM))
    ang = jnp.arange(seq, dtype=F32)[:, None] * inv[None, :]
    cos, sin = jnp.cos(ang), jnp.sin(ang)
    d = jnp.arange(LANES) % HEAD_DIM
    lo = d < ROT_HALF
    hi = (d >= ROT_HALF) & (d < ROT_DIM)
    f = d % ROT_HALF
    cos_l, sin_l = cos[:, f], sin[:, f]
    ctab = jnp.where(lo | hi, cos_l, 1.0)
    slo = jnp.where(lo, -sin_l, 0.0)
    shi = jnp.where(hi, sin_l, 0.0)
    return cos.T, sin.T, ctab, slo, shi


def _in_projection(x, g_attn, w_in, tm=512):
    b, s, d = x.shape
    splits = (MIXER_W,) * 6 + (IDX_HEADS * IDX_DIM, IDX_DIM, IDX_HEADS, d, d)
    offs = [0]
    for w in splits:
        offs.append(offs[-1] + w)
    wqa, wka, wva, wqb, wkb, wvb, wiq, wik, wiw, wga, wgb = (
        w_in[:, offs[i]:offs[i + 1]] for i in range(len(splits)))
    wf = jnp.concatenate([wqa, wva, wqb, wvb, wiq], axis=1).T.astype(BF16)
    wiw_t = wiw.T.astype(BF16)
    wt = jnp.concatenate([wka, wkb, wik, jnp.zeros((d, LANES - IDX_DIM), w_in.dtype)], axis=1).astype(BF16)
    wg = jnp.concatenate([wga, wgb], axis=1).astype(BF16)
    cos_t, sin_t, ctab, slo, shi = _rope_tables(s)
    n_blk = s // MOBA_BLOCK
    fm_shape = jax.ShapeDtypeStruct((b, s // Q_TILE, MIXER_W, Q_TILE), BF16)
    tok_shape = jax.ShapeDtypeStruct((b, s, MIXER_W), BF16)
    fm_spec = pl.BlockSpec((1, tm // Q_TILE, MIXER_W, Q_TILE), lambda bi, ti: (bi, ti, 0, 0))
    tok_spec = pl.BlockSpec((1, tm, MIXER_W), lambda bi, ti: (bi, ti, 0))
    gate_spec = pl.BlockSpec((1, tm, d), lambda bi, ti: (bi, ti, 0))
    outs = pl.pallas_call(
        functools.partial(_inproj_kernel, w_scale=IDX_HEADS ** -0.5),
        grid=(b, s // tm),
        in_specs=[
            pl.BlockSpec((1, tm, d), lambda bi, ti: (bi, ti, 0)),
            _const_spec((1, d)),
            _const_spec(wf.shape), _const_spec(wiw_t.shape), _const_spec(wt.shape), _const_spec(wg.shape),
            pl.BlockSpec((ROT_HALF, tm), lambda bi, ti: (0, ti)),
            pl.BlockSpec((ROT_HALF, tm), lambda bi, ti: (0, ti)),
            pl.BlockSpec((tm, LANES), lambda bi, ti: (ti, 0)),
            pl.BlockSpec((tm, LANES), lambda bi, ti: (ti, 0)),
            pl.BlockSpec((tm, LANES), lambda bi, ti: (ti, 0)),
        ],
        out_specs=[
            fm_spec, fm_spec, fm_spec, fm_spec, fm_spec,
            pl.BlockSpec((1, tm // Q_TILE, IDX_HEADS, Q_TILE), lambda bi, ti: (bi, ti, 0, 0)),
            tok_spec, tok_spec,
            pl.BlockSpec((1, tm, IDX_DIM), lambda bi, ti: (bi, ti, 0)),
            gate_spec, gate_spec,
            pl.BlockSpec((1, tm // MOBA_BLOCK, 1, MIXER_W), lambda bi, ti: (bi, ti, 0, 0)),
        ],
        out_shape=[
            fm_shape, fm_shape, fm_shape, fm_shape, fm_shape,
            jax.ShapeDtypeStruct((b, s // Q_TILE, IDX_HEADS, Q_TILE), F32),
            tok_shape, tok_shape,
            jax.ShapeDtypeStruct((b, s, IDX_DIM), BF16),
            jax.ShapeDtypeStruct((b, s, d), F32), jax.ShapeDtypeStruct((b, s, d), F32),
            jax.ShapeDtypeStruct((b, n_blk, 1, MIXER_W), F32),
        ],
        compiler_params=_cparams(("parallel", "parallel")),
        name="in_projection",
    )(x, g_attn.reshape(1, d), wf, wiw_t, wt, wg, cos_t, sin_t, ctab, slo, shi)
    return outs


MASKED = -1e30


HEADS_PER_GROUP = LANES // HEAD_DIM


def _one_head_of_pair(q_pair, hh):
    row = lax.broadcasted_iota(I32, q_pair.shape, 0)
    return jnp.where((row // HEAD_DIM) == hh, q_pair, jnp.zeros_like(q_pair))


def _head_lanes(h):
    g = h // HEADS_PER_GROUP
    return slice(g * LANES, (g + 1) * LANES)


def _head_rows(h):
    return slice(h * HEAD_DIM, (h + 1) * HEAD_DIM)


ATTN_GROUP = 2
ONES_ROWS = 16
ACC_ROWS = HEAD_DIM + ONES_ROWS


def _split_heads(q_all, qh_ref):
    for h in range(N_HEADS):
        qh_ref[h] = _one_head_of_pair(q_all[_head_lanes(h), :], h % HEADS_PER_GROUP)


def _weighted_values(s, m, v_t):
    p = jnp.exp(s - m).astype(BF16)
    lhs = jnp.concatenate([v_t, jnp.ones((ONES_ROWS, v_t.shape[1]), BF16)], axis=0)
    return jnp.dot(lhs, p, preferred_element_type=F32)


def _loop_in_groups(n, body, init, width=4):
    carry = lax.fori_loop(0, n // width, lambda k, c: body(width * k, width, c), init)
    done = (n // width) * width
    part = width // 2
    while part >= 1:
        has_part = ((n - done) // part) % 2 == 1
        carry = lax.cond(has_part, lambda c, done=done, part=part: body(done, part, c), lambda c: c, carry)
        done = done + jnp.where(has_part, part, 0)
        part //= 2
    return carry


def _store_head_output(o_t_ref, h, acc):
    o_t_ref[_head_rows(h), :] = acc[:HEAD_DIM, :] / acc[HEAD_DIM:HEAD_DIM + 1, :]


def _moba_kernel(q_t_ref, k_ref, v_t_ref, km_ref, o_ref, qh_ref, bias_ref, s_even_ref, s_odd_ref, o_t_ref,
                 *acc_refs):
    s_refs = (s_even_ref, s_odd_ref)
    i = pl.program_id(1)
    tq = q_t_ref.shape[3]
    n_blk = km_ref.shape[1]
    _split_heads(q_t_ref[0, 0], qh_ref)

    blk = lax.broadcasted_iota(I32, (n_blk, tq), 0)
    for h in range(N_HEADS):
        gate = jnp.dot(km_ref[0, :, _head_lanes(h)], qh_ref[h].astype(F32), preferred_element_type=F32)
        gate = jnp.where(blk < i, gate, -jnp.inf)
        keep = blk == i
        for _ in range(MOBA_TOPK):
            _, first = _first_index_of_max(gate, blk)
            taken = (blk == first) & (blk < i)
            keep = keep | taken
            gate = jnp.where(taken, -jnp.inf, gate)
        bias_ref[h] = jnp.where(keep, 0.0, MASKED)

    def key_rows(j):
        return pl.ds(pl.multiple_of(j * tq, tq), tq)

    def block_bias(h, j):
        return bias_ref[h, pl.ds(j, 1), :]

    causal = lax.broadcasted_iota(I32, (tq, tq), 0) <= lax.broadcasted_iota(I32, (tq, tq), 1)

    def scores(h, j):
        return jnp.dot(k_ref[0, key_rows(j), _head_lanes(h)], qh_ref[h], preferred_element_type=F32)

    def own_block_scores(g):
        own_max = []
        for gi, h in _group_heads(g):
            s = jnp.where(causal, scores(h, i), MASKED)
            s_refs[g % 2][gi, key_rows(i), :] = s
            own_max.append(jnp.max(s, axis=0, keepdims=True))
        return tuple(own_max)

    def past_scores(g, j, mx):
        for gi, h in _group_heads(g):
            s = scores(h, j)
            s_refs[g % 2][gi, key_rows(j), :] = s
            mx[gi] = jnp.maximum(mx[gi], jnp.max(s, axis=0, keepdims=True) + block_bias(h, j))

    def weighted(g, gi, h, j, mx):
        return _weighted_values(s_refs[g % 2][gi, key_rows(j), :], mx[gi] - block_bias(h, j),
                                v_t_ref[0, j, _head_rows(h), :])

    def finish(g, mx):
        for gi, h in _group_heads(g):
            _store_head_output(o_t_ref, h, acc_refs[gi][...] + weighted(g, gi, h, i, mx))

    _attention_pipeline(i, own_block_scores, past_scores, weighted, finish, acc_refs)
    o_ref[0] = o_t_ref[...].T.astype(BF16)


def _group_heads(g):
    return list(enumerate(range(g * ATTN_GROUP, (g + 1) * ATTN_GROUP)))


def _attention_pipeline(n, first_max, score_step, weighted, finish, acc_refs):
    n_groups = N_HEADS // ATTN_GROUP
    prev_mx = None
    for g in range(n_groups + 1):
        scoring, weighting = g < n_groups, g > 0
        if weighting:
            for acc_ref in acc_refs:
                acc_ref[...] = jnp.zeros(acc_ref.shape, F32)

        def body(j0, count, mx, g=g, scoring=scoring, weighting=weighting, prev_mx=prev_mx):
            mx = list(mx)
            partial = [0.0] * ATTN_GROUP
            for d in range(count):
                if scoring:
                    score_step(g, j0 + d, mx)
                if weighting:
                    for gi, h in _group_heads(g - 1):
                        partial[gi] = partial[gi] + weighted(g - 1, gi, h, j0 + d, prev_mx)
            if weighting:
                for gi, _ in _group_heads(g - 1):
                    acc_refs[gi][...] += partial[gi]
            return tuple(mx)

        mx = _loop_in_groups(n, body, first_max(g) if scoring else ())
        if weighting:
            finish(g - 1, prev_mx)
        prev_mx = mx


def _attention_scratch(s, w, tq):
    return ([pltpu.VMEM((N_HEADS, LANES, tq), BF16)],
            [pltpu.VMEM((ATTN_GROUP, s, tq), F32)] * 2 + [pltpu.VMEM((w, tq), F32)]
            + [pltpu.VMEM((ACC_ROWS, tq), F32)] * ATTN_GROUP)


def _resident_spec(shape):
    nd = len(shape)
    return pl.BlockSpec(shape, lambda bi, i: (bi,) + (0,) * (nd - 1), pipeline_mode=pl.Buffered(1))


def _moba_attention(qa_t, ka, va_t, km):
    b, n_blk, w, tq = qa_t.shape
    s = n_blk * tq
    qh, work = _attention_scratch(s, w, tq)
    return pl.pallas_call(
        _moba_kernel,
        grid=(b, n_blk),
        in_specs=[
            pl.BlockSpec((1, 1, w, tq), lambda bi, i: (bi, i, 0, 0)),
            _resident_spec((1, s, w)),
            _resident_spec((1, n_blk, w, tq)),
            _resident_spec((1, n_blk, w)),
        ],
        out_specs=pl.BlockSpec((1, tq, w), lambda bi, i: (bi, i, 0)),
        out_shape=jax.ShapeDtypeStruct((b, s, w), BF16),
        scratch_shapes=qh + [pltpu.VMEM((N_HEADS, n_blk, tq), F32)] + work,
        compiler_params=_cparams(("parallel", "arbitrary")),
        name="moba_attention",
    )(qa_t, ka, va_t, km)


INT_MIN = -2 ** 31


def _sortable_key(s):
    bits = lax.bitcast_convert_type(s, I32)
    return bits ^ ((bits >> 31) & 0x7FFFFFFF)


def _column_count(mask):
    tk, tq = mask.shape
    return jnp.sum(mask.astype(I32).reshape(tk // SUBLANES, SUBLANES, tq), axis=0)


I16 = jnp.int16
HALF_BITS = 16
I16_MIN = -2 ** (HALF_BITS - 1)
PACKED_ROWS = 2 * SUBLANES
SEARCH_GROUP = 4


def _packed_column_count(mask):
    ones = mask.astype(I16)
    parts = [ones[r:r + PACKED_ROWS, :] for r in range(0, ones.shape[0], PACKED_ROWS)]
    while len(parts) > 1:
        parts = [a + b for a, b in zip(parts[::2], parts[1::2])] + parts[len(parts) - len(parts) % 2:]
    return parts[0]


def _packed_total(count16):
    return jnp.sum(count16.astype(I32), axis=0, keepdims=True)


def _dsa_kernel(iq_t_ref, ik_ref, iw_t_ref, q_t_ref, k_ref, v_t_ref, o_ref,
                key_ref, hi_ref, lo_ref, lo2_ref, bias_ref, qh_ref, s_even_ref, s_odd_ref, o_t_ref, *acc_refs,
                topk, index_bits):
    s_refs = (s_even_ref, s_odd_ref)
    t = pl.program_id(1)
    tq = q_t_ref.shape[3]
    tk = tq
    n_chunks = t + 1
    qpos = t * tq + lax.broadcasted_iota(I32, (1, tq), 1)
    krow = lax.broadcasted_iota(I32, (tk, tq), 0)

    def chunk_rows(c):
        return pl.ds(pl.multiple_of(c * tk, tk), tk)

    def total(count8):
        return jnp.sum(count8, axis=0, keepdims=True)

    def count_over_chunks(pred):
        def body(c, cnt):
            return cnt + _column_count(pred(key_ref[chunk_rows(c), :], c * tk + krow))
        return total(lax.fori_loop(0, n_chunks, body, jnp.zeros((SUBLANES, tq), I32)))

    iq = iq_t_ref[0, 0]
    iw = iw_t_ref[0, 0]

    def score_chunks(c0, count, carry):
        for c in (c0 + d for d in range(count)):
            ik_c = ik_ref[0, chunk_rows(c), :]
            score = jnp.zeros((tk, tq), F32)
            for h in range(IDX_HEADS):
                rel = jnp.dot(ik_c, iq[h * IDX_DIM:(h + 1) * IDX_DIM, :], preferred_element_type=F32)
                score = score + jnp.maximum(rel, 0.0) * iw[h:h + 1, :]
            key = jnp.where(c * tk + krow <= qpos, _sortable_key(score), INT_MIN)
            key_ref[chunk_rows(c), :] = key
            hi_ref[chunk_rows(c), :] = (key >> HALF_BITS).astype(I16)
            lo_ref[chunk_rows(c), :] = key.astype(I16) ^ jnp.int16(I16_MIN)
        return carry

    _loop_in_groups(n_chunks, score_chunks, 0)

    n_groups = (n_chunks + SEARCH_GROUP - 1) // SEARCH_GROUP
    for d in range(SEARCH_GROUP - 1):
        @pl.when(n_chunks + d < n_groups * SEARCH_GROUP)
        def _(d=d):
            lowest = jnp.full((tk, tq), I16_MIN, I16)
            hi_ref[chunk_rows(n_chunks + d), :] = lowest
            lo_ref[chunk_rows(n_chunks + d), :] = lowest

    def packed_counts(n_counts, chunk_masks):
        def body(g, cnts):
            cnts = list(cnts)
            for c in (g * SEARCH_GROUP + d for d in range(SEARCH_GROUP)):
                for i, mask in enumerate(chunk_masks(c)):
                    cnts[i] = cnts[i] + _packed_column_count(mask)
            return tuple(cnts)
        zero = jnp.zeros((PACKED_ROWS, tq), I16)
        return [_packed_total(c) for c in lax.fori_loop(0, n_groups, body, (zero,) * n_counts)]

    def kth_largest_half(half_ref, kth):
        def half_bit(b, prefix):
            cand = prefix | jnp.left_shift(jnp.int32(1), HALF_BITS - 1 - b)
            cand_half = (cand + I16_MIN).astype(I16)
            n_ge, = packed_counts(1, lambda c: [half_ref[chunk_rows(c), :] >= cand_half])
            return jnp.where(n_ge >= kth, cand, prefix)

        return lax.fori_loop(0, HALF_BITS, half_bit, jnp.zeros((1, tq), I32)) + I16_MIN

    thr_hi = kth_largest_half(hi_ref, topk)
    thr_hi_half = thr_hi.astype(I16)

    def above_hi(c):
        hi = hi_ref[chunk_rows(c), :]
        lo2_ref[chunk_rows(c), :] = jnp.where(hi == thr_hi_half, lo_ref[chunk_rows(c), :], jnp.int16(I16_MIN))
        return [hi > thr_hi_half]

    n_gt_hi, = packed_counts(1, above_hi)
    thr_lo = kth_largest_half(lo2_ref, topk - n_gt_hi)
    thr_lo_half = thr_lo.astype(I16)
    thr = jnp.left_shift(thr_hi, HALF_BITS) | (thr_lo - I16_MIN)

    n_gt_lo, n_eq = packed_counts(2, lambda c: [
        lo2_ref[chunk_rows(c), :] > thr_lo_half,
        (hi_ref[chunk_rows(c), :] == thr_hi_half) & (lo_ref[chunk_rows(c), :] == thr_lo_half)])
    need = topk - (n_gt_hi + n_gt_lo)
    has_surplus = jnp.any((n_eq > need) & (thr != INT_MIN))

    def last_tie_index():
        def index_bit(b, prefix):
            cand = prefix | jnp.left_shift(jnp.int32(1), index_bits - 1 - b)
            n_before = count_over_chunks(lambda keys, kpos: (keys == thr) & (kpos < cand))
            return jnp.where(n_before < need, cand, prefix)
        return lax.fori_loop(0, index_bits, index_bit, jnp.zeros((1, tq), I32))

    def bias_with_ties():
        tie_end = last_tie_index()

        def bias_chunk(c, carry):
            keys = key_ref[chunk_rows(c), :]
            kpos = c * tk + krow
            chosen = ((keys > thr) | ((keys == thr) & (kpos <= tie_end))) & (kpos <= qpos)
            bias_ref[chunk_rows(c), :] = jnp.where(chosen, 0.0, MASKED)
            return carry

        lax.fori_loop(0, n_chunks, bias_chunk, 0)

    def bias_without_ties():
        floor = jnp.maximum(thr, INT_MIN + 1)

        def bias_chunks(c0, count, carry):
            for c in (c0 + d for d in range(count)):
                bias_ref[chunk_rows(c), :] = jnp.where(key_ref[chunk_rows(c), :] >= floor, 0.0, MASKED)
            return carry

        _loop_in_groups(n_chunks, bias_chunks, 0)

    lax.cond(has_surplus, bias_with_ties, bias_without_ties)

    _split_heads(q_t_ref[0, 0], qh_ref)

    def masked_scores(g, c, mx):
        for gi, h in _group_heads(g):
            s = jnp.dot(k_ref[0, chunk_rows(c), _head_lanes(h)], qh_ref[h], preferred_element_type=F32)
            s = s + bias_ref[chunk_rows(c), :]
            s_refs[g % 2][gi, chunk_rows(c), :] = s
            mx[gi] = jnp.maximum(mx[gi], jnp.max(s, axis=0, keepdims=True))

    def weighted(g, gi, h, c, mx):
        return _weighted_values(s_refs[g % 2][gi, chunk_rows(c), :], mx[gi], v_t_ref[0, c, _head_rows(h), :])

    def finish(g, mx):
        for gi, h in _group_heads(g):
            _store_head_output(o_t_ref, h, acc_refs[gi][...])

    _attention_pipeline(n_chunks, lambda g: (jnp.full((1, tq), MASKED, F32),) * ATTN_GROUP,
                        masked_scores, weighted, finish, acc_refs)
    o_ref[0] = o_t_ref[...].T.astype(BF16)


def _dsa_attention(iq_t, ik, iw_t, qb_t, kb, vb_t):
    b, n_blk, w, tq = qb_t.shape
    s = n_blk * tq
    topk = min(DSA_TOPK_MAX, s // 4)
    assert n_blk % SEARCH_GROUP == 0, "the packed search pads its chunk range to whole groups"
    qh, work = _attention_scratch(s, w, tq)
    tile_spec = lambda rows: pl.BlockSpec((1, 1, rows, tq), lambda bi, ti: (bi, ti, 0, 0))
    return pl.pallas_call(
        functools.partial(_dsa_kernel, topk=topk, index_bits=max(1, (s - 1).bit_length())),
        grid=(b, n_blk),
        in_specs=[
            tile_spec(IDX_HEADS * IDX_DIM),
            _resident_spec((1, s, IDX_DIM)),
            tile_spec(IDX_HEADS),
            tile_spec(w),
            _resident_spec((1, s, w)),
            _resident_spec((1, n_blk, w, tq)),
        ],
        out_specs=pl.BlockSpec((1, tq, w), lambda bi, ti: (bi, ti, 0)),
        out_shape=jax.ShapeDtypeStruct((b, s, w), BF16),
        scratch_shapes=[pltpu.VMEM((s, tq), I32)] + [pltpu.VMEM((s, tq), I16)] * 3
        + [pltpu.VMEM((s, tq), F32)] + qh + work,
        compiler_params=_cparams(("parallel", "arbitrary")),
        name="dsa_attention",
    )(iq_t, ik, iw_t, qb_t, kb, vb_t)


EXPERT_ROW0 = SUBLANES
ROUTER_ROWS = EXPERT_ROW0 + N_EXPERTS
EXPERT_TOPK = 2


def _rms_norm(x, g):
    ms = jnp.mean(x * x, axis=-1, keepdims=True)
    return (x * lax.rsqrt(ms + RMS_EPS)) * g


def _first_index_of_max(vals, idx):
    top = jnp.max(vals, axis=0, keepdims=True)
    first = jnp.min(jnp.where(vals == top, idx, vals.shape[0]), axis=0, keepdims=True)
    return top, first


def _merge_router_kernel(oa_ref, ob_ref, ga_ref, gb_ref, x_ref, wa_ref, wb_ref, wo_ref, g_ref,
                         wr_ref, br_ref, x1_ref, h2_ref, ids_ref, cw_ref, rank_ref, cnt_ref):
    @pl.when(pl.program_id(0) == 0)
    def _():
        cnt_ref[...] = jnp.zeros_like(cnt_ref)

    a = jnp.dot(oa_ref[...], wa_ref[...], preferred_element_type=F32)
    b = jnp.dot(ob_ref[...], wb_ref[...], preferred_element_type=F32)
    merged = jax.nn.sigmoid(ga_ref[...]) * a + jax.nn.sigmoid(gb_ref[...]) * b
    x1 = x_ref[...] + jnp.dot(merged.astype(BF16), wo_ref[...], preferred_element_type=F32)
    x1_ref[...] = x1
    h2 = _rms_norm(x1, g_ref[...])
    h2_ref[...] = h2
    tm = h2.shape[0]

    logits = lax.dot_general(wr_ref[...], h2, NT_DIMS, preferred_element_type=F32,
                             precision=lax.Precision.HIGHEST) + br_ref[...]
    grp = logits[0:N_GROUPS, :]
    g_top, g_idx = _first_index_of_max(grp, lax.broadcasted_iota(I32, grp.shape, 0))
    p_grp = 1.0 / jnp.sum(jnp.exp(grp - g_top), axis=0, keepdims=True)
    in_grp = logits[EXPERT_ROW0:EXPERT_ROW0 + EXPERTS_PER_GROUP, :]
    for gi in range(1, N_GROUPS):
        rows = slice(EXPERT_ROW0 + gi * EXPERTS_PER_GROUP, EXPERT_ROW0 + (gi + 1) * EXPERTS_PER_GROUP)
        in_grp = jnp.where(g_idx == gi, logits[rows, :], in_grp)
    e_iota = lax.broadcasted_iota(I32, in_grp.shape, 0)
    v0, i0 = _first_index_of_max(in_grp, e_iota)
    v1, i1 = _first_index_of_max(jnp.where(e_iota == i0, -jnp.inf, in_grp), e_iota)
    e1 = jnp.exp(v1 - v0)
    denom = 1.0 + e1
    ids = jnp.concatenate([g_idx * EXPERTS_PER_GROUP + i0, g_idx * EXPERTS_PER_GROUP + i1], axis=0)
    ids_ref[0] = ids
    cw_ref[0] = jnp.concatenate([p_grp * (1.0 / denom), p_grp * (e1 / denom)], axis=0)

    before = (lax.broadcasted_iota(I32, (tm, tm), 0) < lax.broadcasted_iota(I32, (tm, tm), 1)).astype(BF16)
    expert = lax.broadcasted_iota(I32, (N_EXPERTS, tm), 0)
    ranks = []
    for slot in range(EXPERT_TOPK):
        onehot = expert == ids[slot:slot + 1, :]
        seen = jnp.dot(onehot.astype(BF16), before, preferred_element_type=F32) + cnt_ref[...]
        ranks.append(jnp.sum(jnp.where(onehot, seen, 0.0), axis=0, keepdims=True))
        cnt_ref[...] += jnp.sum(onehot.astype(F32), axis=1, keepdims=True)
    rank_ref[0] = jnp.concatenate(ranks, axis=0).astype(I32)


def _merge_and_route(o_a, o_b, ga, gb, x, w_br_a, w_br_b, w_out, g_ffn, w_grp, b_grp, w_rt, b_rt, tm=512):
    t, d = x.shape
    n_tiles = t // tm
    wr = jnp.zeros((ROUTER_ROWS, d), F32).at[0:N_GROUPS].set(w_grp.T).at[EXPERT_ROW0:].set(w_rt.T)
    br = jnp.zeros((ROUTER_ROWS, 1), F32).at[0:N_GROUPS, 0].set(b_grp).at[EXPERT_ROW0:, 0].set(b_rt)
    row_spec = lambda w: pl.BlockSpec((tm, w), lambda i: (i, 0))
    slot_spec = pl.BlockSpec((1, EXPERT_TOPK, tm), lambda i: (i, 0, 0))
    slot_shape = lambda dt: jax.ShapeDtypeStruct((n_tiles, EXPERT_TOPK, tm), dt)
    return pl.pallas_call(
        _merge_router_kernel,
        grid=(n_tiles,),
        in_specs=[row_spec(MIXER_W), row_spec(MIXER_W), row_spec(d), row_spec(d), row_spec(d),
                  _const_spec((MIXER_W, d)), _const_spec((MIXER_W, d)), _const_spec((d, d)),
                  _const_spec((1, d)), _const_spec((ROUTER_ROWS, d)), _const_spec((ROUTER_ROWS, 1))],
        out_specs=[row_spec(d), row_spec(d), slot_spec, slot_spec, slot_spec,
                   pl.BlockSpec((N_EXPERTS, 1), lambda i: (0, 0))],
        out_shape=[jax.ShapeDtypeStruct((t, d), F32), jax.ShapeDtypeStruct((t, d), F32),
                   slot_shape(I32), slot_shape(F32), slot_shape(I32),
                   jax.ShapeDtypeStruct((N_EXPERTS, 1), F32)],
        compiler_params=_cparams(("arbitrary",)),
        name="merge_and_route",
    )(o_a, o_b, ga, gb, x, w_br_a.astype(BF16), w_br_b.astype(BF16), w_out.astype(BF16),
      g_ffn.reshape(1, d), wr, br)


EXPERT_ROWS = 512
DMA_ISSUE_UNROLL = 8


def _row(ref, r):
    return ref.at[pl.ds(r, 1), :]


def _tile_dest(dest_ref, tm, slot, r):
    return dest_ref[(pl.program_id(0) * EXPERT_TOPK + slot) * tm + r]


def _dispatch_kernel(dest_ref, pad_block_ref, h2_ref, xs_hbm, zero_ref, sem, zero_sem):
    tm = h2_ref.shape[0]

    @pl.when(pl.program_id(0) == 0)
    def _():
        zero_ref[...] = jnp.zeros(zero_ref.shape, F32)

        def block_copy(b):
            rows = pl.ds(pl.multiple_of(b * EXPERT_ROWS, EXPERT_ROWS), EXPERT_ROWS)
            return pltpu.make_async_copy(zero_ref, xs_hbm.at[rows, :], zero_sem)

        def start(b, n):
            @pl.when(pad_block_ref[b] != 0)
            def _():
                block_copy(b).start()
            return n + pad_block_ref[b]

        n_started = lax.fori_loop(0, pad_block_ref.shape[0], start, 0)

        def drain(k, carry):
            block_copy(0).wait()
            return carry

        lax.fori_loop(0, n_started, drain, 0)

    for r in range(tm):
        for slot in range(EXPERT_TOPK):
            pltpu.make_async_copy(_row(h2_ref, r), _row(xs_hbm, _tile_dest(dest_ref, tm, slot, r)), sem).start()
    for slot in range(EXPERT_TOPK):
        pltpu.make_async_copy(h2_ref, xs_hbm.at[pl.ds(0, tm), :], sem).wait()


def _dispatch(dest, pad_block, h2, tm):
    t, d = h2.shape
    n_rows = pad_block.shape[0] * EXPERT_ROWS
    return pl.pallas_call(
        _dispatch_kernel,
        grid_spec=pltpu.PrefetchScalarGridSpec(
            num_scalar_prefetch=2,
            grid=(t // tm,),
            in_specs=[pl.BlockSpec((tm, d), lambda i, dest, pad: (i, 0))],
            out_specs=pl.BlockSpec(memory_space=pl.ANY),
            scratch_shapes=[pltpu.VMEM((EXPERT_ROWS, d), F32), pltpu.SemaphoreType.DMA(()),
                            pltpu.SemaphoreType.DMA(())],
        ),
        out_shape=jax.ShapeDtypeStruct((n_rows, d), F32),
        compiler_params=_cparams(("arbitrary",)),
        name="moe_dispatch",
    )(dest, pad_block, h2)


def _expert_kernel(blk_exp_ref, n_used_ref, xs_ref, wg_ref, wu_ref, wd_ref, ys_ref, wg16, wu16, wd16):
    i = pl.program_id(0)
    used = i < n_used_ref[0]

    @pl.when((i == 0) | (blk_exp_ref[i] != blk_exp_ref[jnp.maximum(i - 1, 0)]))
    def _():
        wg16[...] = wg_ref[0].astype(BF16)
        wu16[...] = wu_ref[0].astype(BF16)
        wd16[...] = wd_ref[0].astype(BF16)

    @pl.when(used)
    def _():
        xb = xs_ref[...].astype(BF16)
        gate = jnp.dot(xb, wg16[...], preferred_element_type=F32)
        up = jnp.dot(xb, wu16[...], preferred_element_type=F32)
        hid = (gate * jax.nn.sigmoid(gate)) * up
        ys_ref[...] = jnp.dot(hid.astype(BF16), wd16[...], preferred_element_type=F32)

    @pl.when(jnp.logical_not(used))
    def _():
        ys_ref[...] = jnp.zeros_like(ys_ref)


def _experts(blk_exp, n_used, xs, w_gate, w_up, w_down):
    n_rows, d = xs.shape
    n_blocks = n_rows // EXPERT_ROWS
    expert_spec = lambda rows, cols: pl.BlockSpec((1, rows, cols), lambda i, be, nu: (be[i], 0, 0))
    return pl.pallas_call(
        _expert_kernel,
        grid_spec=pltpu.PrefetchScalarGridSpec(
            num_scalar_prefetch=2,
            grid=(n_blocks,),
            in_specs=[pl.BlockSpec((EXPERT_ROWS, d), lambda i, be, nu: (i, 0)),
                      expert_spec(d, D_EXPERT), expert_spec(d, D_EXPERT), expert_spec(D_EXPERT, d)],
            out_specs=pl.BlockSpec((EXPERT_ROWS, d), lambda i, be, nu: (i, 0)),
            scratch_shapes=[pltpu.VMEM((d, D_EXPERT), BF16), pltpu.VMEM((d, D_EXPERT), BF16),
                            pltpu.VMEM((D_EXPERT, d), BF16)],
        ),
        out_shape=jax.ShapeDtypeStruct((n_rows, d), F32),
        compiler_params=_cparams(("arbitrary",)),
        name="moe_experts",
    )(blk_exp, n_used, xs, w_gate, w_up, w_down)


def _combine_ple_kernel(dest_ref, x1_ref, cw_ref, p_ref, ys_hbm, g_ple_ref, wpg_ref, wpp_ref, g_fin_ref,
                        out_ref, y_even, y_odd, sem):
    tm = y_even.shape[1]
    i = pl.program_id(0)
    last_tile = 2 * pl.num_programs(0) - 1
    bufs = ((y_even, 0), (y_odd, 1))

    def start_row(tile, buf, sem_slot, r):
        for slot in range(EXPERT_TOPK):
            src = _row(ys_hbm, dest_ref[(tile * EXPERT_TOPK + slot) * tm + r])
            pltpu.make_async_copy(src, _row(buf.at[slot], r), sem.at[sem_slot]).start()

    def wait_tile(buf, sem_slot):
        for slot in range(EXPERT_TOPK):
            pltpu.make_async_copy(ys_hbm.at[pl.ds(0, tm), :], buf.at[slot], sem.at[sem_slot]).wait()

    def combine(half, buf):
        rows = slice(half * tm, (half + 1) * tm)
        cw = cw_ref[rows, :]
        x2 = x1_ref[rows, :] + (buf[0] * cw[:, 0:1] + buf[1] * cw[:, 1:2])
        h3 = _rms_norm(x2, g_ple_ref[...]).astype(BF16)
        gate = jax.nn.sigmoid(jnp.dot(h3, wpg_ref[...], preferred_element_type=F32))
        proj = jnp.dot(p_ref[rows, :].astype(BF16), wpp_ref[...], preferred_element_type=F32)
        out_ref[rows, :] = _rms_norm(x2 + gate * proj, g_fin_ref[...])

    @pl.when(i == 0)
    def _():
        def issue(r, carry):
            start_row(0, *bufs[0], r)
            return carry
        lax.fori_loop(0, tm, issue, 0, unroll=DMA_ISSUE_UNROLL)

    for half in range(2):
        buf, sem_slot = bufs[half]
        wait_tile(buf, sem_slot)
        combine(half, buf)
        next_tile = jnp.minimum(2 * i + half + 1, last_tile)
        for r in range(tm):
            start_row(next_tile, *bufs[1 - half], r)

    @pl.when(i == pl.num_programs(0) - 1)
    def _():
        wait_tile(*bufs[0])


def _combine_ple(dest, x1, cw_tok, p, ys, g_ple, w_ple_gate, w_ple_proj, g_final, tm):
    t, d = x1.shape
    assert t % (2 * tm) == 0
    row_spec = lambda w: pl.BlockSpec((2 * tm, w), lambda i, dest: (i, 0))
    return pl.pallas_call(
        _combine_ple_kernel,
        grid_spec=pltpu.PrefetchScalarGridSpec(
            num_scalar_prefetch=1,
            grid=(t // (2 * tm),),
            in_specs=[row_spec(d), row_spec(EXPERT_TOPK), row_spec(p.shape[1]),
                      pl.BlockSpec(memory_space=pl.ANY),
                      _const_spec((1, d)), _const_spec((d, d)), _const_spec((p.shape[1], d)), _const_spec((1, d))],
            out_specs=row_spec(d),
            scratch_shapes=[pltpu.VMEM((EXPERT_TOPK, tm, d), F32), pltpu.VMEM((EXPERT_TOPK, tm, d), F32),
                            pltpu.SemaphoreType.DMA((2,))],
        ),
        out_shape=jax.ShapeDtypeStruct((t, d), F32),
        compiler_params=_cparams(("arbitrary",)),
        name="combine_ple",
    )(dest, x1, cw_tok, p, ys, g_ple.reshape(1, d), w_ple_gate.astype(BF16), w_ple_proj.astype(BF16),
      g_final.reshape(1, d))


def _layer(x, p, g_attn, w_in, w_br_a, w_br_b, w_out, g_ffn, w_grp, b_grp, w_rt, b_rt,
           w_gate, w_up, w_down, g_ple, w_ple_gate, w_ple_proj, g_final):
    b, s, d = x.shape
    t = b * s
    (qa_t, va_t, qb_t, vb_t, iq_t, iw_t, ka, kb, ik, ga, gb, km) = _in_projection(x, g_attn, w_in)
    o_a = _moba_attention(qa_t, ka, va_t, km.reshape(b, s // MOBA_BLOCK, MIXER_W))
    o_b = _dsa_attention(iq_t, ik, iw_t, qb_t, kb, vb_t)
    x1, h2, ids, cw, rank, counts = _merge_and_route(
        o_a.reshape(t, MIXER_W), o_b.reshape(t, MIXER_W), ga.reshape(t, d), gb.reshape(t, d),
        x.reshape(t, d), w_br_a, w_br_b, w_out, g_ffn, w_grp, b_grp, w_rt, b_rt)

    counts = counts[:, 0].astype(I32)
    padded = ((counts + EXPERT_ROWS - 1) // EXPERT_ROWS) * EXPERT_ROWS
    pend = jnp.cumsum(padded)
    pstart = pend - padded
    n_blocks = -(-(t * EXPERT_TOPK) // EXPERT_ROWS) + N_EXPERTS
    experts = jnp.arange(N_EXPERTS, dtype=I32)
    dest = jnp.sum(jnp.where(ids[..., None] == experts, pstart, 0), axis=-1) + rank
    block_row0 = jnp.arange(n_blocks, dtype=I32) * EXPERT_ROWS
    blk_exp = jnp.minimum(jnp.sum((pend[None, :] <= block_row0[:, None]).astype(I32), axis=1), N_EXPERTS - 1)
    n_used = (pend[-1:] // EXPERT_ROWS).astype(I32)

    route_tile = dest.shape[2]
    dest = dest.reshape(-1)
    is_last_of_expert = jnp.any(pend[None, :] == (block_row0 + EXPERT_ROWS)[:, None], axis=1)
    pad_block = (is_last_of_expert | (block_row0 >= pend[-1])).astype(I32)
    xs = _dispatch(dest, pad_block, h2, route_tile)
    ys = _experts(blk_exp, n_used, xs, w_gate, w_up, w_down)
    cw_tok = jnp.swapaxes(cw, 1, 2).reshape(t, EXPERT_TOPK)
    out = _combine_ple(dest, x1, cw_tok, p.reshape(t, p.shape[-1]), ys, g_ple, w_ple_gate, w_ple_proj, g_final,
                       route_tile)
    return out.reshape(b, s, d)


def kernel(x, p, g_attn, w_in, w_br_a, w_br_b, w_out, g_ffn, w_grp, b_grp, w_rt, b_rt, w_gate, w_up, w_down, g_ple, w_ple_gate, w_ple_proj, g_final):
    depth = w_in.shape[0]
    assert depth == 1, "the final RMSNorm is fused into the last layer's kernel"
    i = 0
    return _layer(x, p[i], g_attn[i], w_in[i], w_br_a[i], w_br_b[i], w_out[i], g_ffn[i], w_grp[i], b_grp[i],
                  w_rt[i], b_rt[i], w_gate[i], w_up[i], w_down[i], g_ple[i], w_ple_gate[i], w_ple_proj[i], g_final)
```

```python
import functools

import jax
import jax.numpy as jnp
from jax import lax
from jax.experimental import pallas as pl
from jax.experimental.pallas import tpu as pltpu

F32 = jnp.float32
BF16 = jnp.bfloat16
I32 = jnp.int32

HEAD_DIM = 64
N_HEADS = 8
ROT_DIM = HEAD_DIM // 4
ROT_HALF = ROT_DIM // 2
ROPE_THETA = 500000.0
MOBA_BLOCK = 256
MOBA_TOPK = 3
IDX_HEADS = 8
IDX_DIM = 64
DSA_TOPK_MAX = 256
N_GROUPS = 4
EXPERTS_PER_GROUP = 8
N_EXPERTS = N_GROUPS * EXPERTS_PER_GROUP
D_EXPERT = 512
PLE_DIM = 256
RMS_EPS = 1e-6
MIXER_W = N_HEADS * HEAD_DIM

Q_TILE = 256
LANES = 128
SUBLANES = 8
VMEM_LIMIT = 56 * 1024 * 1024

NT_DIMS = (((1,), (1,)), ((), ()))
TN_DIMS = (((0,), (0,)), ((), ()))


def _cparams(sem):
    return pltpu.CompilerParams(dimension_semantics=sem, vmem_limit_bytes=VMEM_LIMIT)


def _const_spec(shape):
    nd = len(shape)
    return pl.BlockSpec(shape, lambda *_: (0,) * nd, pipeline_mode=pl.Buffered(1))


def _rope_feature_major(z, cos_t, sin_t):
    tm = z.shape[1]
    z3 = z.reshape(N_HEADS, HEAD_DIM, tm)
    x1 = z3[:, 0:ROT_HALF, :]
    x2 = z3[:, ROT_HALF:ROT_DIM, :]
    o1 = x1 * cos_t - x2 * sin_t
    o2 = x2 * cos_t + x1 * sin_t
    return jnp.concatenate([o1, o2, z3[:, ROT_DIM:, :]], axis=1).reshape(N_HEADS * HEAD_DIM, tm)


def _rope_token_major(z, c_tab, s_lo, s_hi):
    up = pltpu.roll(z, LANES - ROT_HALF, 1)
    dn = pltpu.roll(z, ROT_HALF, 1)
    return z * c_tab + up * s_lo + dn * s_hi


def _store_blocked(ref, z):
    for c in range(z.shape[1] // Q_TILE):
        ref[0, c] = z[:, c * Q_TILE:(c + 1) * Q_TILE]


def _inproj_kernel(x_ref, g_ref, wf_ref, wiw_ref, wt_ref, wg_ref, cos_t_ref, sin_t_ref,
                   ctab_ref, slo_ref, shi_ref,
                   qa_t_ref, va_t_ref, qb_t_ref, vb_t_ref, iq_t_ref, iw_t_ref,
                   ka_ref, kb_ref, ik_ref, ga_ref, gb_ref, km_ref, *, w_scale):
    x = x_ref[0]
    ms = jnp.mean(x * x, axis=-1, keepdims=True)
    h = ((x * lax.rsqrt(ms + RMS_EPS)) * g_ref[...]).astype(BF16)
    cos_t = cos_t_ref[...]
    sin_t = sin_t_ref[...]
    q_scale = HEAD_DIM ** -0.5

    fm_outs = ((qa_t_ref, True, q_scale), (va_t_ref, False, 1.0), (qb_t_ref, True, q_scale),
               (vb_t_ref, False, 1.0), (iq_t_ref, True, IDX_DIM ** -0.5))
    for i, (ref, rope, scale) in enumerate(fm_outs):
        z = lax.dot_general(wf_ref[i * MIXER_W:(i + 1) * MIXER_W, :], h, NT_DIMS,
                            preferred_element_type=F32)
        if rope:
            z = _rope_feature_major(z, cos_t, sin_t)
        if scale != 1.0:
            z = z * scale
        _store_blocked(ref, z.astype(BF16))

    iw = lax.dot_general(wiw_ref[...], h, NT_DIMS, preferred_element_type=F32)
    _store_blocked(iw_t_ref, iw * w_scale)

    zt = jnp.dot(h, wt_ref[...], preferred_element_type=F32)
    ctab, slo, shi = ctab_ref[...], slo_ref[...], shi_ref[...]
    n_grp = zt.shape[1] // LANES
    roped = [_rope_token_major(zt[:, j * LANES:(j + 1) * LANES], ctab, slo, shi) for j in range(n_grp)]
    per_mixer = MIXER_W // LANES
    ka = jnp.concatenate(roped[:per_mixer], axis=1)
    kb = jnp.concatenate(roped[per_mixer:2 * per_mixer], axis=1)
    ka_ref[0] = ka.astype(BF16)
    kb_ref[0] = kb.astype(BF16)
    ik_ref[0] = roped[2 * per_mixer][:, :IDX_DIM].astype(BF16)
    tm = ka.shape[0]
    km_ref[0] = jnp.mean(ka.reshape(tm // MOBA_BLOCK, MOBA_BLOCK, MIXER_W), axis=1, keepdims=True)

    zg = jnp.dot(h, wg_ref[...], preferred_element_type=F32)
    d_model = zg.shape[1] // 2
    ga_ref[0] = zg[:, :d_model]
    gb_ref[0] = zg[:, d_model:]


def _rope_tables(seq):
    inv = 1.0 / (ROPE_THETA ** (jnp.arange(0, ROT_DIM, 2, dtype=F32) / ROT_DIM))
    ang = jnp.arange(seq, dtype=F32)[:, None] * inv[None, :]
    cos, sin = jnp.cos(ang), jnp.sin(ang)
    d = jnp.arange(LANES) % HEAD_DIM
    lo = d < ROT_HALF
    hi = (d >= ROT_HALF) & (d < ROT_DIM)
    f = d % ROT_HALF
    cos_l, sin_l = cos[:, f], sin[:, f]
    ctab = jnp.where(lo | hi, cos_l, 1.0)
    slo = jnp.where(lo, -sin_l, 0.0)
    shi = jnp.where(hi, sin_l, 0.0)
    return cos.T, sin.T, ctab, slo, shi


def _in_projection(x, g_attn, w_in, tm=512):
    b, s, d = x.shape
    splits = (MIXER_W,) * 6 + (IDX_HEADS * IDX_DIM, IDX_DIM, IDX_HEADS, d, d)
    offs = [0]
    for w in splits:
        offs.append(offs[-1] + w)
    wqa, wka, wva, wqb, wkb, wvb, wiq, wik, wiw, wga, wgb = (
        w_in[:, offs[i]:offs[i + 1]] for i in range(len(splits)))
    wf = jnp.concatenate([wqa, wva, wqb, wvb, wiq], axis=1).T.astype(BF16)
    wiw_t = wiw.T.astype(BF16)
    wt = jnp.concatenate([wka, wkb, wik, jnp.zeros((d, LANES - IDX_DIM), w_in.dtype)], axis=1).astype(BF16)
    wg = jnp.concatenate([wga, wgb], axis=1).astype(BF16)
    cos_t, sin_t, ctab, slo, shi = _rope_tables(s)
    n_blk = s // MOBA_BLOCK
    fm_shape = jax.ShapeDtypeStruct((b, s // Q_TILE, MIXER_W, Q_TILE), BF16)
    tok_shape = jax.ShapeDtypeStruct((b, s, MIXER_W), BF16)
    fm_spec = pl.BlockSpec((1, tm // Q_TILE, MIXER_W, Q_TILE), lambda bi, ti: (bi, ti, 0, 0))
    tok_spec = pl.BlockSpec((1, tm, MIXER_W), lambda bi, ti: (bi, ti, 0))
    gate_spec = pl.BlockSpec((1, tm, d), lambda bi, ti: (bi, ti, 0))
    outs = pl.pallas_call(
        functools.partial(_inproj_kernel, w_scale=IDX_HEADS ** -0.5),
        grid=(b, s // tm),
        in_specs=[
            pl.BlockSpec((1, tm, d), lambda bi, ti: (bi, ti, 0)),
            _const_spec((1, d)),
            _const_spec(wf.shape), _const_spec(wiw_t.shape), _const_spec(wt.shape), _const_spec(wg.shape),
            pl.BlockSpec((ROT_HALF, tm), lambda bi, ti: (0, ti)),
            pl.BlockSpec((ROT_HALF, tm), lambda bi, ti: (0, ti)),
            pl.BlockSpec((tm, LANES), lambda bi, ti: (ti, 0)),
            pl.BlockSpec((tm, LANES), lambda bi, ti: (ti, 0)),
            pl.BlockSpec((tm, LANES), lambda bi, ti: (ti, 0)),
        ],
        out_specs=[
            fm_spec, fm_spec, fm_spec, fm_spec, fm_spec,
            pl.BlockSpec((1, tm // Q_TILE, IDX_HEADS, Q_TILE), lambda bi, ti: (bi, ti, 0, 0)),
            tok_spec, tok_spec,
            pl.BlockSpec((1, tm, IDX_DIM), lambda bi, ti: (bi, ti, 0)),
            gate_spec, gate_spec,
            pl.BlockSpec((1, tm // MOBA_BLOCK, 1, MIXER_W), lambda bi, ti: (bi, ti, 0, 0)),
        ],
        out_shape=[
            fm_shape, fm_shape, fm_shape, fm_shape, fm_shape,
            jax.ShapeDtypeStruct((b, s // Q_TILE, IDX_HEADS, Q_TILE), F32),
            tok_shape, tok_shape,
            jax.ShapeDtypeStruct((b, s, IDX_DIM), BF16),
            jax.ShapeDtypeStruct((b, s, d), F32), jax.ShapeDtypeStruct((b, s, d), F32),
            jax.ShapeDtypeStruct((b, n_blk, 1, MIXER_W), F32),
        ],
        compiler_params=_cparams(("parallel", "parallel")),
        name="in_projection",
    )(x, g_attn.reshape(1, d), wf, wiw_t, wt, wg, cos_t, sin_t, ctab, slo, shi)
    return outs


MASKED = -1e30


HEADS_PER_GROUP = LANES // HEAD_DIM


def _one_head_of_pair(q_pair, hh):
    row = lax.broadcasted_iota(I32, q_pair.shape, 0)
    return jnp.where((row // HEAD_DIM) == hh, q_pair, jnp.zeros_like(q_pair))


def _head_lanes(h):
    g = h // HEADS_PER_GROUP
    return slice(g * LANES, (g + 1) * LANES)


def _head_rows(h):
    return slice(h * HEAD_DIM, (h + 1) * HEAD_DIM)


ATTN_GROUP = 2
ONES_ROWS = 16
ACC_ROWS = HEAD_DIM + ONES_ROWS


def _split_heads(q_all, qh_ref):
    for h in range(N_HEADS):
        qh_ref[h] = _one_head_of_pair(q_all[_head_lanes(h), :], h % HEADS_PER_GROUP)


def _weighted_values(s, m, v_t):
    p = jnp.exp(s - m).astype(BF16)
    lhs = jnp.concatenate([v_t, jnp.ones((ONES_ROWS, v_t.shape[1]), BF16)], axis=0)
    return jnp.dot(lhs, p, preferred_element_type=F32)


def _loop_in_groups(n, body, init, width=4):
    carry = lax.fori_loop(0, n // width, lambda k, c: body(width * k, width, c), init)
    done = (n // width) * width
    part = width // 2
    while part >= 1:
        has_part = ((n - done) // part) % 2 == 1
        carry = lax.cond(has_part, lambda c, done=done, part=part: body(done, part, c), lambda c: c, carry)
        done = done + jnp.where(has_part, part, 0)
        part //= 2
    return carry


def _store_head_output(o_ref, h, acc):
    o_ref[0, 0, _head_rows(h), :] = (acc[:HEAD_DIM, :] / acc[HEAD_DIM:HEAD_DIM + 1, :]).astype(BF16)


def _moba_kernel(q_t_ref, k_ref, v_t_ref, km_ref, o_ref, qh_ref, bias_ref, s_even_ref, s_odd_ref, *acc_refs):
    s_refs = (s_even_ref, s_odd_ref)
    i = pl.program_id(1)
    tq = q_t_ref.shape[3]
    n_blk = km_ref.shape[1]
    _split_heads(q_t_ref[0, 0], qh_ref)

    blk = lax.broadcasted_iota(I32, (n_blk, tq), 0)
    for h in range(N_HEADS):
        gate = jnp.dot(km_ref[0, :, _head_lanes(h)], qh_ref[h].astype(F32), preferred_element_type=F32)
        gate = jnp.where(blk < i, gate, -jnp.inf)
        keep = blk == i
        for _ in range(MOBA_TOPK):
            _, first = _first_index_of_max(gate, blk)
            taken = (blk == first) & (blk < i)
            keep = keep | taken
            gate = jnp.where(taken, -jnp.inf, gate)
        bias_ref[h] = jnp.where(keep, 0.0, MASKED)

    def key_rows(j):
        return pl.ds(pl.multiple_of(j * tq, tq), tq)

    def block_bias(h, j):
        return bias_ref[h, pl.ds(j, 1), :]

    causal = lax.broadcasted_iota(I32, (tq, tq), 0) <= lax.broadcasted_iota(I32, (tq, tq), 1)

    def scores(h, j):
        return jnp.dot(k_ref[0, key_rows(j), _head_lanes(h)], qh_ref[h], preferred_element_type=F32)

    def own_block_scores(g):
        own_max = []
        for gi, h in _group_heads(g):
            s = jnp.where(causal, scores(h, i), MASKED)
            s_refs[g % 2][gi, key_rows(i), :] = s
            own_max.append(jnp.max(s, axis=0, keepdims=True))
        return tuple(own_max)

    def past_scores(g, j, mx):
        for gi, h in _group_heads(g):
            s = scores(h, j)
            s_refs[g % 2][gi, key_rows(j), :] = s
            mx[gi] = jnp.maximum(mx[gi], jnp.max(s, axis=0, keepdims=True) + block_bias(h, j))

    def weighted(g, gi, h, j, mx):
        return _weighted_values(s_refs[g % 2][gi, key_rows(j), :], mx[gi] - block_bias(h, j),
                                v_t_ref[0, j, _head_rows(h), :])

    def finish(g, mx):
        for gi, h in _group_heads(g):
            _store_head_output(o_ref, h, acc_refs[gi][...] + weighted(g, gi, h, i, mx))

    _attention_pipeline(i, own_block_scores, past_scores, weighted, finish, acc_refs)


def _group_heads(g):
    return list(enumerate(range(g * ATTN_GROUP, (g + 1) * ATTN_GROUP)))


def _attention_pipeline(n, first_max, score_step, weighted, finish, acc_refs):
    n_groups = N_HEADS // ATTN_GROUP
    prev_mx = None
    for g in range(n_groups + 1):
        scoring, weighting = g < n_groups, g > 0
        if weighting:
            for acc_ref in acc_refs:
                acc_ref[...] = jnp.zeros(acc_ref.shape, F32)

        def body(j0, count, mx, g=g, scoring=scoring, weighting=weighting, prev_mx=prev_mx):
            mx = list(mx)
            partial = [0.0] * ATTN_GROUP
            for d in range(count):
                if scoring:
                    score_step(g, j0 + d, mx)
                if weighting:
                    for gi, h in _group_heads(g - 1):
                        partial[gi] = partial[gi] + weighted(g - 1, gi, h, j0 + d, prev_mx)
            if weighting:
                for gi, _ in _group_heads(g - 1):
                    acc_refs[gi][...] += partial[gi]
            return tuple(mx)

        mx = _loop_in_groups(n, body, first_max(g) if scoring else ())
        if weighting:
            finish(g - 1, prev_mx)
        prev_mx = mx


def _attention_scratch(s, tq):
    return ([pltpu.VMEM((N_HEADS, LANES, tq), BF16)],
            [pltpu.VMEM((ATTN_GROUP, s, tq), F32)] * 2 + [pltpu.VMEM((ACC_ROWS, tq), F32)] * ATTN_GROUP)


def _resident_spec(shape):
    nd = len(shape)
    return pl.BlockSpec(shape, lambda bi, i: (bi,) + (0,) * (nd - 1), pipeline_mode=pl.Buffered(1))


def _moba_attention(qa_t, ka, va_t, km):
    b, n_blk, w, tq = qa_t.shape
    s = n_blk * tq
    qh, work = _attention_scratch(s, tq)
    return pl.pallas_call(
        _moba_kernel,
        grid=(b, n_blk),
        in_specs=[
            pl.BlockSpec((1, 1, w, tq), lambda bi, i: (bi, i, 0, 0)),
            _resident_spec((1, s, w)),
            _resident_spec((1, n_blk, w, tq)),
            _resident_spec((1, n_blk, w)),
        ],
        out_specs=pl.BlockSpec((1, 1, w, tq), lambda bi, i: (bi, i, 0, 0)),
        out_shape=jax.ShapeDtypeStruct((b, n_blk, w, tq), BF16),
        scratch_shapes=qh + [pltpu.VMEM((N_HEADS, n_blk, tq), F32)] + work,
        compiler_params=_cparams(("parallel", "arbitrary")),
        name="moba_attention",
    )(qa_t, ka, va_t, km)


INT_MIN = -2 ** 31


def _sortable_key(s):
    bits = lax.bitcast_convert_type(s, I32)
    return bits ^ ((bits >> 31) & 0x7FFFFFFF)


def _column_count(mask):
    tk, tq = mask.shape
    return jnp.sum(mask.astype(I32).reshape(tk // SUBLANES, SUBLANES, tq), axis=0)


I16 = jnp.int16
HALF_BITS = 16
I16_MIN = -2 ** (HALF_BITS - 1)
PACKED_ROWS = 2 * SUBLANES
SEARCH_GROUP = 4


def _packed_column_count(mask):
    ones = mask.astype(I16)
    parts = [ones[r:r + PACKED_ROWS, :] for r in range(0, ones.shape[0], PACKED_ROWS)]
    while len(parts) > 1:
        parts = [a + b for a, b in zip(parts[::2], parts[1::2])] + parts[len(parts) - len(parts) % 2:]
    return parts[0]


def _packed_total(count16):
    return jnp.sum(count16.astype(I32), axis=0, keepdims=True)


def _dsa_kernel(iq_t_ref, ik_ref, iw_t_ref, q_t_ref, k_ref, v_t_ref, o_ref,
                key_ref, hi_ref, lo_ref, lo2_ref, bias_ref, qh_ref, s_even_ref, s_odd_ref, *acc_refs,
                topk, index_bits):
    s_refs = (s_even_ref, s_odd_ref)
    t = pl.program_id(1)
    tq = q_t_ref.shape[3]
    tk = tq
    n_chunks = t + 1
    qpos = t * tq + lax.broadcasted_iota(I32, (1, tq), 1)
    krow = lax.broadcasted_iota(I32, (tk, tq), 0)

    def chunk_rows(c):
        return pl.ds(pl.multiple_of(c * tk, tk), tk)

    def total(count8):
        return jnp.sum(count8, axis=0, keepdims=True)

    def count_over_chunks(pred):
        def body(c, cnt):
            return cnt + _column_count(pred(key_ref[chunk_rows(c), :], c * tk + krow))
        return total(lax.fori_loop(0, n_chunks, body, jnp.zeros((SUBLANES, tq), I32)))

    iq = iq_t_ref[0, 0]
    iw = iw_t_ref[0, 0]

    def score_chunks(c0, count, carry):
        for c in (c0 + d for d in range(count)):
            ik_c = ik_ref[0, chunk_rows(c), :]
            score = jnp.zeros((tk, tq), F32)
            for h in range(IDX_HEADS):
                rel = jnp.dot(ik_c, iq[h * IDX_DIM:(h + 1) * IDX_DIM, :], preferred_element_type=F32)
                score = score + jnp.maximum(rel, 0.0) * iw[h:h + 1, :]
            key = jnp.where(c * tk + krow <= qpos, _sortable_key(score), INT_MIN)
            key_ref[chunk_rows(c), :] = key
            hi_ref[chunk_rows(c), :] = (key >> HALF_BITS).astype(I16)
            lo_ref[chunk_rows(c), :] = key.astype(I16) ^ jnp.int16(I16_MIN)
        return carry

    _loop_in_groups(n_chunks, score_chunks, 0)

    n_groups = (n_chunks + SEARCH_GROUP - 1) // SEARCH_GROUP
    for d in range(SEARCH_GROUP - 1):
        @pl.when(n_chunks + d < n_groups * SEARCH_GROUP)
        def _(d=d):
            lowest = jnp.full((tk, tq), I16_MIN, I16)
            hi_ref[chunk_rows(n_chunks + d), :] = lowest
            lo_ref[chunk_rows(n_chunks + d), :] = lowest

    def packed_counts(n_counts, chunk_masks):
        def body(g, cnts):
            cnts = list(cnts)
            for c in (g * SEARCH_GROUP + d for d in range(SEARCH_GROUP)):
                for i, mask in enumerate(chunk_masks(c)):
                    cnts[i] = cnts[i] + _packed_column_count(mask)
            return tuple(cnts)
        zero = jnp.zeros((PACKED_ROWS, tq), I16)
        return [_packed_total(c) for c in lax.fori_loop(0, n_groups, body, (zero,) * n_counts)]

    def kth_largest_half(half_ref, kth):
        def half_bit(b, prefix):
            cand = prefix | jnp.left_shift(jnp.int32(1), HALF_BITS - 1 - b)
            cand_half = (cand + I16_MIN).astype(I16)
            n_ge, = packed_counts(1, lambda c: [half_ref[chunk_rows(c), :] >= cand_half])
            return jnp.where(n_ge >= kth, cand, prefix)

        return lax.fori_loop(0, HALF_BITS, half_bit, jnp.zeros((1, tq), I32)) + I16_MIN

    thr_hi = kth_largest_half(hi_ref, topk)
    thr_hi_half = thr_hi.astype(I16)

    def above_hi(c):
        hi = hi_ref[chunk_rows(c), :]
        lo2_ref[chunk_rows(c), :] = jnp.where(hi == thr_hi_half, lo_ref[chunk_rows(c), :], jnp.int16(I16_MIN))
        return [hi > thr_hi_half]

    n_gt_hi, = packed_counts(1, above_hi)
    thr_lo = kth_largest_half(lo2_ref, topk - n_gt_hi)
    thr_lo_half = thr_lo.astype(I16)
    thr = jnp.left_shift(thr_hi, HALF_BITS) | (thr_lo - I16_MIN)

    n_gt_lo, n_eq = packed_counts(2, lambda c: [
        lo2_ref[chunk_rows(c), :] > thr_lo_half,
        (hi_ref[chunk_rows(c), :] == thr_hi_half) & (lo_ref[chunk_rows(c), :] == thr_lo_half)])
    need = topk - (n_gt_hi + n_gt_lo)
    has_surplus = jnp.any((n_eq > need) & (thr != INT_MIN))

    def last_tie_index():
        def index_bit(b, prefix):
            cand = prefix | jnp.left_shift(jnp.int32(1), index_bits - 1 - b)
            n_before = count_over_chunks(lambda keys, kpos: (keys == thr) & (kpos < cand))
            return jnp.where(n_before < need, cand, prefix)
        return lax.fori_loop(0, index_bits, index_bit, jnp.zeros((1, tq), I32))

    def bias_with_ties():
        tie_end = last_tie_index()

        def bias_chunk(c, carry):
            keys = key_ref[chunk_rows(c), :]
            kpos = c * tk + krow
            chosen = ((keys > thr) | ((keys == thr) & (kpos <= tie_end))) & (kpos <= qpos)
            bias_ref[chunk_rows(c), :] = jnp.where(chosen, 0.0, MASKED)
            return carry

        lax.fori_loop(0, n_chunks, bias_chunk, 0)

    def bias_without_ties():
        floor = jnp.maximum(thr, INT_MIN + 1)

        def bias_chunks(c0, count, carry):
            for c in (c0 + d for d in range(count)):
                bias_ref[chunk_rows(c), :] = jnp.where(key_ref[chunk_rows(c), :] >= floor, 0.0, MASKED)
            return carry

        _loop_in_groups(n_chunks, bias_chunks, 0)

    lax.cond(has_surplus, bias_with_ties, bias_without_ties)

    _split_heads(q_t_ref[0, 0], qh_ref)

    def masked_scores(g, c, mx):
        for gi, h in _group_heads(g):
            s = jnp.dot(k_ref[0, chunk_rows(c), _head_lanes(h)], qh_ref[h], preferred_element_type=F32)
            s = s + bias_ref[chunk_rows(c), :]
            s_refs[g % 2][gi, chunk_rows(c), :] = s
            mx[gi] = jnp.maximum(mx[gi], jnp.max(s, axis=0, keepdims=True))

    def weighted(g, gi, h, c, mx):
        return _weighted_values(s_refs[g % 2][gi, chunk_rows(c), :], mx[gi], v_t_ref[0, c, _head_rows(h), :])

    def finish(g, mx):
        for gi, h in _group_heads(g):
            _store_head_output(o_ref, h, acc_refs[gi][...])

    _attention_pipeline(n_chunks, lambda g: (jnp.full((1, tq), MASKED, F32),) * ATTN_GROUP,
                        masked_scores, weighted, finish, acc_refs)


def _dsa_attention(iq_t, ik, iw_t, qb_t, kb, vb_t):
    b, n_blk, w, tq = qb_t.shape
    s = n_blk * tq
    topk = min(DSA_TOPK_MAX, s // 4)
    assert n_blk % SEARCH_GROUP == 0, "the packed search pads its chunk range to whole groups"
    qh, work = _attention_scratch(s, tq)
    tile_spec = lambda rows: pl.BlockSpec((1, 1, rows, tq), lambda bi, ti: (bi, ti, 0, 0))
    return pl.pallas_call(
        functools.partial(_dsa_kernel, topk=topk, index_bits=max(1, (s - 1).bit_length())),
        grid=(b, n_blk),
        in_specs=[
            tile_spec(IDX_HEADS * IDX_DIM),
            _resident_spec((1, s, IDX_DIM)),
            tile_spec(IDX_HEADS),
            tile_spec(w),
            _resident_spec((1, s, w)),
            _resident_spec((1, n_blk, w, tq)),
        ],
        out_specs=pl.BlockSpec((1, 1, w, tq), lambda bi, ti: (bi, ti, 0, 0)),
        out_shape=jax.ShapeDtypeStruct((b, n_blk, w, tq), BF16),
        scratch_shapes=[pltpu.VMEM((s, tq), I32)] + [pltpu.VMEM((s, tq), I16)] * 3
        + [pltpu.VMEM((s, tq), F32)] + qh + work,
        compiler_params=_cparams(("parallel", "arbitrary")),
        name="dsa_attention",
    )(iq_t, ik, iw_t, qb_t, kb, vb_t)


EXPERT_ROW0 = SUBLANES
ROUTER_ROWS = EXPERT_ROW0 + N_EXPERTS
EXPERT_TOPK = 2


def _rms_norm(x, g):
    ms = jnp.mean(x * x, axis=-1, keepdims=True)
    return (x * lax.rsqrt(ms + RMS_EPS)) * g


def _first_index_of_max(vals, idx):
    top = jnp.max(vals, axis=0, keepdims=True)
    first = jnp.min(jnp.where(vals == top, idx, vals.shape[0]), axis=0, keepdims=True)
    return top, first


def _merge_router_kernel(oa_ref, ob_ref, ga_ref, gb_ref, x_ref, wa_ref, wb_ref, wo_ref, g_ref,
                         wr_ref, br_ref, x1_ref, h2_ref, ids_ref, cw_ref, rank_ref, cnt_ref):
    @pl.when(pl.program_id(0) == 0)
    def _():
        cnt_ref[...] = jnp.zeros_like(cnt_ref)

    def branch(o_t_ref, w_ref):
        return jnp.concatenate([lax.dot_general(o_t_ref[c], w_ref[...], TN_DIMS, preferred_element_type=F32)
                                for c in range(o_t_ref.shape[0])], axis=0)

    a = branch(oa_ref, wa_ref)
    b = branch(ob_ref, wb_ref)
    merged = jax.nn.sigmoid(ga_ref[...]) * a + jax.nn.sigmoid(gb_ref[...]) * b
    x1 = x_ref[...] + jnp.dot(merged.astype(BF16), wo_ref[...], preferred_element_type=F32)
    x1_ref[...] = x1
    h2 = _rms_norm(x1, g_ref[...])
    h2_ref[...] = h2
    tm = h2.shape[0]

    logits = lax.dot_general(wr_ref[...], h2, NT_DIMS, preferred_element_type=F32,
                             precision=lax.Precision.HIGHEST) + br_ref[...]
    grp = logits[0:N_GROUPS, :]
    g_top, g_idx = _first_index_of_max(grp, lax.broadcasted_iota(I32, grp.shape, 0))
    p_grp = 1.0 / jnp.sum(jnp.exp(grp - g_top), axis=0, keepdims=True)
    in_grp = logits[EXPERT_ROW0:EXPERT_ROW0 + EXPERTS_PER_GROUP, :]
    for gi in range(1, N_GROUPS):
        rows = slice(EXPERT_ROW0 + gi * EXPERTS_PER_GROUP, EXPERT_ROW0 + (gi + 1) * EXPERTS_PER_GROUP)
        in_grp = jnp.where(g_idx == gi, logits[rows, :], in_grp)
    e_iota = lax.broadcasted_iota(I32, in_grp.shape, 0)
    v0, i0 = _first_index_of_max(in_grp, e_iota)
    v1, i1 = _first_index_of_max(jnp.where(e_iota == i0, -jnp.inf, in_grp), e_iota)
    e1 = jnp.exp(v1 - v0)
    denom = 1.0 + e1
    ids = jnp.concatenate([g_idx * EXPERTS_PER_GROUP + i0, g_idx * EXPERTS_PER_GROUP + i1], axis=0)
    ids_ref[0] = ids
    cw_ref[0] = jnp.concatenate([p_grp * (1.0 / denom), p_grp * (e1 / denom)], axis=0)

    before = (lax.broadcasted_iota(I32, (tm, tm), 0) < lax.broadcasted_iota(I32, (tm, tm), 1)).astype(BF16)
    expert = lax.broadcasted_iota(I32, (N_EXPERTS, tm), 0)
    ranks = []
    for slot in range(EXPERT_TOPK):
        onehot = expert == ids[slot:slot + 1, :]
        seen = jnp.dot(onehot.astype(BF16), before, preferred_element_type=F32) + cnt_ref[...]
        ranks.append(jnp.sum(jnp.where(onehot, seen, 0.0), axis=0, keepdims=True))
        cnt_ref[...] += jnp.sum(onehot.astype(F32), axis=1, keepdims=True)
    rank_ref[0] = jnp.concatenate(ranks, axis=0).astype(I32)


def _merge_and_route(o_a, o_b, ga, gb, x, w_br_a, w_br_b, w_out, g_ffn, w_grp, b_grp, w_rt, b_rt, tm=512):
    t, d = x.shape
    n_tiles = t // tm
    wr = jnp.zeros((ROUTER_ROWS, d), F32).at[0:N_GROUPS].set(w_grp.T).at[EXPERT_ROW0:].set(w_rt.T)
    br = jnp.zeros((ROUTER_ROWS, 1), F32).at[0:N_GROUPS, 0].set(b_grp).at[EXPERT_ROW0:, 0].set(b_rt)
    row_spec = lambda w: pl.BlockSpec((tm, w), lambda i: (i, 0))
    mixer_spec = pl.BlockSpec((tm // Q_TILE, MIXER_W, Q_TILE), lambda i: (i, 0, 0))
    slot_spec = pl.BlockSpec((1, EXPERT_TOPK, tm), lambda i: (i, 0, 0))
    slot_shape = lambda dt: jax.ShapeDtypeStruct((n_tiles, EXPERT_TOPK, tm), dt)
    return pl.pallas_call(
        _merge_router_kernel,
        grid=(n_tiles,),
        in_specs=[mixer_spec, mixer_spec, row_spec(d), row_spec(d), row_spec(d),
                  _const_spec((MIXER_W, d)), _const_spec((MIXER_W, d)), _const_spec((d, d)),
                  _const_spec((1, d)), _const_spec((ROUTER_ROWS, d)), _const_spec((ROUTER_ROWS, 1))],
        out_specs=[row_spec(d), row_spec(d), slot_spec, slot_spec, slot_spec,
                   pl.BlockSpec((N_EXPERTS, 1), lambda i: (0, 0))],
        out_shape=[jax.ShapeDtypeStruct((t, d), F32), jax.ShapeDtypeStruct((t, d), F32),
                   slot_shape(I32), slot_shape(F32), slot_shape(I32),
                   jax.ShapeDtypeStruct((N_EXPERTS, 1), F32)],
        compiler_params=_cparams(("arbitrary",)),
        name="merge_and_route",
    )(o_a, o_b, ga, gb, x, w_br_a.astype(BF16), w_br_b.astype(BF16), w_out.astype(BF16),
      g_ffn.reshape(1, d), wr, br)


EXPERT_ROWS = 512
DMA_ISSUE_UNROLL = 8


def _row(ref, r):
    return ref.at[pl.ds(r, 1), :]


def _tile_dest(dest_ref, tm, slot, r):
    return dest_ref[(pl.program_id(0) * EXPERT_TOPK + slot) * tm + r]


def _dispatch_kernel(dest_ref, pad_block_ref, h2_ref, xs_hbm, zero_ref, sem, zero_sem):
    tm = h2_ref.shape[0]

    @pl.when(pl.program_id(0) == 0)
    def _():
        zero_ref[...] = jnp.zeros(zero_ref.shape, F32)

        def block_copy(b):
            rows = pl.ds(pl.multiple_of(b * EXPERT_ROWS, EXPERT_ROWS), EXPERT_ROWS)
            return pltpu.make_async_copy(zero_ref, xs_hbm.at[rows, :], zero_sem)

        def start(b, n):
            @pl.when(pad_block_ref[b] != 0)
            def _():
                block_copy(b).start()
            return n + pad_block_ref[b]

        n_started = lax.fori_loop(0, pad_block_ref.shape[0], start, 0)

        def drain(k, carry):
            block_copy(0).wait()
            return carry

        lax.fori_loop(0, n_started, drain, 0)

    for r in range(tm):
        for slot in range(EXPERT_TOPK):
            pltpu.make_async_copy(_row(h2_ref, r), _row(xs_hbm, _tile_dest(dest_ref, tm, slot, r)), sem).start()
    for slot in range(EXPERT_TOPK):
        pltpu.make_async_copy(h2_ref, xs_hbm.at[pl.ds(0, tm), :], sem).wait()


def _dispatch(dest, pad_block, h2, tm):
    t, d = h2.shape
    n_rows = pad_block.shape[0] * EXPERT_ROWS
    return pl.pallas_call(
        _dispatch_kernel,
        grid_spec=pltpu.PrefetchScalarGridSpec(
            num_scalar_prefetch=2,
            grid=(t // tm,),
            in_specs=[pl.BlockSpec((tm, d), lambda i, dest, pad: (i, 0))],
            out_specs=pl.BlockSpec(memory_space=pl.ANY),
            scratch_shapes=[pltpu.VMEM((EXPERT_ROWS, d), F32), pltpu.SemaphoreType.DMA(()),
                            pltpu.SemaphoreType.DMA(())],
        ),
        out_shape=jax.ShapeDtypeStruct((n_rows, d), F32),
        compiler_params=_cparams(("arbitrary",)),
        name="moe_dispatch",
    )(dest, pad_block, h2)


def _expert_kernel(blk_exp_ref, n_used_ref, xs_ref, wg_ref, wu_ref, wd_ref, ys_ref, wg16, wu16, wd16):
    i = pl.program_id(0)
    used = i < n_used_ref[0]

    @pl.when((i == 0) | (blk_exp_ref[i] != blk_exp_ref[jnp.maximum(i - 1, 0)]))
    def _():
        wg16[...] = wg_ref[0].astype(BF16)
        wu16[...] = wu_ref[0].astype(BF16)
        wd16[...] = wd_ref[0].astype(BF16)

    @pl.when(used)
    def _():
        xb = xs_ref[...].astype(BF16)
        gate = jnp.dot(xb, wg16[...], preferred_element_type=F32)
        up = jnp.dot(xb, wu16[...], preferred_element_type=F32)
        hid = (gate * jax.nn.sigmoid(gate)) * up
        ys_ref[...] = jnp.dot(hid.astype(BF16), wd16[...], preferred_element_type=F32)

    @pl.when(jnp.logical_not(used))
    def _():
        ys_ref[...] = jnp.zeros_like(ys_ref)


def _experts(blk_exp, n_used, xs, w_gate, w_up, w_down):
    n_rows, d = xs.shape
    n_blocks = n_rows // EXPERT_ROWS
    expert_spec = lambda rows, cols: pl.BlockSpec((1, rows, cols), lambda i, be, nu: (be[i], 0, 0))
    return pl.pallas_call(
        _expert_kernel,
        grid_spec=pltpu.PrefetchScalarGridSpec(
            num_scalar_prefetch=2,
            grid=(n_blocks,),
            in_specs=[pl.BlockSpec((EXPERT_ROWS, d), lambda i, be, nu: (i, 0)),
                      expert_spec(d, D_EXPERT), expert_spec(d, D_EXPERT), expert_spec(D_EXPERT, d)],
            out_specs=pl.BlockSpec((EXPERT_ROWS, d), lambda i, be, nu: (i, 0)),
            scratch_shapes=[pltpu.VMEM((d, D_EXPERT), BF16), pltpu.VMEM((d, D_EXPERT), BF16),
                            pltpu.VMEM((D_EXPERT, d), BF16)],
        ),
        out_shape=jax.ShapeDtypeStruct((n_rows, d), F32),
        compiler_params=_cparams(("arbitrary",)),
        name="moe_experts",
    )(blk_exp, n_used, xs, w_gate, w_up, w_down)


def _combine_ple_kernel(dest_ref, x1_ref, cw_ref, p_ref, ys_hbm, g_ple_ref, wpg_ref, wpp_ref, g_fin_ref,
                        out_ref, y_even, y_odd, sem):
    tm = y_even.shape[1]
    i = pl.program_id(0)
    last_tile = 2 * pl.num_programs(0) - 1
    bufs = ((y_even, 0), (y_odd, 1))

    def start_row(tile, buf, sem_slot, r):
        for slot in range(EXPERT_TOPK):
            src = _row(ys_hbm, dest_ref[(tile * EXPERT_TOPK + slot) * tm + r])
            pltpu.make_async_copy(src, _row(buf.at[slot], r), sem.at[sem_slot]).start()

    def wait_tile(buf, sem_slot):
        for slot in range(EXPERT_TOPK):
            pltpu.make_async_copy(ys_hbm.at[pl.ds(0, tm), :], buf.at[slot], sem.at[sem_slot]).wait()

    def combine(half, buf):
        rows = slice(half * tm, (half + 1) * tm)
        cw = cw_ref[rows, :]
        x2 = x1_ref[rows, :] + (buf[0] * cw[:, 0:1] + buf[1] * cw[:, 1:2])
        h3 = _rms_norm(x2, g_ple_ref[...]).astype(BF16)
        gate = jax.nn.sigmoid(jnp.dot(h3, wpg_ref[...], preferred_element_type=F32))
        proj = jnp.dot(p_ref[rows, :].astype(BF16), wpp_ref[...], preferred_element_type=F32)
        out_ref[rows, :] = _rms_norm(x2 + gate * proj, g_fin_ref[...])

    @pl.when(i == 0)
    def _():
        def issue(r, carry):
            start_row(0, *bufs[0], r)
            return carry
        lax.fori_loop(0, tm, issue, 0, unroll=DMA_ISSUE_UNROLL)

    for half in range(2):
        buf, sem_slot = bufs[half]
        wait_tile(buf, sem_slot)
        combine(half, buf)
        next_tile = jnp.minimum(2 * i + half + 1, last_tile)
        for r in range(tm):
            start_row(next_tile, *bufs[1 - half], r)

    @pl.when(i == pl.num_programs(0) - 1)
    def _():
        wait_tile(*bufs[0])


def _combine_ple(dest, x1, cw_tok, p, ys, g_ple, w_ple_gate, w_ple_proj, g_final, tm):
    t, d = x1.shape
    assert t % (2 * tm) == 0
    row_spec = lambda w: pl.BlockSpec((2 * tm, w), lambda i, dest: (i, 0))
    return pl.pallas_call(
        _combine_ple_kernel,
        grid_spec=pltpu.PrefetchScalarGridSpec(
            num_scalar_prefetch=1,
            grid=(t // (2 * tm),),
            in_specs=[row_spec(d), row_spec(EXPERT_TOPK), row_spec(p.shape[1]),
                      pl.BlockSpec(memory_space=pl.ANY),
                      _const_spec((1, d)), _const_spec((d, d)), _const_spec((p.shape[1], d)), _const_spec((1, d))],
            out_specs=row_spec(d),
            scratch_shapes=[pltpu.VMEM((EXPERT_TOPK, tm, d), F32), pltpu.VMEM((EXPERT_TOPK, tm, d), F32),
                            pltpu.SemaphoreType.DMA((2,))],
        ),
        out_shape=jax.ShapeDtypeStruct((t, d), F32),
        compiler_params=_cparams(("arbitrary",)),
        name="combine_ple",
    )(dest, x1, cw_tok, p, ys, g_ple.reshape(1, d), w_ple_gate.astype(BF16), w_ple_proj.astype(BF16),
      g_final.reshape(1, d))


def _layer(x, p, g_attn, w_in, w_br_a, w_br_b, w_out, g_ffn, w_grp, b_grp, w_rt, b_rt,
           w_gate, w_up, w_down, g_ple, w_ple_gate, w_ple_proj, g_final):
    b, s, d = x.shape
    t = b * s
    (qa_t, va_t, qb_t, vb_t, iq_t, iw_t, ka, kb, ik, ga, gb, km) = _in_projection(x, g_attn, w_in)
    o_a = _moba_attention(qa_t, ka, va_t, km.reshape(b, s // MOBA_BLOCK, MIXER_W))
    o_b = _dsa_attention(iq_t, ik, iw_t, qb_t, kb, vb_t)
    x1, h2, ids, cw, rank, counts = _merge_and_route(
        o_a.reshape(t // Q_TILE, MIXER_W, Q_TILE), o_b.reshape(t // Q_TILE, MIXER_W, Q_TILE),
        ga.reshape(t, d), gb.reshape(t, d),
        x.reshape(t, d), w_br_a, w_br_b, w_out, g_ffn, w_grp, b_grp, w_rt, b_rt)

    counts = counts[:, 0].astype(I32)
    padded = ((counts + EXPERT_ROWS - 1) // EXPERT_ROWS) * EXPERT_ROWS
    pend = jnp.cumsum(padded)
    pstart = pend - padded
    n_blocks = -(-(t * EXPERT_TOPK) // EXPERT_ROWS) + N_EXPERTS
    experts = jnp.arange(N_EXPERTS, dtype=I32)
    dest = jnp.sum(jnp.where(ids[..., None] == experts, pstart, 0), axis=-1) + rank
    block_row0 = jnp.arange(n_blocks, dtype=I32) * EXPERT_ROWS
    blk_exp = jnp.minimum(jnp.sum((pend[None, :] <= block_row0[:, None]).astype(I32), axis=1), N_EXPERTS - 1)
    n_used = (pend[-1:] // EXPERT_ROWS).astype(I32)

    route_tile = dest.shape[2]
    dest = dest.reshape(-1)
    is_last_of_expert = jnp.any(pend[None, :] == (block_row0 + EXPERT_ROWS)[:, None], axis=1)
    pad_block = (is_last_of_expert | (block_row0 >= pend[-1])).astype(I32)
    xs = _dispatch(dest, pad_block, h2, route_tile)
    ys = _experts(blk_exp, n_used, xs, w_gate, w_up, w_down)
    cw_tok = jnp.swapaxes(cw, 1, 2).reshape(t, EXPERT_TOPK)
    out = _combine_ple(dest, x1, cw_tok, p.reshape(t, p.shape[-1]), ys, g_ple, w_ple_gate, w_ple_proj, g_final,
                       route_tile)
    return out.reshape(b, s, d)


def kernel(x, p, g_attn, w_in, w_br_a, w_br_b, w_out, g_ffn, w_grp, b_grp, w_rt, b_rt, w_gate, w_up, w_down, g_ple, w_ple_gate, w_ple_proj, g_final):
    depth = w_in.shape[0]
    assert depth == 1, "the final RMSNorm is fused into the last layer's kernel"
    i = 0
    return _layer(x, p[i], g_attn[i], w_in[i], w_br_a[i], w_br_b[i], w_out[i], g_ffn[i], w_grp[i], b_grp[i],
                  w_rt[i], b_rt[i], w_gate[i], w_up[i], w_down[i], g_ple[i], w_ple_gate[i], w_ple_proj[i], g_final)
```

```python
import functools

import jax
import jax.numpy as jnp
from jax import lax
from jax.experimental import pallas as pl
from jax.experimental.pallas import tpu as pltpu

F32 = jnp.float32
BF16 = jnp.bfloat16
I32 = jnp.int32

HEAD_DIM = 64
N_HEADS = 8
ROT_DIM = HEAD_DIM // 4
ROT_HALF = ROT_DIM // 2
ROPE_THETA = 500000.0
MOBA_BLOCK = 256
MOBA_TOPK = 3
IDX_HEADS = 8
IDX_DIM = 64
DSA_TOPK_MAX = 256
N_GROUPS = 4
EXPERTS_PER_GROUP = 8
N_EXPERTS = N_GROUPS * EXPERTS_PER_GROUP
D_EXPERT = 512
PLE_DIM = 256
RMS_EPS = 1e-6
MIXER_W = N_HEADS * HEAD_DIM

Q_TILE = 256
LANES = 128
SUBLANES = 8
VMEM_LIMIT = 56 * 1024 * 1024

NT_DIMS = (((1,), (1,)), ((), ()))
TN_DIMS = (((0,), (0,)), ((), ()))


def _cparams(sem):
    return pltpu.CompilerParams(dimension_semantics=sem, vmem_limit_bytes=VMEM_LIMIT)


def _const_spec(shape):
    nd = len(shape)
    return pl.BlockSpec(shape, lambda *_: (0,) * nd, pipeline_mode=pl.Buffered(1))


def _rope_feature_major(z, cos_t, sin_t):
    tm = z.shape[1]
    z3 = z.reshape(N_HEADS, HEAD_DIM, tm)
    x1 = z3[:, 0:ROT_HALF, :]
    x2 = z3[:, ROT_HALF:ROT_DIM, :]
    o1 = x1 * cos_t - x2 * sin_t
    o2 = x2 * cos_t + x1 * sin_t
    return jnp.concatenate([o1, o2, z3[:, ROT_DIM:, :]], axis=1).reshape(N_HEADS * HEAD_DIM, tm)


def _rope_token_major(z, c_tab, s_lo, s_hi):
    up = pltpu.roll(z, LANES - ROT_HALF, 1)
    dn = pltpu.roll(z, ROT_HALF, 1)
    return z * c_tab + up * s_lo + dn * s_hi


def _store_blocked(ref, z):
    for c in range(z.shape[1] // Q_TILE):
        ref[0, c] = z[:, c * Q_TILE:(c + 1) * Q_TILE]


def _inproj_kernel(x_ref, g_ref, wf_ref, wiw_ref, wt_ref, wg_ref, cos_t_ref, sin_t_ref,
                   ctab_ref, slo_ref, shi_ref,
                   qa_t_ref, va_t_ref, qb_t_ref, vb_t_ref, iq_t_ref, iw_t_ref,
                   ka_ref, kb_ref, ik_ref, ga_ref, gb_ref, km_ref, *, w_scale):
    x = x_ref[0]
    ms = jnp.mean(x * x, axis=-1, keepdims=True)
    h = ((x * lax.rsqrt(ms + RMS_EPS)) * g_ref[...]).astype(BF16)
    cos_t = cos_t_ref[...]
    sin_t = sin_t_ref[...]
    q_scale = HEAD_DIM ** -0.5

    fm_outs = ((qa_t_ref, True, q_scale), (va_t_ref, False, 1.0), (qb_t_ref, True, q_scale),
               (vb_t_ref, False, 1.0), (iq_t_ref, True, IDX_DIM ** -0.5))
    for i, (ref, rope, scale) in enumerate(fm_outs):
        z = lax.dot_general(wf_ref[i * MIXER_W:(i + 1) * MIXER_W, :], h, NT_DIMS,
                            preferred_element_type=F32)
        if rope:
            z = _rope_feature_major(z, cos_t, sin_t)
        if scale != 1.0:
            z = z * scale
        _store_blocked(ref, z.astype(BF16))

    iw = lax.dot_general(wiw_ref[...], h, NT_DIMS, preferred_element_type=F32)
    _store_blocked(iw_t_ref, iw * w_scale)

    zt = jnp.dot(h, wt_ref[...], preferred_element_type=F32)
    ctab, slo, shi = ctab_ref[...], slo_ref[...], shi_ref[...]
    n_grp = zt.shape[1] // LANES
    roped = [_rope_token_major(zt[:, j * LANES:(j + 1) * LANES], ctab, slo, shi) for j in range(n_grp)]
    per_mixer = MIXER_W // LANES
    ka = jnp.concatenate(roped[:per_mixer], axis=1)
    kb = jnp.concatenate(roped[per_mixer:2 * per_mixer], axis=1)
    ka_ref[0] = ka.astype(BF16)
    kb_ref[0] = kb.astype(BF16)
    ik_ref[0] = roped[2 * per_mixer][:, :IDX_DIM].astype(BF16)
    tm = ka.shape[0]
    km_ref[0] = jnp.mean(ka.reshape(tm // MOBA_BLOCK, MOBA_BLOCK, MIXER_W), axis=1, keepdims=True)

    zg = jnp.dot(h, wg_ref[...], preferred_element_type=F32)
    d_model = zg.shape[1] // 2
    ga_ref[0] = zg[:, :d_model]
    gb_ref[0] = zg[:, d_model:]


def _rope_tables(seq):
    inv = 1.0 / (ROPE_THETA ** (jnp.arange(0, ROT_DIM, 2, dtype=F32) / ROT_DIM))
    ang = jnp.arange(seq, dtype=F32)[:, None] * inv[None, :]
    cos, sin = jnp.cos(ang), jnp.sin(ang)
    d = jnp.arange(LANES) % HEAD_DIM
    lo = d < ROT_HALF
    hi = (d >= ROT_HALF) & (d < ROT_DIM)
    f = d % ROT_HALF
    cos_l, sin_l = cos[:, f], sin[:, f]
    ctab = jnp.where(lo | hi, cos_l, 1.0)
    slo = jnp.where(lo, -sin_l, 0.0)
    shi = jnp.where(hi, sin_l, 0.0)
    return cos.T, sin.T, ctab, slo, shi


def _in_projection(x, g_attn, w_in, tm=512):
    b, s, d = x.shape
    splits = (MIXER_W,) * 6 + (IDX_HEADS * IDX_DIM, IDX_DIM, IDX_HEADS, d, d)
    offs = [0]
    for w in splits:
        offs.append(offs[-1] + w)
    wqa, wka, wva, wqb, wkb, wvb, wiq, wik, wiw, wga, wgb = (
        w_in[:, offs[i]:offs[i + 1]] for i in range(len(splits)))
    wf = jnp.concatenate([wqa, wva, wqb, wvb, wiq], axis=1).T.astype(BF16)
    wiw_t = wiw.T.astype(BF16)
    wt = jnp.concatenate([wka, wkb, wik, jnp.zeros((d, LANES - IDX_DIM), w_in.dtype)], axis=1).astype(BF16)
    wg = jnp.concatenate([wga, wgb], axis=1).astype(BF16)
    cos_t, sin_t, ctab, slo, shi = _rope_tables(s)
    n_blk = s // MOBA_BLOCK
    fm_shape = jax.ShapeDtypeStruct((b, s // Q_TILE, MIXER_W, Q_TILE), BF16)
    tok_shape = jax.ShapeDtypeStruct((b, s, MIXER_W), BF16)
    fm_spec = pl.BlockSpec((1, tm // Q_TILE, MIXER_W, Q_TILE), lambda bi, ti: (bi, ti, 0, 0))
    tok_spec = pl.BlockSpec((1, tm, MIXER_W), lambda bi, ti: (bi, ti, 0))
    gate_spec = pl.BlockSpec((1, tm, d), lambda bi, ti: (bi, ti, 0))
    outs = pl.pallas_call(
        functools.partial(_inproj_kernel, w_scale=IDX_HEADS ** -0.5),
        grid=(b, s // tm),
        in_specs=[
            pl.BlockSpec((1, tm, d), lambda bi, ti: (bi, ti, 0)),
            _const_spec((1, d)),
            _const_spec(wf.shape), _const_spec(wiw_t.shape), _const_spec(wt.shape), _const_spec(wg.shape),
            pl.BlockSpec((ROT_HALF, tm), lambda bi, ti: (0, ti)),
            pl.BlockSpec((ROT_HALF, tm), lambda bi, ti: (0, ti)),
            pl.BlockSpec((tm, LANES), lambda bi, ti: (ti, 0)),
            pl.BlockSpec((tm, LANES), lambda bi, ti: (ti, 0)),
            pl.BlockSpec((tm, LANES), lambda bi, ti: (ti, 0)),
        ],
        out_specs=[
            fm_spec, fm_spec, fm_spec, fm_spec, fm_spec,
            pl.BlockSpec((1, tm // Q_TILE, IDX_HEADS, Q_TILE), lambda bi, ti: (bi, ti, 0, 0)),
            tok_spec, tok_spec,
            pl.BlockSpec((1, tm, IDX_DIM), lambda bi, ti: (bi, ti, 0)),
            gate_spec, gate_spec,
            pl.BlockSpec((1, tm // MOBA_BLOCK, 1, MIXER_W), lambda bi, ti: (bi, ti, 0, 0)),
        ],
        out_shape=[
            fm_shape, fm_shape, fm_shape, fm_shape, fm_shape,
            jax.ShapeDtypeStruct((b, s // Q_TILE, IDX_HEADS, Q_TILE), F32),
            tok_shape, tok_shape,
            jax.ShapeDtypeStruct((b, s, IDX_DIM), BF16),
            jax.ShapeDtypeStruct((b, s, d), F32), jax.ShapeDtypeStruct((b, s, d), F32),
            jax.ShapeDtypeStruct((b, n_blk, 1, MIXER_W), F32),
        ],
        compiler_params=_cparams(("parallel", "parallel")),
        name="in_projection",
    )(x, g_attn.reshape(1, d), wf, wiw_t, wt, wg, cos_t, sin_t, ctab, slo, shi)
    return outs


MASKED = -1e30


HEADS_PER_GROUP = LANES // HEAD_DIM


def _one_head_of_pair(q_pair, hh):
    row = lax.broadcasted_iota(I32, q_pair.shape, 0)
    return jnp.where((row // HEAD_DIM) == hh, q_pair, jnp.zeros_like(q_pair))


def _head_lanes(h):
    g = h // HEADS_PER_GROUP
    return slice(g * LANES, (g + 1) * LANES)


def _head_rows(h):
    return slice(h * HEAD_DIM, (h + 1) * HEAD_DIM)


ATTN_GROUP = 2
ONES_ROWS = 16
ACC_ROWS = HEAD_DIM + ONES_ROWS


def _split_heads(q_all, qh_ref):
    for h in range(N_HEADS):
        qh_ref[h] = _one_head_of_pair(q_all[_head_lanes(h), :], h % HEADS_PER_GROUP)


def _weighted_values(s, m, v_t):
    p = jnp.exp(s - m).astype(BF16)
    lhs = jnp.concatenate([v_t, jnp.ones((ONES_ROWS, v_t.shape[1]), BF16)], axis=0)
    return jnp.dot(lhs, p, preferred_element_type=F32)


def _loop_in_groups(n, body, init, width=4):
    carry = lax.fori_loop(0, n // width, lambda k, c: body(width * k, width, c), init)
    done = (n // width) * width
    part = width // 2
    while part >= 1:
        has_part = ((n - done) // part) % 2 == 1
        carry = lax.cond(has_part, lambda c, done=done, part=part: body(done, part, c), lambda c: c, carry)
        done = done + jnp.where(has_part, part, 0)
        part //= 2
    return carry


def _store_head_output(o_ref, h, acc):
    o_ref[0, 0, _head_rows(h), :] = (acc[:HEAD_DIM, :] / acc[HEAD_DIM:HEAD_DIM + 1, :]).astype(BF16)


def _moba_kernel(q_t_ref, k_ref, v_t_ref, km_ref, o_ref, qh_ref, bias_ref, s_even_ref, s_odd_ref, *acc_refs):
    s_refs = (s_even_ref, s_odd_ref)
    i = pl.program_id(1)
    tq = q_t_ref.shape[3]
    n_blk = km_ref.shape[1]
    _split_heads(q_t_ref[0, 0], qh_ref)

    blk = lax.broadcasted_iota(I32, (n_blk, tq), 0)
    for h in range(N_HEADS):
        gate = jnp.dot(km_ref[0, :, _head_lanes(h)], qh_ref[h].astype(F32), preferred_element_type=F32)
        gate = jnp.where(blk < i, gate, -jnp.inf)
        keep = blk == i
        for _ in range(MOBA_TOPK):
            _, first = _first_index_of_max(gate, blk)
            taken = (blk == first) & (blk < i)
            keep = keep | taken
            gate = jnp.where(taken, -jnp.inf, gate)
        bias_ref[h] = jnp.where(keep, 0.0, MASKED)

    def key_rows(j):
        return pl.ds(pl.multiple_of(j * tq, tq), tq)

    def block_bias(h, j):
        return bias_ref[h, pl.ds(j, 1), :]

    causal = lax.broadcasted_iota(I32, (tq, tq), 0) <= lax.broadcasted_iota(I32, (tq, tq), 1)

    def scores(h, j):
        return jnp.dot(k_ref[0, key_rows(j), _head_lanes(h)], qh_ref[h], preferred_element_type=F32)

    def own_block_scores(g):
        own_max = []
        for gi, h in _group_heads(g):
            s = jnp.where(causal, scores(h, i), MASKED)
            s_refs[g % 2][gi, key_rows(i), :] = s
            own_max.append(jnp.max(s, axis=0, keepdims=True))
        return tuple(own_max)

    def past_scores(g, j, mx):
        for gi, h in _group_heads(g):
            s = scores(h, j)
            s_refs[g % 2][gi, key_rows(j), :] = s
            mx[gi] = jnp.maximum(mx[gi], jnp.max(s, axis=0, keepdims=True) + block_bias(h, j))

    def weighted(g, gi, h, j, mx):
        return _weighted_values(s_refs[g % 2][gi, key_rows(j), :], mx[gi] - block_bias(h, j),
                                v_t_ref[0, j, _head_rows(h), :])

    def finish(g, mx):
        for gi, h in _group_heads(g):
            _store_head_output(o_ref, h, acc_refs[gi][...] + weighted(g, gi, h, i, mx))

    _attention_pipeline(i, own_block_scores, past_scores, weighted, finish, acc_refs)


def _group_heads(g):
    return list(enumerate(range(g * ATTN_GROUP, (g + 1) * ATTN_GROUP)))


def _attention_pipeline(n, first_max, score_step, weighted, finish, acc_refs):
    n_groups = N_HEADS // ATTN_GROUP
    prev_mx = None
    for g in range(n_groups + 1):
        scoring, weighting = g < n_groups, g > 0
        if weighting:
            for acc_ref in acc_refs:
                acc_ref[...] = jnp.zeros(acc_ref.shape, F32)

        def body(j0, count, mx, g=g, scoring=scoring, weighting=weighting, prev_mx=prev_mx):
            mx = list(mx)
            partial = [0.0] * ATTN_GROUP
            for d in range(count):
                if scoring:
                    score_step(g, j0 + d, mx)
                if weighting:
                    for gi, h in _group_heads(g - 1):
                        partial[gi] = partial[gi] + weighted(g - 1, gi, h, j0 + d, prev_mx)
            if weighting:
                for gi, _ in _group_heads(g - 1):
                    acc_refs[gi][...] += partial[gi]
            return tuple(mx)

        mx = _loop_in_groups(n, body, first_max(g) if scoring else ())
        if weighting:
            finish(g - 1, prev_mx)
        prev_mx = mx


def _attention_scratch(s, tq):
    return ([pltpu.VMEM((N_HEADS, LANES, tq), BF16)],
            [pltpu.VMEM((ATTN_GROUP, s, tq), F32)] * 2 + [pltpu.VMEM((ACC_ROWS, tq), F32)] * ATTN_GROUP)


def _resident_spec(shape):
    nd = len(shape)
    return pl.BlockSpec(shape, lambda bi, i: (bi,) + (0,) * (nd - 1), pipeline_mode=pl.Buffered(1))


def _moba_attention(qa_t, ka, va_t, km):
    b, n_blk, w, tq = qa_t.shape
    s = n_blk * tq
    qh, work = _attention_scratch(s, tq)
    return pl.pallas_call(
        _moba_kernel,
        grid=(b, n_blk),
        in_specs=[
            pl.BlockSpec((1, 1, w, tq), lambda bi, i: (bi, i, 0, 0)),
            _resident_spec((1, s, w)),
            _resident_spec((1, n_blk, w, tq)),
            _resident_spec((1, n_blk, w)),
        ],
        out_specs=pl.BlockSpec((1, 1, w, tq), lambda bi, i: (bi, i, 0, 0)),
        out_shape=jax.ShapeDtypeStruct((b, n_blk, w, tq), BF16),
        scratch_shapes=qh + [pltpu.VMEM((N_HEADS, n_blk, tq), F32)] + work,
        compiler_params=_cparams(("parallel", "arbitrary")),
        name="moba_attention",
    )(qa_t, ka, va_t, km)


INT_MIN = -2 ** 31


def _sortable_key(s):
    bits = lax.bitcast_convert_type(s, I32)
    return bits ^ ((bits >> 31) & 0x7FFFFFFF)


def _column_count(mask):
    tk, tq = mask.shape
    return jnp.sum(mask.astype(I32).reshape(tk // SUBLANES, SUBLANES, tq), axis=0)


I16 = jnp.int16
HALF_BITS = 16
I16_MIN = -2 ** (HALF_BITS - 1)
PACKED_ROWS = 2 * SUBLANES
SEARCH_GROUP = 4


def _packed_column_count(mask):
    ones = mask.astype(I16)
    parts = [ones[r:r + PACKED_ROWS, :] for r in range(0, ones.shape[0], PACKED_ROWS)]
    while len(parts) > 1:
        parts = [a + b for a, b in zip(parts[::2], parts[1::2])] + parts[len(parts) - len(parts) % 2:]
    return parts[0]


def _packed_total(count16):
    return jnp.sum(count16.astype(I32), axis=0, keepdims=True)


def _dsa_kernel(iq_t_ref, ik_ref, iw_t_ref, q_t_ref, k_ref, v_t_ref, o_ref,
                key_ref, hi_ref, lo_ref, lo2_ref, bias_ref, qh_ref, s_even_ref, s_odd_ref, *acc_refs,
                topk, index_bits):
    s_refs = (s_even_ref, s_odd_ref)
    t = pl.program_id(1)
    tq = q_t_ref.shape[3]
    tk = tq
    n_chunks = t + 1
    qpos = t * tq + lax.broadcasted_iota(I32, (1, tq), 1)
    krow = lax.broadcasted_iota(I32, (tk, tq), 0)

    def chunk_rows(c):
        return pl.ds(pl.multiple_of(c * tk, tk), tk)

    def total(count8):
        return jnp.sum(count8, axis=0, keepdims=True)

    def count_over_chunks(pred):
        def body(c, cnt):
            return cnt + _column_count(pred(key_ref[chunk_rows(c), :], c * tk + krow))
        return total(lax.fori_loop(0, n_chunks, body, jnp.zeros((SUBLANES, tq), I32)))

    iq = iq_t_ref[0, 0]
    iw = iw_t_ref[0, 0]

    def score_chunks(c0, count, carry):
        for c in (c0 + d for d in range(count)):
            ik_c = ik_ref[0, chunk_rows(c), :]
            score = jnp.zeros((tk, tq), F32)
            for h in range(IDX_HEADS):
                rel = jnp.dot(ik_c, iq[h * IDX_DIM:(h + 1) * IDX_DIM, :], preferred_element_type=F32)
                score = score + jnp.maximum(rel, 0.0) * iw[h:h + 1, :]
            key = jnp.where(c * tk + krow <= qpos, _sortable_key(score), INT_MIN)
            key_ref[chunk_rows(c), :] = key
            hi_ref[chunk_rows(c), :] = (key >> HALF_BITS).astype(I16)
            lo_ref[chunk_rows(c), :] = key.astype(I16) ^ jnp.int16(I16_MIN)
        return carry

    _loop_in_groups(n_chunks, score_chunks, 0)

    n_groups = (n_chunks + SEARCH_GROUP - 1) // SEARCH_GROUP
    for d in range(SEARCH_GROUP - 1):
        @pl.when(n_chunks + d < n_groups * SEARCH_GROUP)
        def _(d=d):
            lowest = jnp.full((tk, tq), I16_MIN, I16)
            hi_ref[chunk_rows(n_chunks + d), :] = lowest
            lo_ref[chunk_rows(n_chunks + d), :] = lowest

    def packed_counts(n_counts, chunk_masks):
        def body(g, cnts):
            cnts = list(cnts)
            for c in (g * SEARCH_GROUP + d for d in range(SEARCH_GROUP)):
                for i, mask in enumerate(chunk_masks(c)):
                    cnts[i] = cnts[i] + _packed_column_count(mask)
            return tuple(cnts)
        zero = jnp.zeros((PACKED_ROWS, tq), I16)
        return [_packed_total(c) for c in lax.fori_loop(0, n_groups, body, (zero,) * n_counts)]

    def kth_largest_half(half_ref, kth):
        def half_bit(b, prefix):
            cand = prefix | jnp.left_shift(jnp.int32(1), HALF_BITS - 1 - b)
            cand_half = (cand + I16_MIN).astype(I16)
            n_ge, = packed_counts(1, lambda c: [half_ref[chunk_rows(c), :] >= cand_half])
            return jnp.where(n_ge >= kth, cand, prefix)

        return lax.fori_loop(0, HALF_BITS, half_bit, jnp.zeros((1, tq), I32)) + I16_MIN

    thr_hi = kth_largest_half(hi_ref, topk)
    thr_hi_half = thr_hi.astype(I16)

    def above_hi(c):
        hi = hi_ref[chunk_rows(c), :]
        lo2_ref[chunk_rows(c), :] = jnp.where(hi == thr_hi_half, lo_ref[chunk_rows(c), :], jnp.int16(I16_MIN))
        return [hi > thr_hi_half]

    n_gt_hi, = packed_counts(1, above_hi)
    thr_lo = kth_largest_half(lo2_ref, topk - n_gt_hi)
    thr_lo_half = thr_lo.astype(I16)
    thr = jnp.left_shift(thr_hi, HALF_BITS) | (thr_lo - I16_MIN)

    n_gt_lo, n_eq = packed_counts(2, lambda c: [
        lo2_ref[chunk_rows(c), :] > thr_lo_half,
        (hi_ref[chunk_rows(c), :] == thr_hi_half) & (lo_ref[chunk_rows(c), :] == thr_lo_half)])
    need = topk - (n_gt_hi + n_gt_lo)
    has_surplus = jnp.any((n_eq > need) & (thr != INT_MIN))

    def last_tie_index():
        def index_bit(b, prefix):
            cand = prefix | jnp.left_shift(jnp.int32(1), index_bits - 1 - b)
            n_before = count_over_chunks(lambda keys, kpos: (keys == thr) & (kpos < cand))
            return jnp.where(n_before < need, cand, prefix)
        return lax.fori_loop(0, index_bits, index_bit, jnp.zeros((1, tq), I32))

    def bias_with_ties():
        tie_end = last_tie_index()

        def bias_chunk(c, carry):
            keys = key_ref[chunk_rows(c), :]
            kpos = c * tk + krow
            chosen = ((keys > thr) | ((keys == thr) & (kpos <= tie_end))) & (kpos <= qpos)
            bias_ref[chunk_rows(c), :] = jnp.where(chosen, 0.0, MASKED)
            return carry

        lax.fori_loop(0, n_chunks, bias_chunk, 0)

    def bias_without_ties():
        floor = jnp.maximum(thr, INT_MIN + 1)

        def bias_chunks(c0, count, carry):
            for c in (c0 + d for d in range(count)):
                bias_ref[chunk_rows(c), :] = jnp.where(key_ref[chunk_rows(c), :] >= floor, 0.0, MASKED)
            return carry

        _loop_in_groups(n_chunks, bias_chunks, 0)

    lax.cond(has_surplus, bias_with_ties, bias_without_ties)

    _split_heads(q_t_ref[0, 0], qh_ref)

    def masked_scores(g, c, mx):
        for gi, h in _group_heads(g):
            s = jnp.dot(k_ref[0, chunk_rows(c), _head_lanes(h)], qh_ref[h], preferred_element_type=F32)
            s = s + bias_ref[chunk_rows(c), :]
            s_refs[g % 2][gi, chunk_rows(c), :] = s
            mx[gi] = jnp.maximum(mx[gi], jnp.max(s, axis=0, keepdims=True))

    def weighted(g, gi, h, c, mx):
        return _weighted_values(s_refs[g % 2][gi, chunk_rows(c), :], mx[gi], v_t_ref[0, c, _head_rows(h), :])

    def finish(g, mx):
        for gi, h in _group_heads(g):
            _store_head_output(o_ref, h, acc_refs[gi][...])

    _attention_pipeline(n_chunks, lambda g: (jnp.full((1, tq), MASKED, F32),) * ATTN_GROUP,
                        masked_scores, weighted, finish, acc_refs)


def _dsa_attention(iq_t, ik, iw_t, qb_t, kb, vb_t):
    b, n_blk, w, tq = qb_t.shape
    s = n_blk * tq
    topk = min(DSA_TOPK_MAX, s // 4)
    assert n_blk % SEARCH_GROUP == 0, "the packed search pads its chunk range to whole groups"
    qh, work = _attention_scratch(s, tq)
    tile_spec = lambda rows: pl.BlockSpec((1, 1, rows, tq), lambda bi, ti: (bi, ti, 0, 0))
    return pl.pallas_call(
        functools.partial(_dsa_kernel, topk=topk, index_bits=max(1, (s - 1).bit_length())),
        grid=(b, n_blk),
        in_specs=[
            tile_spec(IDX_HEADS * IDX_DIM),
            _resident_spec((1, s, IDX_DIM)),
            tile_spec(IDX_HEADS),
            tile_spec(w),
            _resident_spec((1, s, w)),
            _resident_spec((1, n_blk, w, tq)),
        ],
        out_specs=pl.BlockSpec((1, 1, w, tq), lambda bi, ti: (bi, ti, 0, 0)),
        out_shape=jax.ShapeDtypeStruct((b, n_blk, w, tq), BF16),
        scratch_shapes=[pltpu.VMEM((s, tq), I32)] + [pltpu.VMEM((s, tq), I16)] * 3
        + [pltpu.VMEM((s, tq), F32)] + qh + work,
        compiler_params=_cparams(("parallel", "arbitrary")),
        name="dsa_attention",
    )(iq_t, ik, iw_t, qb_t, kb, vb_t)


EXPERT_ROW0 = SUBLANES
ROUTER_ROWS = EXPERT_ROW0 + N_EXPERTS
EXPERT_TOPK = 2


def _rms_norm(x, g):
    ms = jnp.mean(x * x, axis=-1, keepdims=True)
    return (x * lax.rsqrt(ms + RMS_EPS)) * g


HIGH_HALF = -2 ** 16


def _pack_bf16_halves(x):
    bits = lax.bitcast_convert_type(x.astype(BF16).astype(F32), I32)
    half = x.shape[1] // 2
    return (bits[:, :half] & HIGH_HALF) | lax.shift_right_logical(bits[:, half:], 16)


def _unpack_bf16_halves(packed):
    upper = lax.bitcast_convert_type(packed & HIGH_HALF, F32)
    lower = lax.bitcast_convert_type(lax.shift_left(packed, 16), F32)
    return jnp.concatenate([upper, lower], axis=1).astype(BF16)


def _first_index_of_max(vals, idx):
    top = jnp.max(vals, axis=0, keepdims=True)
    first = jnp.min(jnp.where(vals == top, idx, vals.shape[0]), axis=0, keepdims=True)
    return top, first


def _merge_router_kernel(oa_ref, ob_ref, ga_ref, gb_ref, x_ref, wa_ref, wb_ref, wo_ref, g_ref,
                         wr_ref, br_ref, x1_ref, h2_ref, ids_ref, cw_ref, rank_ref, cnt_ref):
    @pl.when(pl.program_id(0) == 0)
    def _():
        cnt_ref[...] = jnp.zeros_like(cnt_ref)

    def branch(o_t_ref, w_ref):
        return jnp.concatenate([lax.dot_general(o_t_ref[c], w_ref[...], TN_DIMS, preferred_element_type=F32)
                                for c in range(o_t_ref.shape[0])], axis=0)

    a = branch(oa_ref, wa_ref)
    b = branch(ob_ref, wb_ref)
    merged = jax.nn.sigmoid(ga_ref[...]) * a + jax.nn.sigmoid(gb_ref[...]) * b
    x1 = x_ref[...] + jnp.dot(merged.astype(BF16), wo_ref[...], preferred_element_type=F32)
    x1_ref[...] = x1
    h2 = _rms_norm(x1, g_ref[...])
    h2_ref[...] = _pack_bf16_halves(h2)
    tm = h2.shape[0]

    logits = lax.dot_general(wr_ref[...], h2, NT_DIMS, preferred_element_type=F32,
                             precision=lax.Precision.HIGHEST) + br_ref[...]
    grp = logits[0:N_GROUPS, :]
    g_top, g_idx = _first_index_of_max(grp, lax.broadcasted_iota(I32, grp.shape, 0))
    p_grp = 1.0 / jnp.sum(jnp.exp(grp - g_top), axis=0, keepdims=True)
    in_grp = logits[EXPERT_ROW0:EXPERT_ROW0 + EXPERTS_PER_GROUP, :]
    for gi in range(1, N_GROUPS):
        rows = slice(EXPERT_ROW0 + gi * EXPERTS_PER_GROUP, EXPERT_ROW0 + (gi + 1) * EXPERTS_PER_GROUP)
        in_grp = jnp.where(g_idx == gi, logits[rows, :], in_grp)
    e_iota = lax.broadcasted_iota(I32, in_grp.shape, 0)
    v0, i0 = _first_index_of_max(in_grp, e_iota)
    v1, i1 = _first_index_of_max(jnp.where(e_iota == i0, -jnp.inf, in_grp), e_iota)
    e1 = jnp.exp(v1 - v0)
    denom = 1.0 + e1
    ids = jnp.concatenate([g_idx * EXPERTS_PER_GROUP + i0, g_idx * EXPERTS_PER_GROUP + i1], axis=0)
    ids_ref[0] = ids
    cw_ref[0] = jnp.concatenate([p_grp * (1.0 / denom), p_grp * (e1 / denom)], axis=0)

    before = (lax.broadcasted_iota(I32, (tm, tm), 0) < lax.broadcasted_iota(I32, (tm, tm), 1)).astype(BF16)
    expert = lax.broadcasted_iota(I32, (N_EXPERTS, tm), 0)
    ranks = []
    for slot in range(EXPERT_TOPK):
        onehot = expert == ids[slot:slot + 1, :]
        seen = jnp.dot(onehot.astype(BF16), before, preferred_element_type=F32) + cnt_ref[...]
        ranks.append(jnp.sum(jnp.where(onehot, seen, 0.0), axis=0, keepdims=True))
        cnt_ref[...] += jnp.sum(onehot.astype(F32), axis=1, keepdims=True)
    rank_ref[0] = jnp.concatenate(ranks, axis=0).astype(I32)


def _merge_and_route(o_a, o_b, ga, gb, x, w_br_a, w_br_b, w_out, g_ffn, w_grp, b_grp, w_rt, b_rt, tm=512):
    t, d = x.shape
    n_tiles = t // tm
    wr = jnp.zeros((ROUTER_ROWS, d), F32).at[0:N_GROUPS].set(w_grp.T).at[EXPERT_ROW0:].set(w_rt.T)
    br = jnp.zeros((ROUTER_ROWS, 1), F32).at[0:N_GROUPS, 0].set(b_grp).at[EXPERT_ROW0:, 0].set(b_rt)
    row_spec = lambda w: pl.BlockSpec((tm, w), lambda i: (i, 0))
    mixer_spec = pl.BlockSpec((tm // Q_TILE, MIXER_W, Q_TILE), lambda i: (i, 0, 0))
    slot_spec = pl.BlockSpec((1, EXPERT_TOPK, tm), lambda i: (i, 0, 0))
    slot_shape = lambda dt: jax.ShapeDtypeStruct((n_tiles, EXPERT_TOPK, tm), dt)
    return pl.pallas_call(
        _merge_router_kernel,
        grid=(n_tiles,),
        in_specs=[mixer_spec, mixer_spec, row_spec(d), row_spec(d), row_spec(d),
                  _const_spec((MIXER_W, d)), _const_spec((MIXER_W, d)), _const_spec((d, d)),
                  _const_spec((1, d)), _const_spec((ROUTER_ROWS, d)), _const_spec((ROUTER_ROWS, 1))],
        out_specs=[row_spec(d), row_spec(d // 2), slot_spec, slot_spec, slot_spec,
                   pl.BlockSpec((N_EXPERTS, 1), lambda i: (0, 0))],
        out_shape=[jax.ShapeDtypeStruct((t, d), F32), jax.ShapeDtypeStruct((t, d // 2), I32),
                   slot_shape(I32), slot_shape(F32), slot_shape(I32),
                   jax.ShapeDtypeStruct((N_EXPERTS, 1), F32)],
        compiler_params=_cparams(("arbitrary",)),
        name="merge_and_route",
    )(o_a, o_b, ga, gb, x, w_br_a.astype(BF16), w_br_b.astype(BF16), w_out.astype(BF16),
      g_ffn.reshape(1, d), wr, br)


EXPERT_ROWS = 512
DMA_ISSUE_UNROLL = 8


def _row(ref, r):
    return ref.at[pl.ds(r, 1), :]


def _tile_dest(dest_ref, tm, slot, r):
    return dest_ref[(pl.program_id(0) * EXPERT_TOPK + slot) * tm + r]


def _dispatch_kernel(dest_ref, pad_block_ref, h2_ref, xs_hbm, zero_ref, sem, zero_sem):
    tm = h2_ref.shape[0]

    @pl.when(pl.program_id(0) == 0)
    def _():
        zero_ref[...] = jnp.zeros(zero_ref.shape, zero_ref.dtype)

        def block_copy(b):
            rows = pl.ds(pl.multiple_of(b * EXPERT_ROWS, EXPERT_ROWS), EXPERT_ROWS)
            return pltpu.make_async_copy(zero_ref, xs_hbm.at[rows, :], zero_sem)

        def start(b, n):
            @pl.when(pad_block_ref[b] != 0)
            def _():
                block_copy(b).start()
            return n + pad_block_ref[b]

        n_started = lax.fori_loop(0, pad_block_ref.shape[0], start, 0)

        def drain(k, carry):
            block_copy(0).wait()
            return carry

        lax.fori_loop(0, n_started, drain, 0)

    for r in range(tm):
        for slot in range(EXPERT_TOPK):
            pltpu.make_async_copy(_row(h2_ref, r), _row(xs_hbm, _tile_dest(dest_ref, tm, slot, r)), sem).start()
    for slot in range(EXPERT_TOPK):
        pltpu.make_async_copy(h2_ref, xs_hbm.at[pl.ds(0, tm), :], sem).wait()


def _dispatch(dest, pad_block, h2, tm):
    t, d = h2.shape
    n_rows = pad_block.shape[0] * EXPERT_ROWS
    return pl.pallas_call(
        _dispatch_kernel,
        grid_spec=pltpu.PrefetchScalarGridSpec(
            num_scalar_prefetch=2,
            grid=(t // tm,),
            in_specs=[pl.BlockSpec((tm, d), lambda i, dest, pad: (i, 0))],
            out_specs=pl.BlockSpec(memory_space=pl.ANY),
            scratch_shapes=[pltpu.VMEM((EXPERT_ROWS, d), h2.dtype), pltpu.SemaphoreType.DMA(()),
                            pltpu.SemaphoreType.DMA(())],
        ),
        out_shape=jax.ShapeDtypeStruct((n_rows, d), h2.dtype),
        compiler_params=_cparams(("arbitrary",)),
        name="moe_dispatch",
    )(dest, pad_block, h2)


def _expert_kernel(blk_exp_ref, n_used_ref, xs_ref, wg_ref, wu_ref, wd_ref, ys_ref, wg16, wu16, wd16):
    i = pl.program_id(0)
    used = i < n_used_ref[0]

    @pl.when((i == 0) | (blk_exp_ref[i] != blk_exp_ref[jnp.maximum(i - 1, 0)]))
    def _():
        wg16[...] = wg_ref[0].astype(BF16)
        wu16[...] = wu_ref[0].astype(BF16)
        wd16[...] = wd_ref[0].astype(BF16)

    @pl.when(used)
    def _():
        xb = _unpack_bf16_halves(xs_ref[...])
        gate = jnp.dot(xb, wg16[...], preferred_element_type=F32)
        up = jnp.dot(xb, wu16[...], preferred_element_type=F32)
        hid = (gate * jax.nn.sigmoid(gate)) * up
        ys_ref[...] = jnp.dot(hid.astype(BF16), wd16[...], preferred_element_type=F32)

    @pl.when(jnp.logical_not(used))
    def _():
        ys_ref[...] = jnp.zeros_like(ys_ref)


def _experts(blk_exp, n_used, xs, w_gate, w_up, w_down):
    n_rows, d = xs.shape[0], w_gate.shape[1]
    n_blocks = n_rows // EXPERT_ROWS
    expert_spec = lambda rows, cols: pl.BlockSpec((1, rows, cols), lambda i, be, nu: (be[i], 0, 0))
    return pl.pallas_call(
        _expert_kernel,
        grid_spec=pltpu.PrefetchScalarGridSpec(
            num_scalar_prefetch=2,
            grid=(n_blocks,),
            in_specs=[pl.BlockSpec((EXPERT_ROWS, xs.shape[1]), lambda i, be, nu: (i, 0)),
                      expert_spec(d, D_EXPERT), expert_spec(d, D_EXPERT), expert_spec(D_EXPERT, d)],
            out_specs=pl.BlockSpec((EXPERT_ROWS, d), lambda i, be, nu: (i, 0)),
            scratch_shapes=[pltpu.VMEM((d, D_EXPERT), BF16), pltpu.VMEM((d, D_EXPERT), BF16),
                            pltpu.VMEM((D_EXPERT, d), BF16)],
        ),
        out_shape=jax.ShapeDtypeStruct((n_rows, d), F32),
        compiler_params=_cparams(("arbitrary",)),
        name="moe_experts",
    )(blk_exp, n_used, xs, w_gate, w_up, w_down)


def _combine_ple_kernel(dest_ref, x1_ref, cw_ref, p_ref, ys_hbm, g_ple_ref, wpg_ref, wpp_ref, g_fin_ref,
                        out_ref, y_even, y_odd, sem):
    tm = y_even.shape[1]
    i = pl.program_id(0)
    last_tile = 2 * pl.num_programs(0) - 1
    bufs = ((y_even, 0), (y_odd, 1))

    def start_row(tile, buf, sem_slot, r):
        for slot in range(EXPERT_TOPK):
            src = _row(ys_hbm, dest_ref[(tile * EXPERT_TOPK + slot) * tm + r])
            pltpu.make_async_copy(src, _row(buf.at[slot], r), sem.at[sem_slot]).start()

    def wait_tile(buf, sem_slot):
        for slot in range(EXPERT_TOPK):
            pltpu.make_async_copy(ys_hbm.at[pl.ds(0, tm), :], buf.at[slot], sem.at[sem_slot]).wait()

    def combine(half, buf):
        rows = slice(half * tm, (half + 1) * tm)
        cw = cw_ref[rows, :]
        x2 = x1_ref[rows, :] + (buf[0] * cw[:, 0:1] + buf[1] * cw[:, 1:2])
        h3 = _rms_norm(x2, g_ple_ref[...]).astype(BF16)
        gate = jax.nn.sigmoid(jnp.dot(h3, wpg_ref[...], preferred_element_type=F32))
        proj = jnp.dot(p_ref[rows, :].astype(BF16), wpp_ref[...], preferred_element_type=F32)
        out_ref[rows, :] = _rms_norm(x2 + gate * proj, g_fin_ref[...])

    @pl.when(i == 0)
    def _():
        def issue(r, carry):
            start_row(0, *bufs[0], r)
            return carry
        lax.fori_loop(0, tm, issue, 0, unroll=DMA_ISSUE_UNROLL)

    for half in range(2):
        buf, sem_slot = bufs[half]
        wait_tile(buf, sem_slot)
        combine(half, buf)
        next_tile = jnp.minimum(2 * i + half + 1, last_tile)
        for r in range(tm):
            start_row(next_tile, *bufs[1 - half], r)

    @pl.when(i == pl.num_programs(0) - 1)
    def _():
        wait_tile(*bufs[0])


def _combine_ple(dest, x1, cw_tok, p, ys, g_ple, w_ple_gate, w_ple_proj, g_final, tm):
    t, d = x1.shape
    assert t % (2 * tm) == 0
    row_spec = lambda w: pl.BlockSpec((2 * tm, w), lambda i, dest: (i, 0))
    return pl.pallas_call(
        _combine_ple_kernel,
        grid_spec=pltpu.PrefetchScalarGridSpec(
            num_scalar_prefetch=1,
            grid=(t // (2 * tm),),
            in_specs=[row_spec(d), row_spec(EXPERT_TOPK), row_spec(p.shape[1]),
                      pl.BlockSpec(memory_space=pl.ANY),
                      _const_spec((1, d)), _const_spec((d, d)), _const_spec((p.shape[1], d)), _const_spec((1, d))],
            out_specs=row_spec(d),
            scratch_shapes=[pltpu.VMEM((EXPERT_TOPK, tm, d), F32), pltpu.VMEM((EXPERT_TOPK, tm, d), F32),
                            pltpu.SemaphoreType.DMA((2,))],
        ),
        out_shape=jax.ShapeDtypeStruct((t, d), F32),
        compiler_params=_cparams(("arbitrary",)),
        name="combine_ple",
    )(dest, x1, cw_tok, p, ys, g_ple.reshape(1, d), w_ple_gate.astype(BF16), w_ple_proj.astype(BF16),
      g_final.reshape(1, d))


def _layer(x, p, g_attn, w_in, w_br_a, w_br_b, w_out, g_ffn, w_grp, b_grp, w_rt, b_rt,
           w_gate, w_up, w_down, g_ple, w_ple_gate, w_ple_proj, g_final):
    b, s, d = x.shape
    t = b * s
    (qa_t, va_t, qb_t, vb_t, iq_t, iw_t, ka, kb, ik, ga, gb, km) = _in_projection(x, g_attn, w_in)
    o_a = _moba_attention(qa_t, ka, va_t, km.reshape(b, s // MOBA_BLOCK, MIXER_W))
    o_b = _dsa_attention(iq_t, ik, iw_t, qb_t, kb, vb_t)
    x1, h2, ids, cw, rank, counts = _merge_and_route(
        o_a.reshape(t // Q_TILE, MIXER_W, Q_TILE), o_b.reshape(t // Q_TILE, MIXER_W, Q_TILE),
        ga.reshape(t, d), gb.reshape(t, d),
        x.reshape(t, d), w_br_a, w_br_b, w_out, g_ffn, w_grp, b_grp, w_rt, b_rt)

    counts = counts[:, 0].astype(I32)
    padded = ((counts + EXPERT_ROWS - 1) // EXPERT_ROWS) * EXPERT_ROWS
    pend = jnp.cumsum(padded)
    pstart = pend - padded
    n_blocks = -(-(t * EXPERT_TOPK) // EXPERT_ROWS) + N_EXPERTS
    experts = jnp.arange(N_EXPERTS, dtype=I32)
    dest = jnp.sum(jnp.where(ids[..., None] == experts, pstart, 0), axis=-1) + rank
    block_row0 = jnp.arange(n_blocks, dtype=I32) * EXPERT_ROWS
    blk_exp = jnp.minimum(jnp.sum((pend[None, :] <= block_row0[:, None]).astype(I32), axis=1), N_EXPERTS - 1)
    n_used = (pend[-1:] // EXPERT_ROWS).astype(I32)

    route_tile = dest.shape[2]
    dest = dest.reshape(-1)
    is_last_of_expert = jnp.any(pend[None, :] == (block_row0 + EXPERT_ROWS)[:, None], axis=1)
    pad_block = (is_last_of_expert | (block_row0 >= pend[-1])).astype(I32)
    xs = _dispatch(dest, pad_block, h2, route_tile)
    ys = _experts(blk_exp, n_used, xs, w_gate, w_up, w_down)
    cw_tok = jnp.swapaxes(cw, 1, 2).reshape(t, EXPERT_TOPK)
    out = _combine_ple(dest, x1, cw_tok, p.reshape(t, p.shape[-1]), ys, g_ple, w_ple_gate, w_ple_proj, g_final,
                       route_tile)
    return out.reshape(b, s, d)


def kernel(x, p, g_attn, w_in, w_br_a, w_br_b, w_out, g_ffn, w_grp, b_grp, w_rt, b_rt, w_gate, w_up, w_down, g_ple, w_ple_gate, w_ple_proj, g_final):
    depth = w_in.shape[0]
    assert depth == 1, "the final RMSNorm is fused into the last layer's kernel"
    i = 0
    return _layer(x, p[i], g_attn[i], w_in[i], w_br_a[i], w_br_b[i], w_out[i], g_ffn[i], w_grp[i], b_grp[i],
                  w_rt[i], b_rt[i], w_gate[i], w_up[i], w_down[i], g_ple[i], w_ple_gate[i], w_ple_proj[i], g_final)
```

```python
import functools

import jax
import jax.numpy as jnp
from jax import lax
from jax.experimental import pallas as pl
from jax.experimental.pallas import tpu as pltpu

F32 = jnp.float32
BF16 = jnp.bfloat16
I32 = jnp.int32

HEAD_DIM = 64
N_HEADS = 8
ROT_DIM = HEAD_DIM // 4
ROT_HALF = ROT_DIM // 2
ROPE_THETA = 500000.0
MOBA_BLOCK = 256
MOBA_TOPK = 3
IDX_HEADS = 8
IDX_DIM = 64
DSA_TOPK_MAX = 256
N_GROUPS = 4
EXPERTS_PER_GROUP = 8
N_EXPERTS = N_GROUPS * EXPERTS_PER_GROUP
D_EXPERT = 512
PLE_DIM = 256
RMS_EPS = 1e-6
MIXER_W = N_HEADS * HEAD_DIM

Q_TILE = 256
LANES = 128
SUBLANES = 8
VMEM_LIMIT = 56 * 1024 * 1024

NT_DIMS = (((1,), (1,)), ((), ()))
TN_DIMS = (((0,), (0,)), ((), ()))


def _cparams(sem):
    return pltpu.CompilerParams(dimension_semantics=sem, vmem_limit_bytes=VMEM_LIMIT)


def _const_spec(shape):
    nd = len(shape)
    return pl.BlockSpec(shape, lambda *_: (0,) * nd, pipeline_mode=pl.Buffered(1))


def _rope_feature_major(z, cos_t, sin_t):
    tm = z.shape[1]
    z3 = z.reshape(N_HEADS, HEAD_DIM, tm)
    x1 = z3[:, 0:ROT_HALF, :]
    x2 = z3[:, ROT_HALF:ROT_DIM, :]
    o1 = x1 * cos_t - x2 * sin_t
    o2 = x2 * cos_t + x1 * sin_t
    return jnp.concatenate([o1, o2, z3[:, ROT_DIM:, :]], axis=1).reshape(N_HEADS * HEAD_DIM, tm)


def _rope_token_major(z, c_tab, s_lo, s_hi):
    up = pltpu.roll(z, LANES - ROT_HALF, 1)
    dn = pltpu.roll(z, ROT_HALF, 1)
    return z * c_tab + up * s_lo + dn * s_hi


def _store_blocked(ref, z):
    for c in range(z.shape[1] // Q_TILE):
        ref[0, c] = z[:, c * Q_TILE:(c + 1) * Q_TILE]


def _inproj_kernel(x_ref, g_ref, wf_ref, wiw_ref, wt_ref, wg_ref, cos_t_ref, sin_t_ref,
                   ctab_ref, slo_ref, shi_ref,
                   qa_t_ref, va_t_ref, qb_t_ref, vb_t_ref, iq_t_ref, iw_t_ref,
                   ka_ref, kb_ref, ik_ref, ga_ref, gb_ref, km_ref, *, w_scale):
    x = x_ref[0]
    ms = jnp.mean(x * x, axis=-1, keepdims=True)
    h = ((x * lax.rsqrt(ms + RMS_EPS)) * g_ref[...]).astype(BF16)
    cos_t = cos_t_ref[...]
    sin_t = sin_t_ref[...]
    q_scale = HEAD_DIM ** -0.5

    fm_outs = ((qa_t_ref, True, q_scale), (va_t_ref, False, 1.0), (qb_t_ref, True, q_scale),
               (vb_t_ref, False, 1.0), (iq_t_ref, True, IDX_DIM ** -0.5))
    for i, (ref, rope, scale) in enumerate(fm_outs):
        z = lax.dot_general(wf_ref[i * MIXER_W:(i + 1) * MIXER_W, :], h, NT_DIMS,
                            preferred_element_type=F32)
        if rope:
            z = _rope_feature_major(z, cos_t, sin_t)
        if scale != 1.0:
            z = z * scale
        _store_blocked(ref, z.astype(BF16))

    iw = lax.dot_general(wiw_ref[...], h, NT_DIMS, preferred_element_type=F32)
    _store_blocked(iw_t_ref, iw * w_scale)

    zt = jnp.dot(h, wt_ref[...], preferred_element_type=F32)
    ctab, slo, shi = ctab_ref[...], slo_ref[...], shi_ref[...]
    n_grp = zt.shape[1] // LANES
    roped = [_rope_token_major(zt[:, j * LANES:(j + 1) * LANES], ctab, slo, shi) for j in range(n_grp)]
    per_mixer = MIXER_W // LANES
    ka = jnp.concatenate(roped[:per_mixer], axis=1)
    kb = jnp.concatenate(roped[per_mixer:2 * per_mixer], axis=1)
    ka_ref[0] = ka.astype(BF16)
    kb_ref[0] = kb.astype(BF16)
    ik_ref[0] = roped[2 * per_mixer][:, :IDX_DIM].astype(BF16)
    tm = ka.shape[0]
    km_ref[0] = jnp.mean(ka.reshape(tm // MOBA_BLOCK, MOBA_BLOCK, MIXER_W), axis=1, keepdims=True)

    zg = jnp.dot(h, wg_ref[...], preferred_element_type=F32)
    d_model = zg.shape[1] // 2
    ga_ref[0] = zg[:, :d_model]
    gb_ref[0] = zg[:, d_model:]


def _rope_tables(seq):
    inv = 1.0 / (ROPE_THETA ** (jnp.arange(0, ROT_DIM, 2, dtype=F32) / ROT_DIM))
    ang = jnp.arange(seq, dtype=F32)[:, None] * inv[None, :]
    cos, sin = jnp.cos(ang), jnp.sin(ang)
    d = jnp.arange(LANES) % HEAD_DIM
    lo = d < ROT_HALF
    hi = (d >= ROT_HALF) & (d < ROT_DIM)
    f = d % ROT_HALF
    cos_l, sin_l = cos[:, f], sin[:, f]
    ctab = jnp.where(lo | hi, cos_l, 1.0)
    slo = jnp.where(lo, -sin_l, 0.0)
    shi = jnp.where(hi, sin_l, 0.0)
    return cos.T, sin.T, ctab, slo, shi


def _in_projection(x, g_attn, w_in, tm=512):
    b, s, d = x.shape
    splits = (MIXER_W,) * 6 + (IDX_HEADS * IDX_DIM, IDX_DIM, IDX_HEADS, d, d)
    offs = [0]
    for w in splits:
        offs.append(offs[-1] + w)
    wqa, wka, wva, wqb, wkb, wvb, wiq, wik, wiw, wga, wgb = (
        w_in[:, offs[i]:offs[i + 1]] for i in range(len(splits)))
    wf = jnp.concatenate([wqa, wva, wqb, wvb, wiq], axis=1).T.astype(BF16)
    wiw_t = wiw.T.astype(BF16)
    wt = jnp.concatenate([wka, wkb, wik, jnp.zeros((d, LANES - IDX_DIM), w_in.dtype)], axis=1).astype(BF16)
    wg = jnp.concatenate([wga, wgb], axis=1).astype(BF16)
    cos_t, sin_t, ctab, slo, shi = _rope_tables(s)
    n_blk = s // MOBA_BLOCK
    fm_shape = jax.ShapeDtypeStruct((b, s // Q_TILE, MIXER_W, Q_TILE), BF16)
    tok_shape = jax.ShapeDtypeStruct((b, s, MIXER_W), BF16)
    fm_spec = pl.BlockSpec((1, tm // Q_TILE, MIXER_W, Q_TILE), lambda bi, ti: (bi, ti, 0, 0))
    tok_spec = pl.BlockSpec((1, tm, MIXER_W), lambda bi, ti: (bi, ti, 0))
    gate_spec = pl.BlockSpec((1, tm, d), lambda bi, ti: (bi, ti, 0))
    outs = pl.pallas_call(
        functools.partial(_inproj_kernel, w_scale=IDX_HEADS ** -0.5),
        grid=(b, s // tm),
        in_specs=[
            pl.BlockSpec((1, tm, d), lambda bi, ti: (bi, ti, 0)),
            _const_spec((1, d)),
            _const_spec(wf.shape), _const_spec(wiw_t.shape), _const_spec(wt.shape), _const_spec(wg.shape),
            pl.BlockSpec((ROT_HALF, tm), lambda bi, ti: (0, ti)),
            pl.BlockSpec((ROT_HALF, tm), lambda bi, ti: (0, ti)),
            pl.BlockSpec((tm, LANES), lambda bi, ti: (ti, 0)),
            pl.BlockSpec((tm, LANES), lambda bi, ti: (ti, 0)),
            pl.BlockSpec((tm, LANES), lambda bi, ti: (ti, 0)),
        ],
        out_specs=[
            fm_spec, fm_spec, fm_spec, fm_spec, fm_spec,
            pl.BlockSpec((1, tm // Q_TILE, IDX_HEADS, Q_TILE), lambda bi, ti: (bi, ti, 0, 0)),
            tok_spec, tok_spec,
            pl.BlockSpec((1, tm, IDX_DIM), lambda bi, ti: (bi, ti, 0)),
            gate_spec, gate_spec,
            pl.BlockSpec((1, tm // MOBA_BLOCK, 1, MIXER_W), lambda bi, ti: (bi, ti, 0, 0)),
        ],
        out_shape=[
            fm_shape, fm_shape, fm_shape, fm_shape, fm_shape,
            jax.ShapeDtypeStruct((b, s // Q_TILE, IDX_HEADS, Q_TILE), F32),
            tok_shape, tok_shape,
            jax.ShapeDtypeStruct((b, s, IDX_DIM), BF16),
            jax.ShapeDtypeStruct((b, s, d), F32), jax.ShapeDtypeStruct((b, s, d), F32),
            jax.ShapeDtypeStruct((b, n_blk, 1, MIXER_W), F32),
        ],
        compiler_params=_cparams(("parallel", "parallel")),
        name="in_projection",
    )(x, g_attn.reshape(1, d), wf, wiw_t, wt, wg, cos_t, sin_t, ctab, slo, shi)
    return outs


MASKED = -1e30


HEADS_PER_GROUP = LANES // HEAD_DIM


def _one_head_of_pair(q_pair, hh):
    row = lax.broadcasted_iota(I32, q_pair.shape, 0)
    return jnp.where((row // HEAD_DIM) == hh, q_pair, jnp.zeros_like(q_pair))


def _head_lanes(h):
    g = h // HEADS_PER_GROUP
    return slice(g * LANES, (g + 1) * LANES)


def _head_rows(h):
    return slice(h * HEAD_DIM, (h + 1) * HEAD_DIM)


ATTN_GROUP = 2
ONES_ROWS = 16
ACC_ROWS = HEAD_DIM + ONES_ROWS


def _split_heads(q_all, qh_ref):
    for h in range(N_HEADS):
        qh_ref[h] = _one_head_of_pair(q_all[_head_lanes(h), :], h % HEADS_PER_GROUP)


def _weighted_values(s, m, v_t):
    p = jnp.exp(s - m).astype(BF16)
    lhs = jnp.concatenate([v_t, jnp.ones((ONES_ROWS, v_t.shape[1]), BF16)], axis=0)
    return jnp.dot(lhs, p, preferred_element_type=F32)


def _loop_in_groups(n, body, init, width=4):
    carry = lax.fori_loop(0, n // width, lambda k, c: body(width * k, width, c), init)
    done = (n // width) * width
    part = width // 2
    while part >= 1:
        has_part = ((n - done) // part) % 2 == 1
        carry = lax.cond(has_part, lambda c, done=done, part=part: body(done, part, c), lambda c: c, carry)
        done = done + jnp.where(has_part, part, 0)
        part //= 2
    return carry


def _store_head_output(o_ref, h, acc):
    o_ref[0, 0, _head_rows(h), :] = (acc[:HEAD_DIM, :] / acc[HEAD_DIM:HEAD_DIM + 1, :]).astype(BF16)


def _moba_kernel(q_t_ref, k_ref, v_t_ref, km_ref, o_ref, qh_ref, bias_ref, s_even_ref, s_odd_ref, *acc_refs):
    s_refs = (s_even_ref, s_odd_ref)
    i = pl.program_id(1)
    tq = q_t_ref.shape[3]
    n_blk = km_ref.shape[1]
    _split_heads(q_t_ref[0, 0], qh_ref)

    blk = lax.broadcasted_iota(I32, (n_blk, tq), 0)
    for h in range(N_HEADS):
        gate = jnp.dot(km_ref[0, :, _head_lanes(h)], qh_ref[h].astype(F32), preferred_element_type=F32)
        gate = jnp.where(blk < i, gate, -jnp.inf)
        keep = blk == i
        for _ in range(MOBA_TOPK):
            _, first = _first_index_of_max(gate, blk)
            taken = (blk == first) & (blk < i)
            keep = keep | taken
            gate = jnp.where(taken, -jnp.inf, gate)
        bias_ref[h] = jnp.where(keep, 0.0, MASKED)

    def key_rows(j):
        return pl.ds(pl.multiple_of(j * tq, tq), tq)

    def block_bias(h, j):
        return bias_ref[h, pl.ds(j, 1), :]

    causal = lax.broadcasted_iota(I32, (tq, tq), 0) <= lax.broadcasted_iota(I32, (tq, tq), 1)

    def scores(h, j):
        return jnp.dot(k_ref[0, key_rows(j), _head_lanes(h)], qh_ref[h], preferred_element_type=F32)

    def own_block_scores(g):
        own_max = []
        for gi, h in _group_heads(g):
            s = jnp.where(causal, scores(h, i), MASKED)
            s_refs[g % 2][gi, key_rows(i), :] = s
            own_max.append(jnp.max(s, axis=0, keepdims=True))
        return tuple(own_max)

    def past_scores(g, j, mx):
        for gi, h in _group_heads(g):
            s = scores(h, j)
            s_refs[g % 2][gi, key_rows(j), :] = s
            mx[gi] = jnp.maximum(mx[gi], jnp.max(s, axis=0, keepdims=True) + block_bias(h, j))

    def weighted(g, gi, h, j, mx):
        return _weighted_values(s_refs[g % 2][gi, key_rows(j), :], mx[gi] - block_bias(h, j),
                                v_t_ref[0, j, _head_rows(h), :])

    def finish(g, mx):
        for gi, h in _group_heads(g):
            _store_head_output(o_ref, h, acc_refs[gi][...] + weighted(g, gi, h, i, mx))

    _attention_pipeline(i, own_block_scores, past_scores, weighted, finish, acc_refs)


def _group_heads(g):
    return list(enumerate(range(g * ATTN_GROUP, (g + 1) * ATTN_GROUP)))


def _attention_pipeline(n, first_max, score_step, weighted, finish, acc_refs):
    n_groups = N_HEADS // ATTN_GROUP
    prev_mx = None
    for g in range(n_groups + 1):
        scoring, weighting = g < n_groups, g > 0
        if weighting:
            for acc_ref in acc_refs:
                acc_ref[...] = jnp.zeros(acc_ref.shape, F32)

        def body(j0, count, mx, g=g, scoring=scoring, weighting=weighting, prev_mx=prev_mx):
            mx = list(mx)
            partial = [0.0] * ATTN_GROUP
            for d in range(count):
                if scoring:
                    score_step(g, j0 + d, mx)
                if weighting:
                    for gi, h in _group_heads(g - 1):
                        partial[gi] = partial[gi] + weighted(g - 1, gi, h, j0 + d, prev_mx)
            if weighting:
                for gi, _ in _group_heads(g - 1):
                    acc_refs[gi][...] += partial[gi]
            return tuple(mx)

        mx = _loop_in_groups(n, body, first_max(g) if scoring else ())
        if weighting:
            finish(g - 1, prev_mx)
        prev_mx = mx


def _attention_scratch(s, tq):
    return ([pltpu.VMEM((N_HEADS, LANES, tq), BF16)],
            [pltpu.VMEM((ATTN_GROUP, s, tq), F32)] * 2 + [pltpu.VMEM((ACC_ROWS, tq), F32)] * ATTN_GROUP)


def _resident_spec(shape):
    nd = len(shape)
    return pl.BlockSpec(shape, lambda bi, i: (bi,) + (0,) * (nd - 1), pipeline_mode=pl.Buffered(1))


def _moba_attention(qa_t, ka, va_t, km):
    b, n_blk, w, tq = qa_t.shape
    s = n_blk * tq
    qh, work = _attention_scratch(s, tq)
    return pl.pallas_call(
        _moba_kernel,
        grid=(b, n_blk),
        in_specs=[
            pl.BlockSpec((1, 1, w, tq), lambda bi, i: (bi, i, 0, 0)),
            _resident_spec((1, s, w)),
            _resident_spec((1, n_blk, w, tq)),
            _resident_spec((1, n_blk, w)),
        ],
        out_specs=pl.BlockSpec((1, 1, w, tq), lambda bi, i: (bi, i, 0, 0)),
        out_shape=jax.ShapeDtypeStruct((b, n_blk, w, tq), BF16),
        scratch_shapes=qh + [pltpu.VMEM((N_HEADS, n_blk, tq), F32)] + work,
        compiler_params=_cparams(("parallel", "arbitrary")),
        name="moba_attention",
    )(qa_t, ka, va_t, km)


INT_MIN = -2 ** 31


def _sortable_key(s):
    bits = lax.bitcast_convert_type(s, I32)
    return bits ^ ((bits >> 31) & 0x7FFFFFFF)


def _column_count(mask):
    tk, tq = mask.shape
    return jnp.sum(mask.astype(I32).reshape(tk // SUBLANES, SUBLANES, tq), axis=0)


I16 = jnp.int16
HALF_BITS = 16
I16_MIN = -2 ** (HALF_BITS - 1)
PACKED_ROWS = 2 * SUBLANES
SEARCH_GROUP = 4


def _packed_column_count(mask):
    ones = mask.astype(I16)
    parts = [ones[r:r + PACKED_ROWS, :] for r in range(0, ones.shape[0], PACKED_ROWS)]
    while len(parts) > 1:
        parts = [a + b for a, b in zip(parts[::2], parts[1::2])] + parts[len(parts) - len(parts) % 2:]
    return parts[0]


def _packed_total(count16):
    return jnp.sum(count16.astype(I32), axis=0, keepdims=True)


def _dsa_kernel(iq_t_ref, ik_ref, iw_t_ref, q_t_ref, k_ref, v_t_ref, o_ref,
                key_ref, hi_ref, lo_ref, lo2_ref, bias_ref, qh_ref, s_even_ref, s_odd_ref, *acc_refs,
                topk, index_bits):
    s_refs = (s_even_ref, s_odd_ref)
    t = pl.program_id(1)
    tq = q_t_ref.shape[3]
    tk = tq
    n_chunks = t + 1
    qpos = t * tq + lax.broadcasted_iota(I32, (1, tq), 1)
    krow = lax.broadcasted_iota(I32, (tk, tq), 0)

    def chunk_rows(c):
        return pl.ds(pl.multiple_of(c * tk, tk), tk)

    def total(count8):
        return jnp.sum(count8, axis=0, keepdims=True)

    def count_over_chunks(pred):
        def body(c, cnt):
            return cnt + _column_count(pred(key_ref[chunk_rows(c), :], c * tk + krow))
        return total(lax.fori_loop(0, n_chunks, body, jnp.zeros((SUBLANES, tq), I32)))

    iq = iq_t_ref[0, 0]
    iw = iw_t_ref[0, 0]

    def score_chunks(c0, count, carry):
        for c in (c0 + d for d in range(count)):
            ik_c = ik_ref[0, chunk_rows(c), :]
            score = jnp.zeros((tk, tq), F32)
            for h in range(IDX_HEADS):
                rel = jnp.dot(ik_c, iq[h * IDX_DIM:(h + 1) * IDX_DIM, :], preferred_element_type=F32)
                score = score + jnp.maximum(rel, 0.0) * iw[h:h + 1, :]
            key = jnp.where(c * tk + krow <= qpos, _sortable_key(score), INT_MIN)
            key_ref[chunk_rows(c), :] = key
            hi_ref[chunk_rows(c), :] = (key >> HALF_BITS).astype(I16)
            lo_ref[chunk_rows(c), :] = key.astype(I16) ^ jnp.int16(I16_MIN)
        return carry

    _loop_in_groups(n_chunks, score_chunks, 0)

    n_groups = (n_chunks + SEARCH_GROUP - 1) // SEARCH_GROUP
    for d in range(SEARCH_GROUP - 1):
        @pl.when(n_chunks + d < n_groups * SEARCH_GROUP)
        def _(d=d):
            lowest = jnp.full((tk, tq), I16_MIN, I16)
            hi_ref[chunk_rows(n_chunks + d), :] = lowest
            lo_ref[chunk_rows(n_chunks + d), :] = lowest

    def packed_counts(n_counts, chunk_masks):
        def body(g, cnts):
            cnts = list(cnts)
            for c in (g * SEARCH_GROUP + d for d in range(SEARCH_GROUP)):
                for i, mask in enumerate(chunk_masks(c)):
                    cnts[i] = cnts[i] + _packed_column_count(mask)
            return tuple(cnts)
        zero = jnp.zeros((PACKED_ROWS, tq), I16)
        return [_packed_total(c) for c in lax.fori_loop(0, n_groups, body, (zero,) * n_counts)]

    def kth_largest_half(half_ref, kth):
        def half_bit(b, prefix):
            cand = prefix | jnp.left_shift(jnp.int32(1), HALF_BITS - 1 - b)
            cand_half = (cand + I16_MIN).astype(I16)
            n_ge, = packed_counts(1, lambda c: [half_ref[chunk_rows(c), :] >= cand_half])
            return jnp.where(n_ge >= kth, cand, prefix)

        return lax.fori_loop(0, HALF_BITS, half_bit, jnp.zeros((1, tq), I32)) + I16_MIN

    thr_hi = kth_largest_half(hi_ref, topk)
    thr_hi_half = thr_hi.astype(I16)

    def above_hi(c):
        hi = hi_ref[chunk_rows(c), :]
        lo2_ref[chunk_rows(c), :] = jnp.where(hi == thr_hi_half, lo_ref[chunk_rows(c), :], jnp.int16(I16_MIN))
        return [hi > thr_hi_half]

    n_gt_hi, = packed_counts(1, above_hi)
    thr_lo = kth_largest_half(lo2_ref, topk - n_gt_hi)
    thr_lo_half = thr_lo.astype(I16)
    thr = jnp.left_shift(thr_hi, HALF_BITS) | (thr_lo - I16_MIN)

    n_ge_lo, = packed_counts(1, lambda c: [lo2_ref[chunk_rows(c), :] >= thr_lo_half])
    has_surplus = jnp.any((n_gt_hi + n_ge_lo > topk) & (thr != INT_MIN))

    def last_tie_index(need):
        def index_bit(b, prefix):
            cand = prefix | jnp.left_shift(jnp.int32(1), index_bits - 1 - b)
            n_before = count_over_chunks(lambda keys, kpos: (keys == thr) & (kpos < cand))
            return jnp.where(n_before < need, cand, prefix)
        return lax.fori_loop(0, index_bits, index_bit, jnp.zeros((1, tq), I32))

    def bias_with_ties():
        n_gt_lo, = packed_counts(1, lambda c: [lo2_ref[chunk_rows(c), :] > thr_lo_half])
        tie_end = last_tie_index(topk - (n_gt_hi + n_gt_lo))

        def bias_chunk(c, carry):
            keys = key_ref[chunk_rows(c), :]
            kpos = c * tk + krow
            chosen = ((keys > thr) | ((keys == thr) & (kpos <= tie_end))) & (kpos <= qpos)
            bias_ref[chunk_rows(c), :] = jnp.where(chosen, 0.0, MASKED)
            return carry

        lax.fori_loop(0, n_chunks, bias_chunk, 0)

    def bias_without_ties():
        floor = jnp.maximum(thr, INT_MIN + 1)

        def bias_chunks(c0, count, carry):
            for c in (c0 + d for d in range(count)):
                bias_ref[chunk_rows(c), :] = jnp.where(key_ref[chunk_rows(c), :] >= floor, 0.0, MASKED)
            return carry

        _loop_in_groups(n_chunks, bias_chunks, 0)

    lax.cond(has_surplus, bias_with_ties, bias_without_ties)

    _split_heads(q_t_ref[0, 0], qh_ref)

    def masked_scores(g, c, mx):
        for gi, h in _group_heads(g):
            s = jnp.dot(k_ref[0, chunk_rows(c), _head_lanes(h)], qh_ref[h], preferred_element_type=F32)
            s = s + bias_ref[chunk_rows(c), :]
            s_refs[g % 2][gi, chunk_rows(c), :] = s
            mx[gi] = jnp.maximum(mx[gi], jnp.max(s, axis=0, keepdims=True))

    def weighted(g, gi, h, c, mx):
        return _weighted_values(s_refs[g % 2][gi, chunk_rows(c), :], mx[gi], v_t_ref[0, c, _head_rows(h), :])

    def finish(g, mx):
        for gi, h in _group_heads(g):
            _store_head_output(o_ref, h, acc_refs[gi][...])

    _attention_pipeline(n_chunks, lambda g: (jnp.full((1, tq), MASKED, F32),) * ATTN_GROUP,
                        masked_scores, weighted, finish, acc_refs)


def _dsa_attention(iq_t, ik, iw_t, qb_t, kb, vb_t):
    b, n_blk, w, tq = qb_t.shape
    s = n_blk * tq
    topk = min(DSA_TOPK_MAX, s // 4)
    assert n_blk % SEARCH_GROUP == 0, "the packed search pads its chunk range to whole groups"
    qh, work = _attention_scratch(s, tq)
    tile_spec = lambda rows: pl.BlockSpec((1, 1, rows, tq), lambda bi, ti: (bi, ti, 0, 0))
    return pl.pallas_call(
        functools.partial(_dsa_kernel, topk=topk, index_bits=max(1, (s - 1).bit_length())),
        grid=(b, n_blk),
        in_specs=[
            tile_spec(IDX_HEADS * IDX_DIM),
            _resident_spec((1, s, IDX_DIM)),
            tile_spec(IDX_HEADS),
            tile_spec(w),
            _resident_spec((1, s, w)),
            _resident_spec((1, n_blk, w, tq)),
        ],
        out_specs=pl.BlockSpec((1, 1, w, tq), lambda bi, ti: (bi, ti, 0, 0)),
        out_shape=jax.ShapeDtypeStruct((b, n_blk, w, tq), BF16),
        scratch_shapes=[pltpu.VMEM((s, tq), I32)] + [pltpu.VMEM((s, tq), I16)] * 3
        + [pltpu.VMEM((s, tq), F32)] + qh + work,
        compiler_params=_cparams(("parallel", "arbitrary")),
        name="dsa_attention",
    )(iq_t, ik, iw_t, qb_t, kb, vb_t)


EXPERT_ROW0 = SUBLANES
ROUTER_ROWS = EXPERT_ROW0 + N_EXPERTS
EXPERT_TOPK = 2


def _rms_norm(x, g):
    ms = jnp.mean(x * x, axis=-1, keepdims=True)
    return (x * lax.rsqrt(ms + RMS_EPS)) * g


HIGH_HALF = -2 ** 16


def _pack_bf16_halves(x):
    bits = lax.bitcast_convert_type(x.astype(BF16).astype(F32), I32)
    half = x.shape[1] // 2
    return (bits[:, :half] & HIGH_HALF) | lax.shift_right_logical(bits[:, half:], 16)


def _unpack_bf16_halves(packed):
    upper = lax.bitcast_convert_type(packed & HIGH_HALF, F32)
    lower = lax.bitcast_convert_type(lax.shift_left(packed, 16), F32)
    return jnp.concatenate([upper, lower], axis=1).astype(BF16)


def _first_index_of_max(vals, idx):
    top = jnp.max(vals, axis=0, keepdims=True)
    first = jnp.min(jnp.where(vals == top, idx, vals.shape[0]), axis=0, keepdims=True)
    return top, first


def _merge_router_kernel(oa_ref, ob_ref, ga_ref, gb_ref, x_ref, wa_ref, wb_ref, wo_ref, g_ref,
                         wr_ref, br_ref, x1_ref, h2_ref, ids_ref, cw_ref, rank_ref, cnt_ref):
    @pl.when(pl.program_id(0) == 0)
    def _():
        cnt_ref[...] = jnp.zeros_like(cnt_ref)

    def branch(o_t_ref, w_ref):
        return jnp.concatenate([lax.dot_general(o_t_ref[c], w_ref[...], TN_DIMS, preferred_element_type=F32)
                                for c in range(o_t_ref.shape[0])], axis=0)

    a = branch(oa_ref, wa_ref)
    b = branch(ob_ref, wb_ref)
    merged = jax.nn.sigmoid(ga_ref[...]) * a + jax.nn.sigmoid(gb_ref[...]) * b
    x1 = x_ref[...] + jnp.dot(merged.astype(BF16), wo_ref[...], preferred_element_type=F32)
    x1_ref[...] = x1
    h2 = _rms_norm(x1, g_ref[...])
    h2_ref[...] = _pack_bf16_halves(h2)
    tm = h2.shape[0]

    logits = lax.dot_general(wr_ref[...], h2, NT_DIMS, preferred_element_type=F32,
                             precision=lax.Precision.HIGHEST) + br_ref[...]
    grp = logits[0:N_GROUPS, :]
    g_top, g_idx = _first_index_of_max(grp, lax.broadcasted_iota(I32, grp.shape, 0))
    p_grp = 1.0 / jnp.sum(jnp.exp(grp - g_top), axis=0, keepdims=True)
    in_grp = logits[EXPERT_ROW0:EXPERT_ROW0 + EXPERTS_PER_GROUP, :]
    for gi in range(1, N_GROUPS):
        rows = slice(EXPERT_ROW0 + gi * EXPERTS_PER_GROUP, EXPERT_ROW0 + (gi + 1) * EXPERTS_PER_GROUP)
        in_grp = jnp.where(g_idx == gi, logits[rows, :], in_grp)
    e_iota = lax.broadcasted_iota(I32, in_grp.shape, 0)
    v0, i0 = _first_index_of_max(in_grp, e_iota)
    v1, i1 = _first_index_of_max(jnp.where(e_iota == i0, -jnp.inf, in_grp), e_iota)
    e1 = jnp.exp(v1 - v0)
    denom = 1.0 + e1
    ids = jnp.concatenate([g_idx * EXPERTS_PER_GROUP + i0, g_idx * EXPERTS_PER_GROUP + i1], axis=0)
    ids_ref[0] = ids
    cw_ref[0] = jnp.concatenate([p_grp * (1.0 / denom), p_grp * (e1 / denom)], axis=0)

    before = (lax.broadcasted_iota(I32, (tm, tm), 0) < lax.broadcasted_iota(I32, (tm, tm), 1)).astype(BF16)
    expert = lax.broadcasted_iota(I32, (N_EXPERTS, tm), 0)
    ranks = []
    for slot in range(EXPERT_TOPK):
        onehot = expert == ids[slot:slot + 1, :]
        seen = jnp.dot(onehot.astype(BF16), before, preferred_element_type=F32) + cnt_ref[...]
        ranks.append(jnp.sum(jnp.where(onehot, seen, 0.0), axis=0, keepdims=True))
        cnt_ref[...] += jnp.sum(onehot.astype(F32), axis=1, keepdims=True)
    rank_ref[0] = jnp.concatenate(ranks, axis=0).astype(I32)


def _merge_and_route(o_a, o_b, ga, gb, x, w_br_a, w_br_b, w_out, g_ffn, w_grp, b_grp, w_rt, b_rt, tm=512):
    t, d = x.shape
    n_tiles = t // tm
    wr = jnp.zeros((ROUTER_ROWS, d), F32).at[0:N_GROUPS].set(w_grp.T).at[EXPERT_ROW0:].set(w_rt.T)
    br = jnp.zeros((ROUTER_ROWS, 1), F32).at[0:N_GROUPS, 0].set(b_grp).at[EXPERT_ROW0:, 0].set(b_rt)
    row_spec = lambda w: pl.BlockSpec((tm, w), lambda i: (i, 0))
    mixer_spec = pl.BlockSpec((tm // Q_TILE, MIXER_W, Q_TILE), lambda i: (i, 0, 0))
    slot_spec = pl.BlockSpec((1, EXPERT_TOPK, tm), lambda i: (i, 0, 0))
    slot_shape = lambda dt: jax.ShapeDtypeStruct((n_tiles, EXPERT_TOPK, tm), dt)
    return pl.pallas_call(
        _merge_router_kernel,
        grid=(n_tiles,),
        in_specs=[mixer_spec, mixer_spec, row_spec(d), row_spec(d), row_spec(d),
                  _const_spec((MIXER_W, d)), _const_spec((MIXER_W, d)), _const_spec((d, d)),
                  _const_spec((1, d)), _const_spec((ROUTER_ROWS, d)), _const_spec((ROUTER_ROWS, 1))],
        out_specs=[row_spec(d), row_spec(d // 2), slot_spec, slot_spec, slot_spec,
                   pl.BlockSpec((N_EXPERTS, 1), lambda i: (0, 0))],
        out_shape=[jax.ShapeDtypeStruct((t, d), F32), jax.ShapeDtypeStruct((t, d // 2), I32),
                   slot_shape(I32), slot_shape(F32), slot_shape(I32),
                   jax.ShapeDtypeStruct((N_EXPERTS, 1), F32)],
        compiler_params=_cparams(("arbitrary",)),
        name="merge_and_route",
    )(o_a, o_b, ga, gb, x, w_br_a.astype(BF16), w_br_b.astype(BF16), w_out.astype(BF16),
      g_ffn.reshape(1, d), wr, br)


EXPERT_ROWS = 512
DMA_ISSUE_UNROLL = 8


def _row(ref, r):
    return ref.at[pl.ds(r, 1), :]


def _tile_dest(dest_ref, tm, slot, r):
    return dest_ref[(pl.program_id(0) * EXPERT_TOPK + slot) * tm + r]


def _dispatch_kernel(dest_ref, pad_block_ref, h2_ref, xs_hbm, zero_ref, sem, zero_sem):
    tm = h2_ref.shape[0]

    @pl.when(pl.program_id(0) == 0)
    def _():
        zero_ref[...] = jnp.zeros(zero_ref.shape, zero_ref.dtype)

        def block_copy(b):
            rows = pl.ds(pl.multiple_of(b * EXPERT_ROWS, EXPERT_ROWS), EXPERT_ROWS)
            return pltpu.make_async_copy(zero_ref, xs_hbm.at[rows, :], zero_sem)

        def start(b, n):
            @pl.when(pad_block_ref[b] != 0)
            def _():
                block_copy(b).start()
            return n + pad_block_ref[b]

        n_started = lax.fori_loop(0, pad_block_ref.shape[0], start, 0)

        def drain(k, carry):
            block_copy(0).wait()
            return carry

        lax.fori_loop(0, n_started, drain, 0)

    for r in range(tm):
        for slot in range(EXPERT_TOPK):
            pltpu.make_async_copy(_row(h2_ref, r), _row(xs_hbm, _tile_dest(dest_ref, tm, slot, r)), sem).start()
    for slot in range(EXPERT_TOPK):
        pltpu.make_async_copy(h2_ref, xs_hbm.at[pl.ds(0, tm), :], sem).wait()


def _dispatch(dest, pad_block, h2, tm):
    t, d = h2.shape
    n_rows = pad_block.shape[0] * EXPERT_ROWS
    return pl.pallas_call(
        _dispatch_kernel,
        grid_spec=pltpu.PrefetchScalarGridSpec(
            num_scalar_prefetch=2,
            grid=(t // tm,),
            in_specs=[pl.BlockSpec((tm, d), lambda i, dest, pad: (i, 0))],
            out_specs=pl.BlockSpec(memory_space=pl.ANY),
            scratch_shapes=[pltpu.VMEM((EXPERT_ROWS, d), h2.dtype), pltpu.SemaphoreType.DMA(()),
                            pltpu.SemaphoreType.DMA(())],
        ),
        out_shape=jax.ShapeDtypeStruct((n_rows, d), h2.dtype),
        compiler_params=_cparams(("arbitrary",)),
        name="moe_dispatch",
    )(dest, pad_block, h2)


def _expert_kernel(blk_exp_ref, n_used_ref, xs_ref, wg_ref, wu_ref, wd_ref, ys_ref, wg16, wu16, wd16):
    i = pl.program_id(0)
    used = i < n_used_ref[0]

    @pl.when((i == 0) | (blk_exp_ref[i] != blk_exp_ref[jnp.maximum(i - 1, 0)]))
    def _():
        wg16[...] = wg_ref[0].astype(BF16)
        wu16[...] = wu_ref[0].astype(BF16)
        wd16[...] = wd_ref[0].astype(BF16)

    @pl.when(used)
    def _():
        xb = _unpack_bf16_halves(xs_ref[...])
        gate = jnp.dot(xb, wg16[...], preferred_element_type=F32)
        up = jnp.dot(xb, wu16[...], preferred_element_type=F32)
        hid = (gate * jax.nn.sigmoid(gate)) * up
        ys_ref[...] = jnp.dot(hid.astype(BF16), wd16[...], preferred_element_type=F32)

    @pl.when(jnp.logical_not(used))
    def _():
        ys_ref[...] = jnp.zeros_like(ys_ref)


def _experts(blk_exp, n_used, xs, w_gate, w_up, w_down):
    n_rows, d = xs.shape[0], w_gate.shape[1]
    n_blocks = n_rows // EXPERT_ROWS
    expert_spec = lambda rows, cols: pl.BlockSpec((1, rows, cols), lambda i, be, nu: (be[i], 0, 0))
    return pl.pallas_call(
        _expert_kernel,
        grid_spec=pltpu.PrefetchScalarGridSpec(
            num_scalar_prefetch=2,
            grid=(n_blocks,),
            in_specs=[pl.BlockSpec((EXPERT_ROWS, xs.shape[1]), lambda i, be, nu: (i, 0)),
                      expert_spec(d, D_EXPERT), expert_spec(d, D_EXPERT), expert_spec(D_EXPERT, d)],
            out_specs=pl.BlockSpec((EXPERT_ROWS, d), lambda i, be, nu: (i, 0)),
            scratch_shapes=[pltpu.VMEM((d, D_EXPERT), BF16), pltpu.VMEM((d, D_EXPERT), BF16),
                            pltpu.VMEM((D_EXPERT, d), BF16)],
        ),
        out_shape=jax.ShapeDtypeStruct((n_rows, d), F32),
        compiler_params=_cparams(("arbitrary",)),
        name="moe_experts",
    )(blk_exp, n_used, xs, w_gate, w_up, w_down)


def _combine_ple_kernel(dest_ref, x1_ref, cw_ref, p_ref, ys_hbm, g_ple_ref, wpg_ref, wpp_ref, g_fin_ref,
                        out_ref, y_even, y_odd, sem):
    tm = y_even.shape[1]
    i = pl.program_id(0)
    last_tile = 2 * pl.num_programs(0) - 1
    bufs = ((y_even, 0), (y_odd, 1))

    def start_row(tile, buf, sem_slot, r):
        for slot in range(EXPERT_TOPK):
            src = _row(ys_hbm, dest_ref[(tile * EXPERT_TOPK + slot) * tm + r])
            pltpu.make_async_copy(src, _row(buf.at[slot], r), sem.at[sem_slot]).start()

    def wait_tile(buf, sem_slot):
        for slot in range(EXPERT_TOPK):
            pltpu.make_async_copy(ys_hbm.at[pl.ds(0, tm), :], buf.at[slot], sem.at[sem_slot]).wait()

    def combine(half, buf):
        rows = slice(half * tm, (half + 1) * tm)
        cw = cw_ref[rows, :]
        x2 = x1_ref[rows, :] + (buf[0] * cw[:, 0:1] + buf[1] * cw[:, 1:2])
        h3 = _rms_norm(x2, g_ple_ref[...]).astype(BF16)
        gate = jax.nn.sigmoid(jnp.dot(h3, wpg_ref[...], preferred_element_type=F32))
        proj = jnp.dot(p_ref[rows, :].astype(BF16), wpp_ref[...], preferred_element_type=F32)
        out_ref[rows, :] = _rms_norm(x2 + gate * proj, g_fin_ref[...])

    @pl.when(i == 0)
    def _():
        def issue(r, carry):
            start_row(0, *bufs[0], r)
            return carry
        lax.fori_loop(0, tm, issue, 0, unroll=DMA_ISSUE_UNROLL)

    for half in range(2):
        buf, sem_slot = bufs[half]
        wait_tile(buf, sem_slot)
        combine(half, buf)
        next_tile = jnp.minimum(2 * i + half + 1, last_tile)
        for r in range(tm):
            start_row(next_tile, *bufs[1 - half], r)

    @pl.when(i == pl.num_programs(0) - 1)
    def _():
        wait_tile(*bufs[0])


def _combine_ple(dest, x1, cw_tok, p, ys, g_ple, w_ple_gate, w_ple_proj, g_final, tm):
    t, d = x1.shape
    assert t % (2 * tm) == 0
    row_spec = lambda w: pl.BlockSpec((2 * tm, w), lambda i, dest: (i, 0))
    return pl.pallas_call(
        _combine_ple_kernel,
        grid_spec=pltpu.PrefetchScalarGridSpec(
            num_scalar_prefetch=1,
            grid=(t // (2 * tm),),
            in_specs=[row_spec(d), row_spec(EXPERT_TOPK), row_spec(p.shape[1]),
                      pl.BlockSpec(memory_space=pl.ANY),
                      _const_spec((1, d)), _const_spec((d, d)), _const_spec((p.shape[1], d)), _const_spec((1, d))],
            out_specs=row_spec(d),
            scratch_shapes=[pltpu.VMEM((EXPERT_TOPK, tm, d), F32), pltpu.VMEM((EXPERT_TOPK, tm, d), F32),
                            pltpu.SemaphoreType.DMA((2,))],
        ),
        out_shape=jax.ShapeDtypeStruct((t, d), F32),
        compiler_params=_cparams(("arbitrary",)),
        name="combine_ple",
    )(dest, x1, cw_tok, p, ys, g_ple.reshape(1, d), w_ple_gate.astype(BF16), w_ple_proj.astype(BF16),
      g_final.reshape(1, d))


def _layer(x, p, g_attn, w_in, w_br_a, w_br_b, w_out, g_ffn, w_grp, b_grp, w_rt, b_rt,
           w_gate, w_up, w_down, g_ple, w_ple_gate, w_ple_proj, g_final):
    b, s, d = x.shape
    t = b * s
    (qa_t, va_t, qb_t, vb_t, iq_t, iw_t, ka, kb, ik, ga, gb, km) = _in_projection(x, g_attn, w_in)
    o_a = _moba_attention(qa_t, ka, va_t, km.reshape(b, s // MOBA_BLOCK, MIXER_W))
    o_b = _dsa_attention(iq_t, ik, iw_t, qb_t, kb, vb_t)
    x1, h2, ids, cw, rank, counts = _merge_and_route(
        o_a.reshape(t // Q_TILE, MIXER_W, Q_TILE), o_b.reshape(t // Q_TILE, MIXER_W, Q_TILE),
        ga.reshape(t, d), gb.reshape(t, d),
        x.reshape(t, d), w_br_a, w_br_b, w_out, g_ffn, w_grp, b_grp, w_rt, b_rt)

    counts = counts[:, 0].astype(I32)
    padded = ((counts + EXPERT_ROWS - 1) // EXPERT_ROWS) * EXPERT_ROWS
    pend = jnp.cumsum(padded)
    pstart = pend - padded
    n_blocks = -(-(t * EXPERT_TOPK) // EXPERT_ROWS) + N_EXPERTS
    experts = jnp.arange(N_EXPERTS, dtype=I32)
    dest = jnp.sum(jnp.where(ids[..., None] == experts, pstart, 0), axis=-1) + rank
    block_row0 = jnp.arange(n_blocks, dtype=I32) * EXPERT_ROWS
    blk_exp = jnp.minimum(jnp.sum((pend[None, :] <= block_row0[:, None]).astype(I32), axis=1), N_EXPERTS - 1)
    n_used = (pend[-1:] // EXPERT_ROWS).astype(I32)

    route_tile = dest.shape[2]
    dest = dest.reshape(-1)
    is_last_of_expert = jnp.any(pend[None, :] == (block_row0 + EXPERT_ROWS)[:, None], axis=1)
    pad_block = (is_last_of_expert | (block_row0 >= pend[-1])).astype(I32)
    xs = _dispatch(dest, pad_block, h2, route_tile)
    ys = _experts(blk_exp, n_used, xs, w_gate, w_up, w_down)
    cw_tok = jnp.swapaxes(cw, 1, 2).reshape(t, EXPERT_TOPK)
    out = _combine_ple(dest, x1, cw_tok, p.reshape(t, p.shape[-1]), ys, g_ple, w_ple_gate, w_ple_proj, g_final,
                       route_tile)
    return out.reshape(b, s, d)


def kernel(x, p, g_attn, w_in, w_br_a, w_br_b, w_out, g_ffn, w_grp, b_grp, w_rt, b_rt, w_gate, w_up, w_down, g_ple, w_ple_gate, w_ple_proj, g_final):
    depth = w_in.shape[0]
    assert depth == 1, "the final RMSNorm is fused into the last layer's kernel"
    i = 0
    return _layer(x, p[i], g_attn[i], w_in[i], w_br_a[i], w_br_b[i], w_out[i], g_ffn[i], w_grp[i], b_grp[i],
                  w_rt[i], b_rt[i], w_gate[i], w_up[i], w_down[i], g_ple[i], w_ple_gate[i], w_ple_proj[i], g_final)
```

```python
import functools

import jax
import jax.numpy as jnp
from jax import lax
from jax.experimental import pallas as pl
from jax.experimental.pallas import tpu as pltpu

F32 = jnp.float32
BF16 = jnp.bfloat16
I32 = jnp.int32

HEAD_DIM = 64
N_HEADS = 8
ROT_DIM = HEAD_DIM // 4
ROT_HALF = ROT_DIM // 2
ROPE_THETA = 500000.0
MOBA_BLOCK = 256
MOBA_TOPK = 3
IDX_HEADS = 8
IDX_DIM = 64
DSA_TOPK_MAX = 256
N_GROUPS = 4
EXPERTS_PER_GROUP = 8
N_EXPERTS = N_GROUPS * EXPERTS_PER_GROUP
D_EXPERT = 512
PLE_DIM = 256
RMS_EPS = 1e-6
MIXER_W = N_HEADS * HEAD_DIM

Q_TILE = 256
LANES = 128
SUBLANES = 8
VMEM_LIMIT = 56 * 1024 * 1024

NT_DIMS = (((1,), (1,)), ((), ()))
TN_DIMS = (((0,), (0,)), ((), ()))


def _cparams(sem):
    return pltpu.CompilerParams(dimension_semantics=sem, vmem_limit_bytes=VMEM_LIMIT)


def _const_spec(shape):
    nd = len(shape)
    return pl.BlockSpec(shape, lambda *_: (0,) * nd, pipeline_mode=pl.Buffered(1))


def _rope_feature_major(z, cos_t, sin_t):
    tm = z.shape[1]
    z3 = z.reshape(N_HEADS, HEAD_DIM, tm)
    x1 = z3[:, 0:ROT_HALF, :]
    x2 = z3[:, ROT_HALF:ROT_DIM, :]
    o1 = x1 * cos_t - x2 * sin_t
    o2 = x2 * cos_t + x1 * sin_t
    return jnp.concatenate([o1, o2, z3[:, ROT_DIM:, :]], axis=1).reshape(N_HEADS * HEAD_DIM, tm)


def _rope_token_major(z, c_tab, s_lo, s_hi):
    up = pltpu.roll(z, LANES - ROT_HALF, 1)
    dn = pltpu.roll(z, ROT_HALF, 1)
    return z * c_tab + up * s_lo + dn * s_hi


def _store_blocked(ref, z):
    for c in range(z.shape[1] // Q_TILE):
        ref[0, c] = z[:, c * Q_TILE:(c + 1) * Q_TILE]


def _inproj_kernel(x_ref, g_ref, wf_ref, wiw_ref, wt_ref, wg_ref, cos_t_ref, sin_t_ref,
                   ctab_ref, slo_ref, shi_ref,
                   qa_t_ref, va_t_ref, qb_t_ref, vb_t_ref, iq_t_ref, iw_t_ref,
                   ka_ref, kb_ref, ik_ref, ga_ref, gb_ref, km_ref, *, w_scale):
    x = x_ref[0]
    ms = jnp.mean(x * x, axis=-1, keepdims=True)
    h = ((x * lax.rsqrt(ms + RMS_EPS)) * g_ref[...]).astype(BF16)
    cos_t = cos_t_ref[...]
    sin_t = sin_t_ref[...]
    q_scale = HEAD_DIM ** -0.5

    fm_outs = ((qa_t_ref, True, q_scale), (va_t_ref, False, 1.0), (qb_t_ref, True, q_scale),
               (vb_t_ref, False, 1.0), (iq_t_ref, True, IDX_DIM ** -0.5))
    for i, (ref, rope, scale) in enumerate(fm_outs):
        z = lax.dot_general(wf_ref[i * MIXER_W:(i + 1) * MIXER_W, :], h, NT_DIMS,
                            preferred_element_type=F32)
        if rope:
            z = _rope_feature_major(z, cos_t, sin_t)
        if scale != 1.0:
            z = z * scale
        _store_blocked(ref, z.astype(BF16))

    iw = lax.dot_general(wiw_ref[...], h, NT_DIMS, preferred_element_type=F32)
    _store_blocked(iw_t_ref, iw * w_scale)

    zt = jnp.dot(h, wt_ref[...], preferred_element_type=F32)
    ctab, slo, shi = ctab_ref[...], slo_ref[...], shi_ref[...]
    n_grp = zt.shape[1] // LANES
    roped = [_rope_token_major(zt[:, j * LANES:(j + 1) * LANES], ctab, slo, shi) for j in range(n_grp)]
    per_mixer = MIXER_W // LANES
    ka = jnp.concatenate(roped[:per_mixer], axis=1)
    kb = jnp.concatenate(roped[per_mixer:2 * per_mixer], axis=1)
    ka_ref[0] = ka.astype(BF16)
    kb_ref[0] = kb.astype(BF16)
    ik_ref[0] = roped[2 * per_mixer][:, :IDX_DIM].astype(BF16)
    tm = ka.shape[0]
    km_ref[0] = jnp.mean(ka.reshape(tm // MOBA_BLOCK, MOBA_BLOCK, MIXER_W), axis=1, keepdims=True)

    zg = jnp.dot(h, wg_ref[...], preferred_element_type=F32)
    d_model = zg.shape[1] // 2
    ga_ref[0] = zg[:, :d_model]
    gb_ref[0] = zg[:, d_model:]


def _rope_tables(seq):
    inv = 1.0 / (ROPE_THETA ** (jnp.arange(0, ROT_DIM, 2, dtype=F32) / ROT_DIM))
    ang = jnp.arange(seq, dtype=F32)[:, None] * inv[None, :]
    cos, sin = jnp.cos(ang), jnp.sin(ang)
    d = jnp.arange(LANES) % HEAD_DIM
    lo = d < ROT_HALF
    hi = (d >= ROT_HALF) & (d < ROT_DIM)
    f = d % ROT_HALF
    cos_l, sin_l = cos[:, f], sin[:, f]
    ctab = jnp.where(lo | hi, cos_l, 1.0)
    slo = jnp.where(lo, -sin_l, 0.0)
    shi = jnp.where(hi, sin_l, 0.0)
    return cos.T, sin.T, ctab, slo, shi


def _in_projection(x, g_attn, w_in, tm=512):
    b, s, d = x.shape
    splits = (MIXER_W,) * 6 + (IDX_HEADS * IDX_DIM, IDX_DIM, IDX_HEADS, d, d)
    offs = [0]
    for w in splits:
        offs.append(offs[-1] + w)
    wqa, wka, wva, wqb, wkb, wvb, wiq, wik, wiw, wga, wgb = (
        w_in[:, offs[i]:offs[i + 1]] for i in range(len(splits)))
    wf = jnp.concatenate([wqa, wva, wqb, wvb, wiq], axis=1).T.astype(BF16)
    wiw_t = wiw.T.astype(BF16)
    wt = jnp.concatenate([wka, wkb, wik, jnp.zeros((d, LANES - IDX_DIM), w_in.dtype)], axis=1).astype(BF16)
    wg = jnp.concatenate([wga, wgb], axis=1).astype(BF16)
    cos_t, sin_t, ctab, slo, shi = _rope_tables(s)
    n_blk = s // MOBA_BLOCK
    fm_shape = jax.ShapeDtypeStruct((b, s // Q_TILE, MIXER_W, Q_TILE), BF16)
    tok_shape = jax.ShapeDtypeStruct((b, s, MIXER_W), BF16)
    fm_spec = pl.BlockSpec((1, tm // Q_TILE, MIXER_W, Q_TILE), lambda bi, ti: (bi, ti, 0, 0))
    tok_spec = pl.BlockSpec((1, tm, MIXER_W), lambda bi, ti: (bi, ti, 0))
    gate_spec = pl.BlockSpec((1, tm, d), lambda bi, ti: (bi, ti, 0))
    outs = pl.pallas_call(
        functools.partial(_inproj_kernel, w_scale=IDX_HEADS ** -0.5),
        grid=(b, s // tm),
        in_specs=[
            pl.BlockSpec((1, tm, d), lambda bi, ti: (bi, ti, 0)),
            _const_spec((1, d)),
            _const_spec(wf.shape), _const_spec(wiw_t.shape), _const_spec(wt.shape), _const_spec(wg.shape),
            pl.BlockSpec((ROT_HALF, tm), lambda bi, ti: (0, ti)),
            pl.BlockSpec((ROT_HALF, tm), lambda bi, ti: (0, ti)),
            pl.BlockSpec((tm, LANES), lambda bi, ti: (ti, 0)),
            pl.BlockSpec((tm, LANES), lambda bi, ti: (ti, 0)),
            pl.BlockSpec((tm, LANES), lambda bi, ti: (ti, 0)),
        ],
        out_specs=[
            fm_spec, fm_spec, fm_spec, fm_spec, fm_spec,
            pl.BlockSpec((1, tm // Q_TILE, IDX_HEADS, Q_TILE), lambda bi, ti: (bi, ti, 0, 0)),
            tok_spec, tok_spec,
            pl.BlockSpec((1, tm, IDX_DIM), lambda bi, ti: (bi, ti, 0)),
            gate_spec, gate_spec,
            pl.BlockSpec((1, tm // MOBA_BLOCK, 1, MIXER_W), lambda bi, ti: (bi, ti, 0, 0)),
        ],
        out_shape=[
            fm_shape, fm_shape, fm_shape, fm_shape, fm_shape,
            jax.ShapeDtypeStruct((b, s // Q_TILE, IDX_HEADS, Q_TILE), F32),
            tok_shape, tok_shape,
            jax.ShapeDtypeStruct((b, s, IDX_DIM), BF16),
            jax.ShapeDtypeStruct((b, s, d), F32), jax.ShapeDtypeStruct((b, s, d), F32),
            jax.ShapeDtypeStruct((b, n_blk, 1, MIXER_W), F32),
        ],
        compiler_params=_cparams(("parallel", "parallel")),
        name="in_projection",
    )(x, g_attn.reshape(1, d), wf, wiw_t, wt, wg, cos_t, sin_t, ctab, slo, shi)
    return outs


MASKED = -1e30


HEADS_PER_GROUP = LANES // HEAD_DIM


def _one_head_of_pair(q_pair, hh):
    row = lax.broadcasted_iota(I32, q_pair.shape, 0)
    return jnp.where((row // HEAD_DIM) == hh, q_pair, jnp.zeros_like(q_pair))


def _head_lanes(h):
    g = h // HEADS_PER_GROUP
    return slice(g * LANES, (g + 1) * LANES)


def _head_rows(h):
    return slice(h * HEAD_DIM, (h + 1) * HEAD_DIM)


ATTN_GROUP = 2
ONES_ROWS = 16
ACC_ROWS = HEAD_DIM + ONES_ROWS


def _split_heads(q_all, qh_ref):
    for h in range(N_HEADS):
        qh_ref[h] = _one_head_of_pair(q_all[_head_lanes(h), :], h % HEADS_PER_GROUP)


def _weighted_values(s, m, v_t):
    p = jnp.exp(s - m).astype(BF16)
    lhs = jnp.concatenate([v_t, jnp.ones((ONES_ROWS, v_t.shape[1]), BF16)], axis=0)
    return jnp.dot(lhs, p, preferred_element_type=F32)


def _loop_in_groups(n, body, init, width=4):
    carry = lax.fori_loop(0, n // width, lambda k, c: body(width * k, width, c), init)
    done = (n // width) * width
    part = width // 2
    while part >= 1:
        has_part = ((n - done) // part) % 2 == 1
        carry = lax.cond(has_part, lambda c, done=done, part=part: body(done, part, c), lambda c: c, carry)
        done = done + jnp.where(has_part, part, 0)
        part //= 2
    return carry


def _store_head_output(o_ref, h, acc):
    o_ref[0, 0, _head_rows(h), :] = (acc[:HEAD_DIM, :] / acc[HEAD_DIM:HEAD_DIM + 1, :]).astype(BF16)


def _moba_kernel(q_t_ref, k_ref, v_t_ref, km_ref, o_ref, qh_ref, bias_ref, s_even_ref, s_odd_ref, *acc_refs):
    s_refs = (s_even_ref, s_odd_ref)
    i = pl.program_id(1)
    tq = q_t_ref.shape[3]
    n_blk = km_ref.shape[1]
    _split_heads(q_t_ref[0, 0], qh_ref)

    blk = lax.broadcasted_iota(I32, (n_blk, tq), 0)
    for h in range(N_HEADS):
        gate = jnp.dot(km_ref[0, :, _head_lanes(h)], qh_ref[h].astype(F32), preferred_element_type=F32)
        gate = jnp.where(blk < i, gate, -jnp.inf)
        keep = blk == i
        for _ in range(MOBA_TOPK):
            _, first = _first_index_of_max(gate, blk)
            taken = (blk == first) & (blk < i)
            keep = keep | taken
            gate = jnp.where(taken, -jnp.inf, gate)
        bias_ref[h] = jnp.where(keep, 0.0, MASKED)

    def key_rows(j):
        return pl.ds(pl.multiple_of(j * tq, tq), tq)

    def block_bias(h, j):
        return bias_ref[h, pl.ds(j, 1), :]

    causal = lax.broadcasted_iota(I32, (tq, tq), 0) <= lax.broadcasted_iota(I32, (tq, tq), 1)

    def scores(h, j):
        return jnp.dot(k_ref[0, key_rows(j), _head_lanes(h)], qh_ref[h], preferred_element_type=F32)

    def own_block_scores(g):
        own_max = []
        for gi, h in _group_heads(g):
            s = jnp.where(causal, scores(h, i), MASKED)
            s_refs[g % 2][gi, key_rows(i), :] = s
            own_max.append(jnp.max(s, axis=0, keepdims=True))
        return tuple(own_max)

    def past_scores(g, j, mx):
        for gi, h in _group_heads(g):
            s = scores(h, j)
            s_refs[g % 2][gi, key_rows(j), :] = s
            mx[gi] = jnp.maximum(mx[gi], jnp.max(s, axis=0, keepdims=True) + block_bias(h, j))

    def weighted(g, gi, h, j, mx):
        return _weighted_values(s_refs[g % 2][gi, key_rows(j), :], mx[gi] - block_bias(h, j),
                                v_t_ref[0, j, _head_rows(h), :])

    def finish(g, mx):
        for gi, h in _group_heads(g):
            _store_head_output(o_ref, h, acc_refs[gi][...] + weighted(g, gi, h, i, mx))

    _attention_pipeline(i, own_block_scores, past_scores, weighted, finish, acc_refs)


def _group_heads(g):
    return list(enumerate(range(g * ATTN_GROUP, (g + 1) * ATTN_GROUP)))


def _attention_pipeline(n, first_max, score_step, weighted, finish, acc_refs):
    n_groups = N_HEADS // ATTN_GROUP
    prev_mx = None
    for g in range(n_groups + 1):
        scoring, weighting = g < n_groups, g > 0
        if weighting:
            for acc_ref in acc_refs:
                acc_ref[...] = jnp.zeros(acc_ref.shape, F32)

        def body(j0, count, mx, g=g, scoring=scoring, weighting=weighting, prev_mx=prev_mx):
            mx = list(mx)
            partial = [0.0] * ATTN_GROUP
            for d in range(count):
                if scoring:
                    score_step(g, j0 + d, mx)
                if weighting:
                    for gi, h in _group_heads(g - 1):
                        partial[gi] = partial[gi] + weighted(g - 1, gi, h, j0 + d, prev_mx)
            if weighting:
                for gi, _ in _group_heads(g - 1):
                    acc_refs[gi][...] += partial[gi]
            return tuple(mx)

        mx = _loop_in_groups(n, body, first_max(g) if scoring else ())
        if weighting:
            finish(g - 1, prev_mx)
        prev_mx = mx


def _attention_scratch(s, tq):
    return ([pltpu.VMEM((N_HEADS, LANES, tq), BF16)],
            [pltpu.VMEM((ATTN_GROUP, s, tq), F32)] * 2 + [pltpu.VMEM((ACC_ROWS, tq), F32)] * ATTN_GROUP)


def _resident_spec(shape):
    nd = len(shape)
    return pl.BlockSpec(shape, lambda bi, i: (bi,) + (0,) * (nd - 1), pipeline_mode=pl.Buffered(1))


def _moba_attention(qa_t, ka, va_t, km):
    b, n_blk, w, tq = qa_t.shape
    s = n_blk * tq
    qh, work = _attention_scratch(s, tq)
    return pl.pallas_call(
        _moba_kernel,
        grid=(b, n_blk),
        in_specs=[
            pl.BlockSpec((1, 1, w, tq), lambda bi, i: (bi, i, 0, 0)),
            _resident_spec((1, s, w)),
            _resident_spec((1, n_blk, w, tq)),
            _resident_spec((1, n_blk, w)),
        ],
        out_specs=pl.BlockSpec((1, 1, w, tq), lambda bi, i: (bi, i, 0, 0)),
        out_shape=jax.ShapeDtypeStruct((b, n_blk, w, tq), BF16),
        scratch_shapes=qh + [pltpu.VMEM((N_HEADS, n_blk, tq), F32)] + work,
        compiler_params=_cparams(("parallel", "arbitrary")),
        name="moba_attention",
    )(qa_t, ka, va_t, km)


INT_MIN = -2 ** 31


def _sortable_key(s):
    bits = lax.bitcast_convert_type(s, I32)
    return bits ^ ((bits >> 31) & 0x7FFFFFFF)


def _column_count(mask):
    tk, tq = mask.shape
    return jnp.sum(mask.astype(I32).reshape(tk // SUBLANES, SUBLANES, tq), axis=0)


I16 = jnp.int16
HALF_BITS = 16
I16_MIN = -2 ** (HALF_BITS - 1)
PACKED_ROWS = 2 * SUBLANES
SEARCH_GROUP = 4


def _packed_column_count(mask):
    ones = mask.astype(I16)
    parts = [ones[r:r + PACKED_ROWS, :] for r in range(0, ones.shape[0], PACKED_ROWS)]
    while len(parts) > 1:
        parts = [a + b for a, b in zip(parts[::2], parts[1::2])] + parts[len(parts) - len(parts) % 2:]
    return parts[0]


def _packed_total(count16):
    return jnp.sum(count16.astype(I32), axis=0, keepdims=True)


def _dsa_kernel(iq_t_ref, ik_ref, iw_t_ref, q_t_ref, k_ref, v_t_ref, o_ref,
                key_ref, hi_ref, lo_ref, lo2_ref, bias_ref, qh_ref, s_even_ref, s_odd_ref, *acc_refs,
                topk, index_bits):
    s_refs = (s_even_ref, s_odd_ref)
    t = pl.program_id(1)
    tq = q_t_ref.shape[3]
    tk = tq
    n_chunks = t + 1
    qpos = t * tq + lax.broadcasted_iota(I32, (1, tq), 1)
    krow = lax.broadcasted_iota(I32, (tk, tq), 0)

    def chunk_rows(c):
        return pl.ds(pl.multiple_of(c * tk, tk), tk)

    def total(count8):
        return jnp.sum(count8, axis=0, keepdims=True)

    def count_over_chunks(pred):
        def body(c, cnt):
            return cnt + _column_count(pred(key_ref[chunk_rows(c), :], c * tk + krow))
        return total(lax.fori_loop(0, n_chunks, body, jnp.zeros((SUBLANES, tq), I32)))

    iq = iq_t_ref[0, 0]
    iw = iw_t_ref[0, 0]

    def score_chunks(c0, count, carry):
        for c in (c0 + d for d in range(count)):
            ik_c = ik_ref[0, chunk_rows(c), :]
            score = jnp.zeros((tk, tq), F32)
            for h in range(IDX_HEADS):
                rel = jnp.dot(ik_c, iq[h * IDX_DIM:(h + 1) * IDX_DIM, :], preferred_element_type=F32)
                score = score + jnp.maximum(rel, 0.0) * iw[h:h + 1, :]
            key = jnp.where(c * tk + krow <= qpos, _sortable_key(score), INT_MIN)
            key_ref[chunk_rows(c), :] = key
            hi_ref[chunk_rows(c), :] = (key >> HALF_BITS).astype(I16)
            lo_ref[chunk_rows(c), :] = key.astype(I16) ^ jnp.int16(I16_MIN)
        return carry

    _loop_in_groups(n_chunks, score_chunks, 0)

    n_groups = (n_chunks + SEARCH_GROUP - 1) // SEARCH_GROUP
    for d in range(SEARCH_GROUP - 1):
        @pl.when(n_chunks + d < n_groups * SEARCH_GROUP)
        def _(d=d):
            lowest = jnp.full((tk, tq), I16_MIN, I16)
            hi_ref[chunk_rows(n_chunks + d), :] = lowest
            lo_ref[chunk_rows(n_chunks + d), :] = lowest

    def packed_counts(n_counts, chunk_masks):
        def body(g, cnts):
            cnts = list(cnts)
            for c in (g * SEARCH_GROUP + d for d in range(SEARCH_GROUP)):
                for i, mask in enumerate(chunk_masks(c)):
                    cnts[i] = cnts[i] + _packed_column_count(mask)
            return tuple(cnts)
        zero = jnp.zeros((PACKED_ROWS, tq), I16)
        return [_packed_total(c) for c in lax.fori_loop(0, n_groups, body, (zero,) * n_counts)]

    def kth_largest_half(half_ref, kth):
        def half_bit(b, prefix):
            cand = prefix | jnp.left_shift(jnp.int32(1), HALF_BITS - 1 - b)
            cand_half = (cand + I16_MIN).astype(I16)
            n_ge, = packed_counts(1, lambda c: [half_ref[chunk_rows(c), :] >= cand_half])
            return jnp.where(n_ge >= kth, cand, prefix)

        return lax.fori_loop(0, HALF_BITS, half_bit, jnp.zeros((1, tq), I32)) + I16_MIN

    thr_hi = kth_largest_half(hi_ref, topk)
    thr_hi_half = thr_hi.astype(I16)

    def above_hi(c):
        hi = hi_ref[chunk_rows(c), :]
        lo2_ref[chunk_rows(c), :] = jnp.where(hi == thr_hi_half, lo_ref[chunk_rows(c), :], jnp.int16(I16_MIN))
        return [hi > thr_hi_half]

    n_gt_hi, = packed_counts(1, above_hi)
    thr_lo = kth_largest_half(lo2_ref, topk - n_gt_hi)
    thr_lo_half = thr_lo.astype(I16)
    thr = jnp.left_shift(thr_hi, HALF_BITS) | (thr_lo - I16_MIN)

    n_gt_lo, n_eq = packed_counts(2, lambda c: [
        lo2_ref[chunk_rows(c), :] > thr_lo_half,
        (hi_ref[chunk_rows(c), :] == thr_hi_half) & (lo_ref[chunk_rows(c), :] == thr_lo_half)])
    need = topk - (n_gt_hi + n_gt_lo)
    has_surplus = jnp.any((n_eq > need) & (thr != INT_MIN))

    def last_tie_index():
        def index_bit(b, prefix):
            cand = prefix | jnp.left_shift(jnp.int32(1), index_bits - 1 - b)
            n_before = count_over_chunks(lambda keys, kpos: (keys == thr) & (kpos < cand))
            return jnp.where(n_before < need, cand, prefix)
        return lax.fori_loop(0, index_bits, index_bit, jnp.zeros((1, tq), I32))

    def bias_with_ties():
        tie_end = last_tie_index()

        def bias_chunk(c, carry):
            keys = key_ref[chunk_rows(c), :]
            kpos = c * tk + krow
            chosen = ((keys > thr) | ((keys == thr) & (kpos <= tie_end))) & (kpos <= qpos)
            bias_ref[chunk_rows(c), :] = jnp.where(chosen, 0.0, MASKED)
            return carry

        lax.fori_loop(0, n_chunks, bias_chunk, 0)

    def bias_without_ties():
        floor = jnp.maximum(thr, INT_MIN + 1)

        def bias_chunks(c0, count, carry):
            for c in (c0 + d for d in range(count)):
                bias_ref[chunk_rows(c), :] = jnp.where(key_ref[chunk_rows(c), :] >= floor, 0.0, MASKED)
            return carry

        _loop_in_groups(n_chunks, bias_chunks, 0)

    lax.cond(has_surplus, bias_with_ties, bias_without_ties)

    _split_heads(q_t_ref[0, 0], qh_ref)

    def masked_scores(g, c, mx):
        for gi, h in _group_heads(g):
            s = jnp.dot(k_ref[0, chunk_rows(c), _head_lanes(h)], qh_ref[h], preferred_element_type=F32)
            s = s + bias_ref[chunk_rows(c), :]
            s_refs[g % 2][gi, chunk_rows(c), :] = s
            mx[gi] = jnp.maximum(mx[gi], jnp.max(s, axis=0, keepdims=True))

    def weighted(g, gi, h, c, mx):
        return _weighted_values(s_refs[g % 2][gi, chunk_rows(c), :], mx[gi], v_t_ref[0, c, _head_rows(h), :])

    def finish(g, mx):
        for gi, h in _group_heads(g):
            _store_head_output(o_ref, h, acc_refs[gi][...])

    _attention_pipeline(n_chunks, lambda g: (jnp.full((1, tq), MASKED, F32),) * ATTN_GROUP,
                        masked_scores, weighted, finish, acc_refs)


def _dsa_attention(iq_t, ik, iw_t, qb_t, kb, vb_t):
    b, n_blk, w, tq = qb_t.shape
    s = n_blk * tq
    topk = min(DSA_TOPK_MAX, s // 4)
    assert n_blk % SEARCH_GROUP == 0, "the packed search pads its chunk range to whole groups"
    qh, work = _attention_scratch(s, tq)
    tile_spec = lambda rows: pl.BlockSpec((1, 1, rows, tq), lambda bi, ti: (bi, ti, 0, 0))
    return pl.pallas_call(
        functools.partial(_dsa_kernel, topk=topk, index_bits=max(1, (s - 1).bit_length())),
        grid=(b, n_blk),
        in_specs=[
            tile_spec(IDX_HEADS * IDX_DIM),
            _resident_spec((1, s, IDX_DIM)),
            tile_spec(IDX_HEADS),
            tile_spec(w),
            _resident_spec((1, s, w)),
            _resident_spec((1, n_blk, w, tq)),
        ],
        out_specs=pl.BlockSpec((1, 1, w, tq), lambda bi, ti: (bi, ti, 0, 0)),
        out_shape=jax.ShapeDtypeStruct((b, n_blk, w, tq), BF16),
        scratch_shapes=[pltpu.VMEM((s, tq), I32)] + [pltpu.VMEM((s, tq), I16)] * 3
        + [pltpu.VMEM((s, tq), F32)] + qh + work,
        compiler_params=_cparams(("parallel", "arbitrary")),
        name="dsa_attention",
    )(iq_t, ik, iw_t, qb_t, kb, vb_t)


EXPERT_ROW0 = SUBLANES
ROUTER_ROWS = EXPERT_ROW0 + N_EXPERTS
EXPERT_TOPK = 2


def _rms_norm(x, g):
    ms = jnp.mean(x * x, axis=-1, keepdims=True)
    return (x * lax.rsqrt(ms + RMS_EPS)) * g


HIGH_HALF = -2 ** 16


def _pack_bf16_halves(x):
    bits = lax.bitcast_convert_type(x.astype(BF16).astype(F32), I32)
    half = x.shape[1] // 2
    return (bits[:, :half] & HIGH_HALF) | lax.shift_right_logical(bits[:, half:], 16)


def _unpack_bf16_halves(packed):
    upper = lax.bitcast_convert_type(packed & HIGH_HALF, F32)
    lower = lax.bitcast_convert_type(lax.shift_left(packed, 16), F32)
    return jnp.concatenate([upper, lower], axis=1).astype(BF16)


def _first_index_of_max(vals, idx):
    top = jnp.max(vals, axis=0, keepdims=True)
    first = jnp.min(jnp.where(vals == top, idx, vals.shape[0]), axis=0, keepdims=True)
    return top, first


def _merge_router_kernel(oa_ref, ob_ref, ga_ref, gb_ref, x_ref, wa_ref, wb_ref, wo_ref, g_ref,
                         wr_ref, br_ref, x1_ref, h2_ref, ids_ref, cw_ref, rank_ref, cnt_ref):
    @pl.when(pl.program_id(0) == 0)
    def _():
        cnt_ref[...] = jnp.zeros_like(cnt_ref)

    def branch(o_t_ref, w_ref):
        return jnp.concatenate([lax.dot_general(o_t_ref[c], w_ref[...], TN_DIMS, preferred_element_type=F32)
                                for c in range(o_t_ref.shape[0])], axis=0)

    a = branch(oa_ref, wa_ref)
    b = branch(ob_ref, wb_ref)
    merged = jax.nn.sigmoid(ga_ref[...]) * a + jax.nn.sigmoid(gb_ref[...]) * b
    x1 = x_ref[...] + jnp.dot(merged.astype(BF16), wo_ref[...], preferred_element_type=F32)
    x1_ref[...] = x1
    h2 = _rms_norm(x1, g_ref[...])
    h2_ref[...] = _pack_bf16_halves(h2)
    tm = h2.shape[0]

    logits = lax.dot_general(wr_ref[...], h2, NT_DIMS, preferred_element_type=F32,
                             precision=lax.Precision.HIGHEST) + br_ref[...]
    grp = logits[0:N_GROUPS, :]
    g_top, g_idx = _first_index_of_max(grp, lax.broadcasted_iota(I32, grp.shape, 0))
    p_grp = 1.0 / jnp.sum(jnp.exp(grp - g_top), axis=0, keepdims=True)
    in_grp = logits[EXPERT_ROW0:EXPERT_ROW0 + EXPERTS_PER_GROUP, :]
    for gi in range(1, N_GROUPS):
        rows = slice(EXPERT_ROW0 + gi * EXPERTS_PER_GROUP, EXPERT_ROW0 + (gi + 1) * EXPERTS_PER_GROUP)
        in_grp = jnp.where(g_idx == gi, logits[rows, :], in_grp)
    e_iota = lax.broadcasted_iota(I32, in_grp.shape, 0)
    v0, i0 = _first_index_of_max(in_grp, e_iota)
    v1, i1 = _first_index_of_max(jnp.where(e_iota == i0, -jnp.inf, in_grp), e_iota)
    e1 = jnp.exp(v1 - v0)
    denom = 1.0 + e1
    ids = jnp.concatenate([g_idx * EXPERTS_PER_GROUP + i0, g_idx * EXPERTS_PER_GROUP + i1], axis=0)
    ids_ref[0] = ids
    cw_ref[0] = jnp.concatenate([p_grp * (1.0 / denom), p_grp * (e1 / denom)], axis=0)

    before = (lax.broadcasted_iota(I32, (tm, tm), 0) < lax.broadcasted_iota(I32, (tm, tm), 1)).astype(BF16)
    expert = lax.broadcasted_iota(I32, (N_EXPERTS, tm), 0)
    ranks = []
    for slot in range(EXPERT_TOPK):
        onehot = expert == ids[slot:slot + 1, :]
        seen = jnp.dot(onehot.astype(BF16), before, preferred_element_type=F32) + cnt_ref[...]
        ranks.append(jnp.sum(jnp.where(onehot, seen, 0.0), axis=0, keepdims=True))
        cnt_ref[...] += jnp.sum(onehot.astype(F32), axis=1, keepdims=True)
    rank_ref[0] = jnp.concatenate(ranks, axis=0).astype(I32)


def _merge_and_route(o_a, o_b, ga, gb, x, w_br_a, w_br_b, w_out, g_ffn, w_grp, b_grp, w_rt, b_rt, tm=512):
    t, d = x.shape
    n_tiles = t // tm
    wr = jnp.zeros((ROUTER_ROWS, d), F32).at[0:N_GROUPS].set(w_grp.T).at[EXPERT_ROW0:].set(w_rt.T)
    br = jnp.zeros((ROUTER_ROWS, 1), F32).at[0:N_GROUPS, 0].set(b_grp).at[EXPERT_ROW0:, 0].set(b_rt)
    row_spec = lambda w: pl.BlockSpec((tm, w), lambda i: (i, 0))
    mixer_spec = pl.BlockSpec((tm // Q_TILE, MIXER_W, Q_TILE), lambda i: (i, 0, 0))
    slot_spec = pl.BlockSpec((1, EXPERT_TOPK, tm), lambda i: (i, 0, 0))
    slot_shape = lambda dt: jax.ShapeDtypeStruct((n_tiles, EXPERT_TOPK, tm), dt)
    return pl.pallas_call(
        _merge_router_kernel,
        grid=(n_tiles,),
        in_specs=[mixer_spec, mixer_spec, row_spec(d), row_spec(d), row_spec(d),
                  _const_spec((MIXER_W, d)), _const_spec((MIXER_W, d)), _const_spec((d, d)),
                  _const_spec((1, d)), _const_spec((ROUTER_ROWS, d)), _const_spec((ROUTER_ROWS, 1))],
        out_specs=[row_spec(d), row_spec(d // 2), slot_spec, slot_spec, slot_spec,
                   pl.BlockSpec((N_EXPERTS, 1), lambda i: (0, 0))],
        out_shape=[jax.ShapeDtypeStruct((t, d), F32), jax.ShapeDtypeStruct((t, d // 2), I32),
                   slot_shape(I32), slot_shape(F32), slot_shape(I32),
                   jax.ShapeDtypeStruct((N_EXPERTS, 1), F32)],
        compiler_params=_cparams(("arbitrary",)),
        name="merge_and_route",
    )(o_a, o_b, ga, gb, x, w_br_a.astype(BF16), w_br_b.astype(BF16), w_out.astype(BF16),
      g_ffn.reshape(1, d), wr, br)


EXPERT_ROWS = 512
DMA_ISSUE_UNROLL = 8


def _row(ref, r):
    return ref.at[pl.ds(r, 1), :]


def _tile_dest(dest_ref, tm, slot, r):
    return dest_ref[(pl.program_id(0) * EXPERT_TOPK + slot) * tm + r]


def _dispatch_kernel(dest_ref, pad_block_ref, h2_ref, xs_hbm, zero_ref, sem, zero_sem):
    tm = h2_ref.shape[0]

    @pl.when(pl.program_id(0) == 0)
    def _():
        zero_ref[...] = jnp.zeros(zero_ref.shape, zero_ref.dtype)

        def block_copy(b):
            rows = pl.ds(pl.multiple_of(b * EXPERT_ROWS, EXPERT_ROWS), EXPERT_ROWS)
            return pltpu.make_async_copy(zero_ref, xs_hbm.at[rows, :], zero_sem)

        def start(b, n):
            @pl.when(pad_block_ref[b] != 0)
            def _():
                block_copy(b).start()
            return n + pad_block_ref[b]

        n_started = lax.fori_loop(0, pad_block_ref.shape[0], start, 0)

        def drain(k, carry):
            block_copy(0).wait()
            return carry

        lax.fori_loop(0, n_started, drain, 0)

    for r in range(tm):
        for slot in range(EXPERT_TOPK):
            pltpu.make_async_copy(_row(h2_ref, r), _row(xs_hbm, _tile_dest(dest_ref, tm, slot, r)),
                                  sem).start(priority=slot % 2)
    for slot in range(EXPERT_TOPK):
        pltpu.make_async_copy(h2_ref, xs_hbm.at[pl.ds(0, tm), :], sem).wait()


def _dispatch(dest, pad_block, h2, tm):
    t, d = h2.shape
    n_rows = pad_block.shape[0] * EXPERT_ROWS
    return pl.pallas_call(
        _dispatch_kernel,
        grid_spec=pltpu.PrefetchScalarGridSpec(
            num_scalar_prefetch=2,
            grid=(t // tm,),
            in_specs=[pl.BlockSpec((tm, d), lambda i, dest, pad: (i, 0))],
            out_specs=pl.BlockSpec(memory_space=pl.ANY),
            scratch_shapes=[pltpu.VMEM((EXPERT_ROWS, d), h2.dtype), pltpu.SemaphoreType.DMA(()),
                            pltpu.SemaphoreType.DMA(())],
        ),
        out_shape=jax.ShapeDtypeStruct((n_rows, d), h2.dtype),
        compiler_params=_cparams(("arbitrary",)),
        name="moe_dispatch",
    )(dest, pad_block, h2)


def _expert_kernel(blk_exp_ref, n_used_ref, xs_ref, wg_ref, wu_ref, wd_ref, ys_ref, wg16, wu16, wd16):
    i = pl.program_id(0)
    used = i < n_used_ref[0]

    @pl.when((i == 0) | (blk_exp_ref[i] != blk_exp_ref[jnp.maximum(i - 1, 0)]))
    def _():
        wg16[...] = wg_ref[0].astype(BF16)
        wu16[...] = wu_ref[0].astype(BF16)
        wd16[...] = wd_ref[0].astype(BF16)

    @pl.when(used)
    def _():
        xb = _unpack_bf16_halves(xs_ref[...])
        gate = jnp.dot(xb, wg16[...], preferred_element_type=F32)
        up = jnp.dot(xb, wu16[...], preferred_element_type=F32)
        hid = (gate * jax.nn.sigmoid(gate)) * up
        ys_ref[...] = jnp.dot(hid.astype(BF16), wd16[...], preferred_element_type=F32)

    @pl.when(jnp.logical_not(used))
    def _():
        ys_ref[...] = jnp.zeros_like(ys_ref)


def _experts(blk_exp, n_used, xs, w_gate, w_up, w_down):
    n_rows, d = xs.shape[0], w_gate.shape[1]
    n_blocks = n_rows // EXPERT_ROWS
    expert_spec = lambda rows, cols: pl.BlockSpec((1, rows, cols), lambda i, be, nu: (be[i], 0, 0))
    return pl.pallas_call(
        _expert_kernel,
        grid_spec=pltpu.PrefetchScalarGridSpec(
            num_scalar_prefetch=2,
            grid=(n_blocks,),
            in_specs=[pl.BlockSpec((EXPERT_ROWS, xs.shape[1]), lambda i, be, nu: (i, 0)),
                      expert_spec(d, D_EXPERT), expert_spec(d, D_EXPERT), expert_spec(D_EXPERT, d)],
            out_specs=pl.BlockSpec((EXPERT_ROWS, d), lambda i, be, nu: (i, 0)),
            scratch_shapes=[pltpu.VMEM((d, D_EXPERT), BF16), pltpu.VMEM((d, D_EXPERT), BF16),
                            pltpu.VMEM((D_EXPERT, d), BF16)],
        ),
        out_shape=jax.ShapeDtypeStruct((n_rows, d), F32),
        compiler_params=_cparams(("arbitrary",)),
        name="moe_experts",
    )(blk_exp, n_used, xs, w_gate, w_up, w_down)


def _combine_ple_kernel(dest_ref, x1_ref, cw_ref, p_ref, ys_hbm, g_ple_ref, wpg_ref, wpp_ref, g_fin_ref,
                        out_ref, y_even, y_odd, sem):
    tm = y_even.shape[1]
    i = pl.program_id(0)
    last_tile = 2 * pl.num_programs(0) - 1
    bufs = ((y_even, 0), (y_odd, 1))

    def start_row(tile, buf, sem_slot, r):
        for slot in range(EXPERT_TOPK):
            src = _row(ys_hbm, dest_ref[(tile * EXPERT_TOPK + slot) * tm + r])
            pltpu.make_async_copy(src, _row(buf.at[slot], r), sem.at[sem_slot]).start(priority=slot % 2)

    def wait_tile(buf, sem_slot):
        for slot in range(EXPERT_TOPK):
            pltpu.make_async_copy(ys_hbm.at[pl.ds(0, tm), :], buf.at[slot], sem.at[sem_slot]).wait()

    def combine(half, buf):
        rows = slice(half * tm, (half + 1) * tm)
        cw = cw_ref[rows, :]
        x2 = x1_ref[rows, :] + (buf[0] * cw[:, 0:1] + buf[1] * cw[:, 1:2])
        h3 = _rms_norm(x2, g_ple_ref[...]).astype(BF16)
        gate = jax.nn.sigmoid(jnp.dot(h3, wpg_ref[...], preferred_element_type=F32))
        proj = jnp.dot(p_ref[rows, :].astype(BF16), wpp_ref[...], preferred_element_type=F32)
        out_ref[rows, :] = _rms_norm(x2 + gate * proj, g_fin_ref[...])

    @pl.when(i == 0)
    def _():
        def issue(r, carry):
            start_row(0, *bufs[0], r)
            return carry
        lax.fori_loop(0, tm, issue, 0, unroll=DMA_ISSUE_UNROLL)

    for half in range(2):
        buf, sem_slot = bufs[half]
        wait_tile(buf, sem_slot)
        combine(half, buf)
        next_tile = jnp.minimum(2 * i + half + 1, last_tile)
        for r in range(tm):
            start_row(next_tile, *bufs[1 - half], r)

    @pl.when(i == pl.num_programs(0) - 1)
    def _():
        wait_tile(*bufs[0])


def _combine_ple(dest, x1, cw_tok, p, ys, g_ple, w_ple_gate, w_ple_proj, g_final, tm):
    t, d = x1.shape
    assert t % (2 * tm) == 0
    row_spec = lambda w: pl.BlockSpec((2 * tm, w), lambda i, dest: (i, 0))
    return pl.pallas_call(
        _combine_ple_kernel,
        grid_spec=pltpu.PrefetchScalarGridSpec(
            num_scalar_prefetch=1,
            grid=(t // (2 * tm),),
            in_specs=[row_spec(d), row_spec(EXPERT_TOPK), row_spec(p.shape[1]),
                      pl.BlockSpec(memory_space=pl.ANY),
                      _const_spec((1, d)), _const_spec((d, d)), _const_spec((p.shape[1], d)), _const_spec((1, d))],
            out_specs=row_spec(d),
            scratch_shapes=[pltpu.VMEM((EXPERT_TOPK, tm, d), F32), pltpu.VMEM((EXPERT_TOPK, tm, d), F32),
                            pltpu.SemaphoreType.DMA((2,))],
        ),
        out_shape=jax.ShapeDtypeStruct((t, d), F32),
        compiler_params=_cparams(("arbitrary",)),
        name="combine_ple",
    )(dest, x1, cw_tok, p, ys, g_ple.reshape(1, d), w_ple_gate.astype(BF16), w_ple_proj.astype(BF16),
      g_final.reshape(1, d))


def _layer(x, p, g_attn, w_in, w_br_a, w_br_b, w_out, g_ffn, w_grp, b_grp, w_rt, b_rt,
           w_gate, w_up, w_down, g_ple, w_ple_gate, w_ple_proj, g_final):
    b, s, d = x.shape
    t = b * s
    (qa_t, va_t, qb_t, vb_t, iq_t, iw_t, ka, kb, ik, ga, gb, km) = _in_projection(x, g_attn, w_in)
    o_a = _moba_attention(qa_t, ka, va_t, km.reshape(b, s // MOBA_BLOCK, MIXER_W))
    o_b = _dsa_attention(iq_t, ik, iw_t, qb_t, kb, vb_t)
    x1, h2, ids, cw, rank, counts = _merge_and_route(
        o_a.reshape(t // Q_TILE, MIXER_W, Q_TILE), o_b.reshape(t // Q_TILE, MIXER_W, Q_TILE),
        ga.reshape(t, d), gb.reshape(t, d),
        x.reshape(t, d), w_br_a, w_br_b, w_out, g_ffn, w_grp, b_grp, w_rt, b_rt)

    counts = counts[:, 0].astype(I32)
    padded = ((counts + EXPERT_ROWS - 1) // EXPERT_ROWS) * EXPERT_ROWS
    pend = jnp.cumsum(padded)
    pstart = pend - padded
    n_blocks = -(-(t * EXPERT_TOPK) // EXPERT_ROWS) + N_EXPERTS
    experts = jnp.arange(N_EXPERTS, dtype=I32)
    dest = jnp.sum(jnp.where(ids[..., None] == experts, pstart, 0), axis=-1) + rank
    block_row0 = jnp.arange(n_blocks, dtype=I32) * EXPERT_ROWS
    blk_exp = jnp.minimum(jnp.sum((pend[None, :] <= block_row0[:, None]).astype(I32), axis=1), N_EXPERTS - 1)
    n_used = (pend[-1:] // EXPERT_ROWS).astype(I32)

    route_tile = dest.shape[2]
    dest = dest.reshape(-1)
    is_last_of_expert = jnp.any(pend[None, :] == (block_row0 + EXPERT_ROWS)[:, None], axis=1)
    pad_block = (is_last_of_expert | (block_row0 >= pend[-1])).astype(I32)
    xs = _dispatch(dest, pad_block, h2, route_tile)
    ys = _experts(blk_exp, n_used, xs, w_gate, w_up, w_down)
    cw_tok = jnp.swapaxes(cw, 1, 2).reshape(t, EXPERT_TOPK)
    out = _combine_ple(dest, x1, cw_tok, p.reshape(t, p.shape[-1]), ys, g_ple, w_ple_gate, w_ple_proj, g_final,
                       route_tile)
    return out.reshape(b, s, d)


def kernel(x, p, g_attn, w_in, w_br_a, w_br_b, w_out, g_ffn, w_grp, b_grp, w_rt, b_rt, w_gate, w_up, w_down, g_ple, w_ple_gate, w_ple_proj, g_final):
    depth = w_in.shape[0]
    assert depth == 1, "the final RMSNorm is fused into the last layer's kernel"
    i = 0
    return _layer(x, p[i], g_attn[i], w_in[i], w_br_a[i], w_br_b[i], w_out[i], g_ffn[i], w_grp[i], b_grp[i],
                  w_rt[i], b_rt[i], w_gate[i], w_up[i], w_down[i], g_ple[i], w_ple_gate[i], w_ple_proj[i], g_final)
```
